```python
import jax, jax.numpy as jnp
from jax import lax
import numpy as np

D_MODEL = 2048
BATCH = 2
SEQ = 4096
DEPTH = 4

RW_HEAD_DIM = 64
RW_HEADS = 16
RW_WIDTH = RW_HEADS * RW_HEAD_DIM
RW_DECAY_LORA = 96
RW_ICLR_LORA = 96
RW_GATE_LORA = 256
RW_COLS = 3 * RW_WIDTH + RW_DECAY_LORA + RW_ICLR_LORA + RW_GATE_LORA
RW_SPLITS = (RW_WIDTH, 2 * RW_WIDTH, 3 * RW_WIDTH, 3 * RW_WIDTH + RW_DECAY_LORA,
             3 * RW_WIDTH + RW_DECAY_LORA + RW_ICLR_LORA)
RW_GN_EPS = 64e-5

SG_CHUNK = 128
SG_GROUPS = 16
SG_GROUP_DIM = 64
SG_WIDTH = SG_GROUPS * SG_GROUP_DIM
SG_COLS = 2 * SG_WIDTH

MLA_HEADS = 8
MLA_Q_LORA = 512
MLA_KV_LORA = 256
MLA_QK_NOPE = 128
MLA_QK_ROPE = 64
MLA_QK_DIM = MLA_QK_NOPE + MLA_QK_ROPE
MLA_V_DIM = 128
MLA_WIDTH = MLA_HEADS * MLA_V_DIM
MLA_COLS = MLA_Q_LORA + MLA_KV_LORA + MLA_QK_ROPE
ROPE_BASE = 10000.0
Q_BLOCK = 128

N_BRANCHES = 3
GATE_COLS = N_BRANCHES * D_MODEL
IN_SPLITS = (RW_COLS, RW_COLS + SG_COLS, RW_COLS + SG_COLS + MLA_COLS)
IN_COLS = RW_COLS + SG_COLS + MLA_COLS + GATE_COLS

MOE_GROUPS = 8
MOE_EXPERTS_PER_GROUP = 8
MOE_EXPERTS = MOE_GROUPS * MOE_EXPERTS_PER_GROUP
MOE_TOP_K = 2
MOE_D_FF = 384
MOE_BLOCK = 128

NORM_EPS = 1e-6

kernel_name = 'hybrid_rwkv7_sgu_mla_hmoe_trunk'


def rms_norm(x, g):
    xf = x.astype(jnp.float32)
    y = xf * lax.rsqrt(jnp.mean(xf * xf, axis=-1, keepdims=True) + NORM_EPS)
    return (y * g.astype(jnp.float32)).astype(x.dtype)


def layer_norm(x, g, b):
    xf = x.astype(jnp.float32)
    mu = jnp.mean(xf, axis=-1, keepdims=True)
    var = jnp.mean((xf - mu) ** 2, axis=-1, keepdims=True)
    y = (xf - mu) * lax.rsqrt(var + 1e-5) * g.astype(jnp.float32) + b.astype(jnp.float32)
    return y.astype(x.dtype)


def rope_tables(positions):
    half = MLA_QK_ROPE // 2
    inv_freq = ROPE_BASE ** (-jnp.arange(half, dtype=jnp.float32) / half)
    ang = positions.astype(jnp.float32)[..., None] * inv_freq
    return jnp.cos(ang), jnp.sin(ang)


def apply_rope(x, cos, sin):
    half = x.shape[-1] // 2
    x1, x2 = x[..., :half], x[..., half:]
    cos, sin = cos.astype(x.dtype), sin.astype(x.dtype)
    return jnp.concatenate([x1 * cos - x2 * sin, x1 * sin + x2 * cos], axis=-1)


def rwkv7_recurrence(r, decay, k, v, kk, a):
    b, _, h, n = r.shape

    def step(state, inp):
        r_t, w_t, k_t, v_t, kk_t, a_t = inp
        s_kk = jnp.einsum('bhvk,bhk->bhv', state, kk_t)
        state = (state * w_t[:, :, None, :]
                 - s_kk[..., None] * (kk_t * a_t)[:, :, None, :]
                 + v_t[..., None] * k_t[:, :, None, :])
        return state, jnp.einsum('bhvk,bhk->bhv', state, r_t)

    xs = tuple(jnp.swapaxes(t, 0, 1) for t in (r, decay, k, v, kk, a))
    state0 = jnp.zeros((b, h, n, n), jnp.float32)
    _, out = lax.scan(step, state0, xs)
    return jnp.swapaxes(out, 0, 1)


def rwkv7_branch(p, mu, w0, w_w2, a0, w_a2, w_g2, k_k, k_a, r_k, ln_g, ln_b):
    b, s, _ = p.shape
    f32 = jnp.float32
    p_prev = jnp.pad(p[:, :-1], ((0, 0), (1, 0), (0, 0)))
    p = p + mu * (p_prev - p)
    r, k, v, w_lo, a_lo, g_lo = jnp.split(p, RW_SPLITS, axis=-1)
    log_w = -jax.nn.softplus(-(w0 + jnp.tanh(w_lo) @ w_w2).astype(f32)) - 0.5
    decay = jnp.exp(-jnp.exp(log_w))
    a = jax.nn.sigmoid((a0 + a_lo @ w_a2).astype(f32))
    g = jax.nn.sigmoid(g_lo) @ w_g2

    def heads(t):
        return t.astype(f32).reshape(b, s, RW_HEADS, RW_HEAD_DIM)

    kk = heads(k * k_k)
    kk = kk / jnp.maximum(jnp.linalg.norm(kk, axis=-1, keepdims=True), 1e-12)
    k_mod = k.astype(f32) * (1.0 + (a - 1.0) * k_a.astype(f32))
    r_h, k_h, v_h = heads(r), heads(k_mod), heads(v)
    o = rwkv7_recurrence(r_h, heads(decay), k_h, v_h, kk, heads(a))
    mean = jnp.mean(o, axis=-1, keepdims=True)
    var = jnp.mean((o - mean) ** 2, axis=-1, keepdims=True)
    o = ((o - mean) * lax.rsqrt(var + RW_GN_EPS)).reshape(b, s, RW_WIDTH)
    o = o * ln_g.astype(f32) + ln_b.astype(f32)
    bonus = jnp.sum(r_h * k_h * r_k.astype(f32), axis=-1, keepdims=True) * v_h
    o = o + bonus.reshape(b, s, RW_WIDTH)
    return o.astype(p.dtype) * g


def spatial_gating_branch(p, ln_g, ln_b, w_s, b_s):
    b, s, _ = p.shape
    u, v = jnp.split(jax.nn.gelu(p), 2, axis=-1)
    v = layer_norm(v, ln_g, ln_b)
    v = v.reshape(b, s // SG_CHUNK, SG_CHUNK, SG_GROUPS, SG_GROUP_DIM)
    w_causal = jnp.tril(w_s)
    mixed = jnp.einsum('gts,bcsgd->bctgd', w_causal, v) + b_s.T[:, :, None]
    return u * mixed.reshape(b, s, SG_WIDTH)


def mla_branch(p, cos, sin, q_norm_g, w_uq, kv_norm_g, w_ukv):
    b, s, _ = p.shape
    c_q, c_kv, k_rope = jnp.split(p, (MLA_Q_LORA, MLA_Q_LORA + MLA_KV_LORA), axis=-1)
    q = (rms_norm(c_q, q_norm_g) @ w_uq).reshape(b, s, MLA_HEADS, MLA_QK_DIM)
    q_nope = q[..., :MLA_QK_NOPE]
    q_rope = apply_rope(q[..., MLA_QK_NOPE:], cos[:, :, None, :], sin[:, :, None, :])
    kv = (rms_norm(c_kv, kv_norm_g) @ w_ukv).reshape(b, s, MLA_HEADS, MLA_QK_NOPE + MLA_V_DIM)
    k_nope, v = kv[..., :MLA_QK_NOPE], kv[..., MLA_QK_NOPE:]
    k_rope = apply_rope(k_rope, cos, sin)
    scale = MLA_QK_DIM ** -0.5
    n_blk = s // Q_BLOCK
    qn_b = q_nope.reshape(b, n_blk, Q_BLOCK, MLA_HEADS, MLA_QK_NOPE).swapaxes(0, 1)
    qr_b = q_rope.reshape(b, n_blk, Q_BLOCK, MLA_HEADS, MLA_QK_ROPE).swapaxes(0, 1)
    key_pos = jnp.arange(s)

    def attend(args):
        qn, qr, blk = args
        scores = (jnp.einsum('bqhd,bkhd->bhqk', qn, k_nope)
                  + jnp.einsum('bqhd,bkd->bhqk', qr, k_rope)).astype(jnp.float32) * scale
        q_pos = blk * Q_BLOCK + jnp.arange(Q_BLOCK)
        scores = jnp.where(key_pos[None, :] <= q_pos[:, None], scores, -jnp.inf)
        probs = jax.nn.softmax(scores, axis=-1).astype(v.dtype)
        return jnp.einsum('bhqk,bkhd->bqhd', probs, v)

    o = lax.map(attend, (qn_b, qr_b, jnp.arange(n_blk)))
    return o.swapaxes(0, 1).reshape(b, s, MLA_WIDTH)


def token_mixer(h, cos, sin, w_in, rw_mu, rw_w0, rw_w2, rw_a0, rw_a2, rw_g2, rw_k_k, rw_k_a,
                rw_r_k, rw_ln_g, rw_ln_b, sg_ln_g, sg_ln_b, sg_w, sg_b, mla_q_norm_g,
                mla_w_uq, mla_kv_norm_g, mla_w_ukv, p_rwkv, p_sgu, p_mla, w_o):
    proj = h @ w_in
    p_rw, p_sg, p_at, p_gate = jnp.split(proj, IN_SPLITS, axis=-1)
    y_rw = rwkv7_branch(p_rw, rw_mu, rw_w0, rw_w2, rw_a0, rw_a2, rw_g2, rw_k_k, rw_k_a,
                        rw_r_k, rw_ln_g, rw_ln_b) @ p_rwkv
    y_sg = spatial_gating_branch(p_sg, sg_ln_g, sg_ln_b, sg_w, sg_b) @ p_sgu
    y_at = mla_branch(p_at, cos, sin, mla_q_norm_g, mla_w_uq, mla_kv_norm_g, mla_w_ukv) @ p_mla
    g_rw, g_sg, g_at = jnp.split(jax.nn.sigmoid(p_gate), N_BRANCHES, axis=-1)
    return (g_rw * y_rw + g_sg * y_sg + g_at * y_at) @ w_o


def hierarchical_moe(h, g_w, g_b, e_w, e_b, w_gate, w_up, w_down):
    b, s, d = h.shape
    t = b * s
    f32 = jnp.float32
    hf = h.reshape(t, d)
    group_logits = (hf @ g_w + g_b).astype(f32)
    group = jnp.argmax(group_logits, axis=-1)
    group_w = jnp.take_along_axis(jax.nn.softmax(group_logits, axis=-1), group[:, None], axis=-1)
    exp_logits = (hf @ e_w + e_b).astype(f32).reshape(t, MOE_GROUPS, MOE_EXPERTS_PER_GROUP)
    in_group = jnp.take_along_axis(exp_logits, group[:, None, None], axis=1)[:, 0]
    top_logit, top_idx = lax.top_k(in_group, MOE_TOP_K)
    weights = (group_w * jax.nn.softmax(top_logit, axis=-1)).reshape(-1)
    expert_ids = (group[:, None] * MOE_EXPERTS_PER_GROUP + top_idx).reshape(-1).astype(jnp.int32)
    token_ids = jnp.repeat(jnp.arange(t, dtype=jnp.int32), MOE_TOP_K)
    n_assign = t * MOE_TOP_K
    n_blocks = -(-n_assign // MOE_BLOCK) + MOE_EXPERTS
    rows = n_blocks * MOE_BLOCK
    order = jnp.argsort(expert_ids)
    sorted_e = expert_ids[order]
    counts = jnp.bincount(expert_ids, length=MOE_EXPERTS)
    padded = (counts + MOE_BLOCK - 1) // MOE_BLOCK * MOE_BLOCK
    ends = jnp.cumsum(padded)
    starts = ends - padded
    first = jnp.cumsum(counts) - counts
    dest = starts[sorted_e] + jnp.arange(n_assign, dtype=jnp.int32) - first[sorted_e]
    row_token = jnp.full((rows,), t, jnp.int32).at[dest].set(token_ids[order])
    row_weight = jnp.zeros((rows,), h.dtype).at[dest].set(weights[order].astype(h.dtype))
    block_start = jnp.arange(n_blocks, dtype=jnp.int32) * MOE_BLOCK
    block_expert = jnp.minimum(jnp.searchsorted(ends, block_start, side='right'), MOE_EXPERTS - 1)
    h_pad = jnp.concatenate([hf, jnp.zeros((1, d), hf.dtype)], axis=0)
    xb = h_pad[row_token].reshape(n_blocks, MOE_BLOCK, d)

    def expert_block(args):
        xe, e = args
        return (jax.nn.silu(xe @ w_gate[e]) * (xe @ w_up[e])) @ w_down[e]

    yb = lax.map(expert_block, (xb, block_expert)).reshape(rows, d)
    y = jnp.zeros((t + 1, d), h.dtype).at[row_token].add(yb * row_weight[:, None])
    return y[:t].reshape(b, s, d)


def setup_inputs(seed: int = 0) -> dict:
    key = jax.random.key(seed)
    k = jax.random.split(key, 39)
    L, D = DEPTH, D_MODEL
    f32 = jnp.float32

    def nrm(i, shape, std):
        return std * jax.random.normal(k[i], shape, f32)

    def gain(i, shape):
        return 1.0 + nrm(i, shape, 0.1)

    return {
        'x': nrm(0, (BATCH, SEQ, D), 1.0),
        'c': nrm(1, (BATCH, D), 1.0),
        'positions': jnp.arange(SEQ, dtype=jnp.int32)[None, :]
                     + jax.random.randint(k[2], (BATCH, 1), 0, 1024, dtype=jnp.int32),
        'ada_w': nrm(3, (L, D, 6 * D), 0.5 * D ** -0.5),
        'ada_b': nrm(4, (L, 6 * D), 0.02),
        'norm1_g': gain(5, (L, D)),
        'norm2_g': gain(6, (L, D)),
        'final_g': gain(7, (D,)),
        'w_in': nrm(8, (L, D, IN_COLS), D ** -0.5),
        'rw_mu': jax.random.uniform(k[9], (L, RW_COLS), f32),
        'rw_w0': -1.0 + nrm(10, (L, RW_WIDTH), 0.5),
        'rw_w2': nrm(11, (L, RW_DECAY_LORA, RW_WIDTH), 0.5 * RW_DECAY_LORA ** -0.5),
        'rw_a0': nrm(12, (L, RW_WIDTH), 0.1),
        'rw_a2': nrm(13, (L, RW_ICLR_LORA, RW_WIDTH), 0.5 * RW_ICLR_LORA ** -0.5),
        'rw_g2': nrm(14, (L, RW_GATE_LORA, RW_WIDTH), RW_GATE_LORA ** -0.5),
        'rw_k_k': 0.85 + nrm(15, (L, RW_WIDTH), 0.1),
        'rw_k_a': gain(16, (L, RW_WIDTH)),
        'rw_r_k': nrm(17, (L, RW_HEADS, RW_HEAD_DIM), 0.1),
        'rw_ln_g': gain(18, (L, RW_WIDTH)),
        'rw_ln_b': nrm(19, (L, RW_WIDTH), 0.02),
        'sg_ln_g': gain(20, (L, SG_WIDTH)),
        'sg_ln_b': nrm(21, (L, SG_WIDTH), 0.02),
        'sg_w': nrm(22, (L, SG_GROUPS, SG_CHUNK, SG_CHUNK), SG_CHUNK ** -0.5),
        'sg_b': gain(23, (L, SG_GROUPS, SG_CHUNK)),
        'mla_q_norm_g': gain(24, (L, MLA_Q_LORA)),
        'mla_w_uq': nrm(25, (L, MLA_Q_LORA, MLA_HEADS * MLA_QK_DIM), MLA_Q_LORA ** -0.5),
        'mla_kv_norm_g': gain(26, (L, MLA_KV_LORA)),
        'mla_w_ukv': nrm(27, (L, MLA_KV_LORA, MLA_HEADS * (MLA_QK_NOPE + MLA_V_DIM)), MLA_KV_LORA ** -0.5),
        'p_rwkv': nrm(28, (L, RW_WIDTH, D), RW_WIDTH ** -0.5),
        'p_sgu': nrm(29, (L, SG_WIDTH, D), SG_WIDTH ** -0.5),
        'p_mla': nrm(30, (L, MLA_WIDTH, D), MLA_WIDTH ** -0.5),
        'w_o': nrm(31, (L, D, D), D ** -0.5),
        'router_g_w': nrm(32, (L, D, MOE_GROUPS), D ** -0.5),
        'router_g_b': nrm(33, (L, MOE_GROUPS), 0.01),
        'router_e_w': nrm(34, (L, D, MOE_EXPERTS), D ** -0.5),
        'router_e_b': nrm(35, (L, MOE_EXPERTS), 0.01),
        'exp_w_gate': nrm(36, (L, MOE_EXPERTS, D, MOE_D_FF), D ** -0.5),
        'exp_w_up': nrm(37, (L, MOE_EXPERTS, D, MOE_D_FF), D ** -0.5),
        'exp_w_down': nrm(38, (L, MOE_EXPERTS, MOE_D_FF, D), MOE_D_FF ** -0.5),
    }


def reference(x, c, positions, ada_w, ada_b, norm1_g, norm2_g, final_g, w_in, rw_mu, rw_w0,
              rw_w2, rw_a0, rw_a2, rw_g2, rw_k_k, rw_k_a, rw_r_k, rw_ln_g, rw_ln_b, sg_ln_g,
              sg_ln_b, sg_w, sg_b, mla_q_norm_g, mla_w_uq, mla_kv_norm_g, mla_w_ukv, p_rwkv,
              p_sgu, p_mla, w_o, router_g_w, router_g_b, router_e_w, router_e_b, exp_w_gate,
              exp_w_up, exp_w_down):
    cos, sin = rope_tables(positions)
    c_act = jax.nn.silu(c)
    for l in range(DEPTH):
        mod = (c_act @ ada_w[l] + ada_b[l])[:, None, :]
        shift1, scale1, gate1, shift2, scale2, gate2 = jnp.split(mod, 6, axis=-1)
        h = rms_norm(x, norm1_g[l]) * (1.0 + scale1) + shift1
        x = x + gate1 * token_mixer(
            h, cos, sin, w_in[l], rw_mu[l], rw_w0[l], rw_w2[l], rw_a0[l], rw_a2[l], rw_g2[l],
            rw_k_k[l], rw_k_a[l], rw_r_k[l], rw_ln_g[l], rw_ln_b[l], sg_ln_g[l], sg_ln_b[l],
            sg_w[l], sg_b[l], mla_q_norm_g[l], mla_w_uq[l], mla_kv_norm_g[l], mla_w_ukv[l],
            p_rwkv[l], p_sgu[l], p_mla[l], w_o[l])
        h = rms_norm(x, norm2_g[l]) * (1.0 + scale2) + shift2
        x = x + gate2 * hierarchical_moe(h, router_g_w[l], router_g_b[l], router_e_w[l],
                                         router_e_b[l], exp_w_gate[l], exp_w_up[l], exp_w_down[l])
    return rms_norm(x, final_g)
```

```python
import functools

import jax
import jax.numpy as jnp
from jax import lax
from jax.experimental import pallas as pl
from jax.experimental.pallas import tpu as pltpu

f32 = jnp.float32
bf16 = jnp.bfloat16

D_MODEL = 2048
DEPTH = 4
RW_HEADS = 16
RW_HEAD_DIM = 64
RW_WIDTH = 1024
RW_DECAY_LORA = 96
RW_ICLR_LORA = 96
RW_GATE_LORA = 256
RW_GN_EPS = 64e-5
SG_CHUNK = 128
SG_GROUPS = 16
SG_WIDTH = 1024
MLA_HEADS = 8
MLA_Q_LORA = 512
MLA_KV_LORA = 256
MLA_QK_NOPE = 128
MLA_QK_ROPE = 64
MLA_QK_DIM = 192
MLA_V_DIM = 128
ROPE_BASE = 10000.0
MOE_GROUPS = 8
MOE_EPG = 8
MOE_EXPERTS = 64
MOE_TOP_K = 2
MOE_D_FF = 384
MOE_BLOCK = 128
NORM_EPS = 1e-6

LANES = 128
RW_CHUNK = 64
VMEM_LIMIT = 48 * 1024 * 1024


def _cparams(sem):
    return pltpu.CompilerParams(dimension_semantics=sem, vmem_limit_bytes=VMEM_LIMIT)


def _mm_kernel(a_ref, w_ref, o_ref):
    o_ref[...] = jnp.dot(a_ref[...].astype(bf16), w_ref[...].astype(bf16),
                         preferred_element_type=f32).astype(o_ref.dtype)


def matmul(a, w, *, tm, tn, out_dtype=f32, layer=None):
    m, k = a.shape
    n = w.shape[-1]
    assert m % tm == 0 and n % tn == 0, (m, tm, n, tn)
    if layer is None:
        w_spec = pl.BlockSpec((k, tn), lambda j, i: (0, j))
    else:
        w_spec = pl.BlockSpec((None, k, tn), lambda j, i: (layer, 0, j))
    return pl.pallas_call(
        _mm_kernel,
        grid=(n // tn, m // tm),
        in_specs=[pl.BlockSpec((tm, k), lambda j, i: (i, 0)), w_spec],
        out_specs=pl.BlockSpec((tm, tn), lambda j, i: (i, j)),
        out_shape=jax.ShapeDtypeStruct((m, n), out_dtype),
        compiler_params=_cparams(("parallel", "parallel")),
    )(a, w)


def _dot(a, b):
    return jnp.dot(a.astype(bf16), b.astype(bf16), preferred_element_type=f32)


def _dot_nt(a, b):
    return lax.dot_general(a.astype(bf16), b.astype(bf16), (((1,), (1,)), ((), ())),
                           preferred_element_type=f32)


def _dot_tn(a, b):
    return lax.dot_general(a.astype(bf16), b.astype(bf16), (((0,), (0,)), ((), ())),
                           preferred_element_type=f32)


def _rwkv_kernel(r_ref, lw_ref, k_ref, v_ref, kk_ref, a_ref, o_ref, state_ref):
    c = RW_CHUNK
    n = 2 * c

    @pl.when(pl.program_id(1) == 0)
    def _():
        state_ref[...] = jnp.zeros_like(state_ref)

    row = lax.broadcasted_iota(jnp.int32, (n, n), 0)
    col = lax.broadcasted_iota(jnp.int32, (n, n), 1)
    same = (row >> 6) == (col >> 6)
    rpos = row & (c - 1)
    cpos = col & (c - 1)
    strict = same & (cpos < rpos)
    incl = same & (cpos <= rpos)
    eye = (row == col).astype(f32)
    blk16 = (row >> 4) == (col >> 4)
    blk32 = (row >> 5) == (col >> 5)
    low32 = blk32 & jnp.logical_not(blk16)
    low64 = same & jnp.logical_not(blk32)
    tr = lax.broadcasted_iota(jnp.int32, (c, c), 0)
    tc = lax.broadcasted_iota(jnp.int32, (c, c), 1)
    tril1 = (tc <= tr).astype(bf16)
    m0 = lax.broadcasted_iota(jnp.int32, (c, n), 1) < c

    def expand(y):
        return jnp.concatenate([jnp.where(m0, y, 0.0), jnp.where(m0, 0.0, y)], axis=0)

    def collapse(y):
        return y[:c] + y[c:]

    for p in range(RW_HEADS // 2):
        sl = slice(p * LANES, (p + 1) * LANES)
        r = r_ref[:, sl]
        lw = lw_ref[:, sl]
        k = k_ref[:, sl]
        v = v_ref[:, sl]
        kk = kk_ref[:, sl]
        a = a_ref[:, sl]
        lw1 = lw.astype(bf16)
        res = lw - lw1.astype(f32)
        lw2 = res.astype(bf16)
        lw3 = (res - lw2.astype(f32)).astype(bf16)
        cum = (jnp.dot(tril1, lw1, preferred_element_type=f32)
               + jnp.dot(tril1, lw2, preferred_element_type=f32)
               + jnp.dot(tril1, lw3, preferred_element_type=f32))
        ctot = cum[c - 1:c, :]
        gi = jnp.exp(cum)
        ge = jnp.exp(cum - lw)
        ginv = jnp.exp(-cum)
        g2 = jnp.exp(ctot - cum)
        gc = jnp.exp(ctot)
        b = kk * a
        at = -kk * ge
        bt = b * ginv
        kt = k * ginv
        rt = r * gi
        bt2 = b * g2
        kt2 = k * g2
        lhs = jnp.concatenate([expand(at), expand(rt)], axis=0)
        rhs = jnp.concatenate([expand(bt), expand(kt)], axis=0)
        pm = _dot_nt(lhs, rhs)
        a_ab = jnp.where(strict, pm[:n, :n], 0.0)
        a_ak = jnp.where(strict, pm[:n, n:], 0.0)
        m_rb = jnp.where(incl, pm[n:, :n], 0.0)
        m_rk = jnp.where(incl, pm[n:, n:], 0.0)
        a0 = jnp.where(blk16, a_ab, 0.0)
        p1 = eye + a0
        a2 = _dot(a0, a0)
        p2 = p1 + _dot(p1, a2)
        a4 = _dot(a2, a2)
        p3 = p2 + _dot(p2, a4)
        a8 = _dot(a4, a4)
        d = p3 + _dot(p3, a8)
        d = d + _dot(d, _dot(jnp.where(low32, a_ab, 0.0), d))
        t = d + _dot(d, _dot(jnp.where(low64, a_ab, 0.0), d))
        ev = expand(v)
        at_hat = collapse(_dot(t, expand(at)))
        u0 = collapse(_dot(t, _dot(a_ak, ev)))
        o0 = collapse(_dot(m_rk, ev))
        kv = jnp.where(same, _dot_tn(v, kt2), 0.0)
        s0 = state_ref[p]
        so = _dot_nt(jnp.concatenate([at_hat, rt], axis=0), s0)
        u = so[:c] + u0
        o = so[c:] + o0 + collapse(_dot(m_rb, expand(u)))
        state_ref[p] = s0 * gc + kv + jnp.where(same, _dot_tn(u, bt2), 0.0)
        o_ref[:, sl] = o


def rwkv_recurrence(r, lw, k, v, kk, a):
    b, s, w = r.shape
    spec = pl.BlockSpec((None, RW_CHUNK, w), lambda i, j: (i, j, 0))
    return pl.pallas_call(
        _rwkv_kernel,
        grid=(b, s // RW_CHUNK),
        in_specs=[spec] * 6,
        out_specs=spec,
        out_shape=jax.ShapeDtypeStruct((b, s, w), f32),
        scratch_shapes=[pltpu.VMEM((RW_HEADS // 2, LANES, LANES), f32)],
        compiler_params=_cparams(("parallel", "arbitrary")),
    )(r, lw, k, v, kk, a)


def _sgu_kernel(p_ref, lng_ref, lnb_ref, w_ref, bias_ref, o_ref):
    x = p_ref[...]
    g = 0.5 * x * (1.0 + jnp.tanh(0.7978845608028654 * (x + 0.044715 * (x * x * x))))
    u = g[:, :SG_WIDTH]
    v = g[:, SG_WIDTH:]
    mu = jnp.mean(v, axis=-1, keepdims=True)
    vc = v - mu
    var = jnp.mean(vc * vc, axis=-1, keepdims=True)
    vn = (vc * lax.rsqrt(var + 1e-5) * lng_ref[...] + lnb_ref[...]).astype(bf16)
    first = lax.broadcasted_iota(jnp.int32, (SG_CHUNK, LANES), 1) < (LANES // 2)
    for q in range(SG_GROUPS // 2):
        sl = slice(q * LANES, (q + 1) * LANES)
        vq = vn[:, sl]
        lo = jnp.dot(w_ref[2 * q], vq, preferred_element_type=f32)
        hi = jnp.dot(w_ref[2 * q + 1], vq, preferred_element_type=f32)
        mixed = jnp.where(first, lo, hi) + bias_ref[:, sl]
        o_ref[:, sl] = (u[:, sl] * mixed).astype(o_ref.dtype)


def sgu(p_sg, ln_g, ln_b, w_tril, bias_full):
    t = p_sg.shape[0]
    return pl.pallas_call(
        _sgu_kernel,
        grid=(t // SG_CHUNK,),
        in_specs=[pl.BlockSpec((SG_CHUNK, 2 * SG_WIDTH), lambda i: (i, 0)),
                  pl.BlockSpec((1, SG_WIDTH), lambda i: (0, 0)),
                  pl.BlockSpec((1, SG_WIDTH), lambda i: (0, 0)),
                  pl.BlockSpec((SG_GROUPS, SG_CHUNK, SG_CHUNK), lambda i: (0, 0, 0)),
                  pl.BlockSpec((SG_CHUNK, SG_WIDTH), lambda i: (0, 0))],
        out_specs=pl.BlockSpec((SG_CHUNK, SG_WIDTH), lambda i: (i, 0)),
        out_shape=jax.ShapeDtypeStruct((t, SG_WIDTH), bf16),
        compiler_params=_cparams(("parallel",)),
    )(p_sg, ln_g, ln_b, w_tril, bias_full)


ATT_BLOCK = 512
ATT_QK_PAD = 256


def _attn_kernel(q_ref, k_ref, v_ref, o_ref, m_ref, l_ref, acc_ref):
    i = pl.program_id(2)
    j = pl.program_id(3)

    @pl.when(j == 0)
    def _():
        m_ref[...] = jnp.full_like(m_ref, -jnp.inf)
        l_ref[...] = jnp.zeros_like(l_ref)
        acc_ref[...] = jnp.zeros_like(acc_ref)

    @pl.when(j <= i)
    def _():
        s = lax.dot_general(q_ref[...], k_ref[...], (((1,), (1,)), ((), ())), preferred_element_type=f32)
        rowi = lax.broadcasted_iota(jnp.int32, s.shape, 0)
        coli = lax.broadcasted_iota(jnp.int32, s.shape, 1)
        s = jnp.where((j == i) & (coli > rowi), -jnp.inf, s)
        m_old = m_ref[...]
        m_new = jnp.maximum(m_old, jnp.max(s, axis=-1, keepdims=True))
        p = jnp.exp(s - m_new)
        alpha = jnp.exp(m_old - m_new)
        l_ref[...] = alpha * l_ref[...] + jnp.sum(p, axis=-1, keepdims=True)
        acc_ref[...] = alpha * acc_ref[...] + jnp.dot(p.astype(bf16), v_ref[...], preferred_element_type=f32)
        m_ref[...] = m_new

    @pl.when(j == pl.num_programs(3) - 1)
    def _():
        o_ref[...] = (acc_ref[...] / l_ref[...]).astype(o_ref.dtype)


def attention(q_cat, k_cat, v):
    b, s, _ = q_cat.shape
    nb = s // ATT_BLOCK
    return pl.pallas_call(
        _attn_kernel,
        grid=(b, MLA_HEADS, nb, nb),
        in_specs=[pl.BlockSpec((None, ATT_BLOCK, ATT_QK_PAD), lambda bi, h, i, j: (bi, i, h)),
                  pl.BlockSpec((None, ATT_BLOCK, ATT_QK_PAD), lambda bi, h, i, j: (bi, jnp.minimum(j, i), h)),
                  pl.BlockSpec((None, ATT_BLOCK, MLA_V_DIM), lambda bi, h, i, j: (bi, jnp.minimum(j, i), h))],
        out_specs=pl.BlockSpec((None, ATT_BLOCK, MLA_V_DIM), lambda bi, h, i, j: (bi, i, h)),
        out_shape=jax.ShapeDtypeStruct((b, s, MLA_HEADS * MLA_V_DIM), bf16),
        scratch_shapes=[pltpu.VMEM((ATT_BLOCK, 1), f32), pltpu.VMEM((ATT_BLOCK, 1), f32),
                        pltpu.VMEM((ATT_BLOCK, MLA_V_DIM), f32)],
        compiler_params=_cparams(("parallel", "parallel", "parallel", "arbitrary")),
    )(q_cat, k_cat, v)


def _moe_kernel(be_ref, nu_ref, x_ref, wg_ref, wu_ref, wd_ref, o_ref):
    i = pl.program_id(0)

    @pl.when(i < nu_ref[0])
    def _():
        x = x_ref[...]
        g = jnp.dot(x, wg_ref[...].astype(bf16), preferred_element_type=f32)
        u = jnp.dot(x, wu_ref[...].astype(bf16), preferred_element_type=f32)
        hmid = (g * jax.nn.sigmoid(g) * u).astype(bf16)
        o_ref[...] = jnp.dot(hmid, wd_ref[...].astype(bf16), preferred_element_type=f32)

    @pl.when(i >= nu_ref[0])
    def _():
        o_ref[...] = jnp.zeros_like(o_ref)


def moe_ffn(block_expert, n_used, xb, w_gate, w_up, w_down, layer):
    rows, d = xb.shape
    nblk = rows // MOE_BLOCK

    def xmap(i, be, nu):
        return (jnp.minimum(i, nu[0] - 1), 0)

    def wmap(i, be, nu):
        return (layer, be[i], 0, 0)

    return pl.pallas_call(
        _moe_kernel,
        grid_spec=pltpu.PrefetchScalarGridSpec(
            num_scalar_prefetch=2,
            grid=(nblk,),
            in_specs=[pl.BlockSpec((MOE_BLOCK, d), xmap),
                      pl.BlockSpec((None, None, d, MOE_D_FF), wmap),
                      pl.BlockSpec((None, None, d, MOE_D_FF), wmap),
                      pl.BlockSpec((None, None, MOE_D_FF, d), wmap)],
            out_specs=pl.BlockSpec((MOE_BLOCK, d), lambda i, be, nu: (i, 0)),
        ),
        out_shape=jax.ShapeDtypeStruct((rows, d), f32),
        compiler_params=_cparams(("arbitrary",)),
    )(block_expert, n_used, xb, w_gate, w_up, w_down)


def _rms(x, g):
    return x * lax.rsqrt(jnp.mean(x * x, axis=-1, keepdims=True) + NORM_EPS) * g


def _pad_cols(w, width):
    return jnp.pad(w, ((0, 0), (0, width - w.shape[1])))


def _pad_rows(w, height):
    return jnp.pad(w, ((0, height - w.shape[0]), (0, 0)))


def _rope(x, cos, sin):
    half = x.shape[-1] // 2
    x1, x2 = x[..., :half], x[..., half:]
    return jnp.concatenate([x1 * cos - x2 * sin, x1 * sin + x2 * cos], axis=-1)


def _rwkv_branch(p, bsz, seq, mu, w0, w_w2, a0, w_a2, w_g2, k_k, k_a, r_k, ln_g, ln_b):
    t = p.shape[0]
    p3 = p.reshape(bsz, seq, -1)
    prev = jnp.pad(p3[:, :-1], ((0, 0), (1, 0), (0, 0)))
    p = (p3 + mu * (prev - p3)).reshape(t, -1)
    r, k, v = p[:, :1024], p[:, 1024:2048], p[:, 2048:3072]
    w_lo, a_lo, g_lo = p[:, 3072:3200], p[:, 3200:3328], p[:, 3328:3584]
    dec = matmul(jnp.tanh(w_lo), _pad_rows(w_w2, LANES), tm=2048, tn=1024)
    log_w = -jax.nn.softplus(-(w0 + dec)) - 0.5
    lw = -jnp.exp(log_w)
    a = jax.nn.sigmoid(a0 + matmul(a_lo, _pad_rows(w_a2, LANES), tm=2048, tn=1024))
    g = matmul(jax.nn.sigmoid(g_lo), w_g2, tm=2048, tn=1024)

    def heads(x):
        return x.reshape(t, RW_HEADS, RW_HEAD_DIM)

    kk = heads(k * k_k)
    kk = (kk / jnp.maximum(jnp.sqrt(jnp.sum(kk * kk, axis=-1, keepdims=True)), 1e-12)).reshape(t, RW_WIDTH)
    k_mod = k * (1.0 + (a - 1.0) * k_a)
    sh = (bsz, seq, RW_WIDTH)
    o = rwkv_recurrence(r.reshape(sh), lw.reshape(sh), k_mod.reshape(sh), v.reshape(sh),
                        kk.reshape(sh), a.reshape(sh)).reshape(t, RW_WIDTH)
    oh = heads(o)
    mean = jnp.mean(oh, axis=-1, keepdims=True)
    var = jnp.mean((oh - mean) ** 2, axis=-1, keepdims=True)
    o = ((oh - mean) * lax.rsqrt(var + RW_GN_EPS)).reshape(t, RW_WIDTH) * ln_g + ln_b
    bonus = jnp.sum(heads(r) * heads(k_mod) * r_k, axis=-1, keepdims=True) * heads(v)
    return (o + bonus.reshape(t, RW_WIDTH)) * g


def _mla_branch(p, bsz, seq, cos, sin, q_norm_g, w_uq, kv_norm_g, w_ukv):
    t = p.shape[0]
    c_q, c_kv, k_rope = p[:, :512], p[:, 512:768], p[:, 768:832]
    q = matmul(_rms(c_q, q_norm_g), w_uq, tm=2048, tn=768).reshape(bsz, seq, MLA_HEADS, MLA_QK_DIM)
    kv = matmul(_rms(c_kv, kv_norm_g), w_ukv, tm=2048, tn=1024).reshape(bsz, seq, MLA_HEADS, 256)
    scale = MLA_QK_DIM ** -0.5
    q_rope = _rope(q[..., MLA_QK_NOPE:], cos[:, :, None, :], sin[:, :, None, :])
    zq = jnp.zeros((bsz, seq, MLA_HEADS, ATT_QK_PAD - MLA_QK_DIM), f32)
    q_cat = (jnp.concatenate([q[..., :MLA_QK_NOPE], q_rope, zq], axis=-1) * scale).astype(bf16)
    k_r = _rope(k_rope.reshape(bsz, seq, MLA_QK_ROPE), cos, sin)
    k_r = jnp.broadcast_to(k_r[:, :, None, :], (bsz, seq, MLA_HEADS, MLA_QK_ROPE))
    k_cat = jnp.concatenate([kv[..., :MLA_QK_NOPE], k_r, zq], axis=-1).astype(bf16)
    v = kv[..., MLA_QK_NOPE:].astype(bf16)
    o = attention(q_cat.reshape(bsz, seq, -1), k_cat.reshape(bsz, seq, -1), v.reshape(bsz, seq, -1))
    return o.reshape(t, MLA_HEADS * MLA_V_DIM)


def _moe(h, layer, g_w, g_b, e_w, e_b, w_gate, w_up, w_down):
    t, d = h.shape
    w_r = _pad_cols(jnp.concatenate([g_w, e_w], axis=1), LANES)
    logits = matmul(h, w_r, tm=2048, tn=LANES)
    group_logits = logits[:, :MOE_GROUPS] + g_b
    group = jnp.argmax(group_logits, axis=-1)
    group_w = jnp.take_along_axis(jax.nn.softmax(group_logits, axis=-1), group[:, None], axis=-1)
    exp_logits = (logits[:, MOE_GROUPS:MOE_GROUPS + MOE_EXPERTS] + e_b).reshape(t, MOE_GROUPS, MOE_EPG)
    in_group = jnp.take_along_axis(exp_logits, group[:, None, None], axis=1)[:, 0]
    top_logit, top_idx = lax.top_k(in_group, MOE_TOP_K)
    weights = (group_w * jax.nn.softmax(top_logit, axis=-1)).reshape(-1)
    expert_ids = (group[:, None] * MOE_EPG + top_idx).reshape(-1).astype(jnp.int32)
    n_assign = t * MOE_TOP_K
    n_blocks = -(-n_assign // MOE_BLOCK) + MOE_EXPERTS
    rows = n_blocks * MOE_BLOCK
    onehot = (expert_ids[:, None] == jnp.arange(MOE_EXPERTS, dtype=jnp.int32)[None, :]).astype(jnp.int32)
    csum = jnp.cumsum(onehot, axis=0)
    rank = jnp.take_along_axis(csum, expert_ids[:, None], axis=1)[:, 0] - 1
    counts = csum[-1]
    padded = (counts + MOE_BLOCK - 1) // MOE_BLOCK * MOE_BLOCK
    ends = jnp.cumsum(padded)
    starts = ends - padded
    dest = starts[expert_ids] + rank
    token_ids = jnp.arange(n_assign, dtype=jnp.int32) // MOE_TOP_K
    row_token = jnp.full((rows,), t, jnp.int32).at[dest].set(token_ids)
    block_start = jnp.arange(n_blocks, dtype=jnp.int32) * MOE_BLOCK
    block_expert = jnp.minimum(jnp.searchsorted(ends, block_start, side='right'),
                               MOE_EXPERTS - 1).astype(jnp.int32)
    n_used = (ends[-1] // MOE_BLOCK).astype(jnp.int32).reshape(1)
    h_pad = jnp.concatenate([h, jnp.zeros((1, d), h.dtype)], axis=0)
    xb = h_pad[row_token]
    yb = moe_ffn(block_expert, n_used, xb, w_gate, w_up, w_down, layer)
    y2 = yb[dest].reshape(t, MOE_TOP_K, d) * weights.reshape(t, MOE_TOP_K, 1)
    return y2[:, 0] + y2[:, 1]


def kernel(x, c, positions, ada_w, ada_b, norm1_g, norm2_g, final_g, w_in, rw_mu, rw_w0, rw_w2, rw_a0, rw_a2, rw_g2, rw_k_k, rw_k_a, rw_r_k, rw_ln_g, rw_ln_b, sg_ln_g, sg_ln_b, sg_w, sg_b, mla_q_norm_g, mla_w_uq, mla_kv_norm_g, mla_w_ukv, p_rwkv, p_sgu, p_mla, w_o, router_g_w, router_g_b, router_e_w, router_e_b, exp_w_gate, exp_w_up, exp_w_down):
    bsz, seq, d = x.shape
    t = bsz * seq
    half = MLA_QK_ROPE // 2
    inv_freq = ROPE_BASE ** (-jnp.arange(half, dtype=f32) / half)
    ang = positions.astype(f32)[..., None] * inv_freq
    cos, sin = jnp.cos(ang), jnp.sin(ang)
    c_act = jnp.pad(jax.nn.silu(c), ((0, 8 - bsz), (0, 0)))
    x = x.reshape(t, d)

    def modulate(xn, scale, shift):
        xn = xn.reshape(bsz, seq, d) * (1.0 + scale[:, None, :]) + shift[:, None, :]
        return xn.reshape(t, d).astype(bf16)

    def gated_add(xr, gate, y):
        return (xr.reshape(bsz, seq, d) + gate[:, None, :] * y.reshape(bsz, seq, d)).reshape(t, d)

    for l in range(DEPTH):
        mod = matmul(c_act, ada_w, tm=8, tn=1024, layer=l)[:bsz] + ada_b[l]
        shift1, scale1, gate1, shift2, scale2, gate2 = jnp.split(mod, 6, axis=-1)
        h = modulate(_rms(x, norm1_g[l]), scale1, shift1)
        wl = w_in[l]
        w_rw = jnp.concatenate([wl[:, :3072], _pad_cols(wl[:, 3072:3168], LANES),
                                _pad_cols(wl[:, 3168:3264], LANES), wl[:, 3264:3520]], axis=1).astype(bf16)
        mul = rw_mu[l]
        mu = jnp.concatenate([mul[:3072], jnp.pad(mul[3072:3168], (0, 32)),
                              jnp.pad(mul[3168:3264], (0, 32)), mul[3264:3520]])
        w_sg = wl[:, 3520:5568].astype(bf16)
        w_at = _pad_cols(wl[:, 5568:6400], 896).astype(bf16)
        w_gt = wl[:, 6400:].astype(bf16)
        p_rw = matmul(h, w_rw, tm=1024, tn=896)
        p_sg = matmul(h, w_sg, tm=1024, tn=1024)
        p_at = matmul(h, w_at, tm=1024, tn=896)
        p_gate = matmul(h, w_gt, tm=1024, tn=1024)
        y_rw = _rwkv_branch(p_rw, bsz, seq, mu, rw_w0[l], rw_w2[l], rw_a0[l], rw_a2[l], rw_g2[l],
                            rw_k_k[l], rw_k_a[l], rw_r_k[l], rw_ln_g[l], rw_ln_b[l])
        y_rw = matmul(y_rw, p_rwkv, tm=1024, tn=1024, layer=l)
        bias_full = jnp.repeat(sg_b[l].T, RW_HEAD_DIM, axis=1)
        y_sg = sgu(p_sg, sg_ln_g[l][None, :], sg_ln_b[l][None, :], jnp.tril(sg_w[l]).astype(bf16), bias_full)
        y_sg = matmul(y_sg, p_sgu, tm=1024, tn=1024, layer=l)
        y_at = _mla_branch(p_at, bsz, seq, cos, sin, mla_q_norm_g[l], mla_w_uq[l], mla_kv_norm_g[l],
                           mla_w_ukv[l])
        y_at = matmul(y_at, p_mla, tm=1024, tn=1024, layer=l)
        gates = jax.nn.sigmoid(p_gate)
        mix = gates[:, :d] * y_rw + gates[:, d:2 * d] * y_sg + gates[:, 2 * d:] * y_at
        x = gated_add(x, gate1, matmul(mix, w_o, tm=1024, tn=1024, layer=l))
        h2 = modulate(_rms(x, norm2_g[l]), scale2, shift2)
        y = _moe(h2, l, router_g_w[l], router_g_b[l], router_e_w[l], router_e_b[l],
                 exp_w_gate, exp_w_up, exp_w_down)
        x = gated_add(x, gate2, y)
    return _rms(x, final_g).reshape(bsz, seq, d)
```

```python
import jax
import jax.numpy as jnp
from jax import lax
from jax.experimental import pallas as pl
from jax.experimental.pallas import tpu as pltpu

f32 = jnp.float32
bf16 = jnp.bfloat16

D_MODEL = 2048
DEPTH = 4
RW_HEADS = 16
RW_HEAD_DIM = 64
RW_WIDTH = 1024
RW_LORA_COLS = 512
RW_COLS_PAD = 3 * RW_WIDTH + RW_LORA_COLS
RW_GN_EPS = 64e-5
SG_CHUNK = 128
SG_GROUPS = 16
SG_WIDTH = 1024
MLA_HEADS = 8
MLA_Q_LORA = 512
MLA_KV_LORA = 256
MLA_QK_NOPE = 128
MLA_QK_ROPE = 64
MLA_QK_DIM = 192
MLA_V_DIM = 128
MLA_COLS_PAD = 896
ROPE_BASE = 10000.0
MOE_GROUPS = 8
MOE_EPG = 8
MOE_EXPERTS = 64
MOE_TOP_K = 2
MOE_D_FF = 384
MOE_BLOCK = 128
NORM_EPS = 1e-6

LANES = 128
RW_CHUNK = 64
VMEM_LIMIT = 48 * 1024 * 1024


def _cparams(sem):
    return pltpu.CompilerParams(dimension_semantics=sem, vmem_limit_bytes=VMEM_LIMIT)


def _dot(a, b):
    return jnp.dot(a.astype(bf16), b.astype(bf16), preferred_element_type=f32)


def _dot_nt(a, b):
    return lax.dot_general(a.astype(bf16), b.astype(bf16), (((1,), (1,)), ((), ())),
                           preferred_element_type=f32)


def _dot_tn(a, b):
    return lax.dot_general(a.astype(bf16), b.astype(bf16), (((0,), (0,)), ((), ())),
                           preferred_element_type=f32)


def _mm_kernel(a_ref, w_ref, o_ref):
    o_ref[...] = _dot(a_ref[...], w_ref[...]).astype(o_ref.dtype)


def matmul(a, w, *, tm, tn, out_dtype=f32, layer=None, name="mm"):
    m, k = a.shape
    n = w.shape[-1]
    assert m % tm == 0 and n % tn == 0, (m, tm, n, tn)
    if layer is None:
        w_spec = pl.BlockSpec((k, tn), lambda j, i: (0, j))
    else:
        w_spec = pl.BlockSpec((None, k, tn), lambda j, i: (layer, 0, j))
    return pl.pallas_call(
        _mm_kernel,
        grid=(n // tn, m // tm),
        in_specs=[pl.BlockSpec((tm, k), lambda j, i: (i, 0)), w_spec],
        out_specs=pl.BlockSpec((tm, tn), lambda j, i: (i, j)),
        out_shape=jax.ShapeDtypeStruct((m, n), out_dtype),
        compiler_params=_cparams(("parallel", "parallel")),
        name=name,
    )(a, w)


RW_LORA_TM = 1024


def _softplus(z):
    return jnp.maximum(z, 0.0) + jnp.log(1.0 + jnp.exp(-jnp.abs(z)))


def _rw_lora_kernel(p_ref, mu_ref, w0_ref, a0_ref, ww_ref, wa_ref, wg_ref, lw_ref, a_ref, g_ref, prev_ref):
    @pl.when(pl.program_id(1) == 0)
    def _():
        prev_ref[...] = jnp.zeros_like(prev_ref)

    x = p_ref[...]
    rowi = lax.broadcasted_iota(jnp.int32, x.shape, 0)
    xprev = jnp.where(rowi == 0, prev_ref[...], pltpu.roll(x, 1, 0))
    prev_ref[...] = x[RW_LORA_TM - 1:RW_LORA_TM, :]
    xs = x + mu_ref[...] * (xprev - x)
    dec = _dot(jnp.tanh(xs[:, :LANES]), ww_ref[...])
    log_w = -_softplus(-(w0_ref[...] + dec)) - 0.5
    lw_ref[...] = -jnp.exp(log_w)
    a_ref[...] = jax.nn.sigmoid(a0_ref[...] + _dot(xs[:, LANES:2 * LANES], wa_ref[...]))
    g_ref[...] = _dot(jax.nn.sigmoid(xs[:, 2 * LANES:]), wg_ref[...]).astype(g_ref.dtype)


def rw_lora(p_rw, mu_lora, w0, a0, w_w2, w_a2, w_g2):
    b, s, _ = p_rw.shape
    cb = 3 * RW_WIDTH // RW_LORA_COLS
    vec = pl.BlockSpec((1, RW_WIDTH), lambda i, j: (0, 0))
    out = pl.BlockSpec((None, RW_LORA_TM, RW_WIDTH), lambda i, j: (i, j, 0))
    return pl.pallas_call(
        _rw_lora_kernel,
        grid=(b, s // RW_LORA_TM),
        in_specs=[pl.BlockSpec((None, RW_LORA_TM, RW_LORA_COLS), lambda i, j: (i, j, cb)),
                  pl.BlockSpec((1, RW_LORA_COLS), lambda i, j: (0, 0)), vec, vec,
                  pl.BlockSpec((LANES, RW_WIDTH), lambda i, j: (0, 0)),
                  pl.BlockSpec((LANES, RW_WIDTH), lambda i, j: (0, 0)),
                  pl.BlockSpec((2 * LANES, RW_WIDTH), lambda i, j: (0, 0))],
        out_specs=[out, out, out],
        out_shape=[jax.ShapeDtypeStruct((b, s, RW_WIDTH), f32), jax.ShapeDtypeStruct((b, s, RW_WIDTH), f32),
                   jax.ShapeDtypeStruct((b, s, RW_WIDTH), bf16)],
        scratch_shapes=[pltpu.VMEM((1, RW_LORA_COLS), f32)],
        compiler_params=_cparams(("parallel", "arbitrary")),
        name="rw_lora",
    )(p_rw, mu_lora, w0, a0, w_w2, w_a2, w_g2)


def _rwkv_kernel(p_ref, lw_ref, a_ref, g_ref, mu_ref, kk_ref, ka_ref, rk_ref, lng_ref, lnb_ref,
                 o_ref, state_ref, prev_ref):
    c = RW_CHUNK
    n = 2 * c
    pairs = range(RW_HEADS // 2)

    @pl.when(pl.program_id(1) == 0)
    def _():
        state_ref[...] = jnp.zeros_like(state_ref)
        prev_ref[...] = jnp.zeros_like(prev_ref)

    row = lax.broadcasted_iota(jnp.int32, (n, n), 0)
    col = lax.broadcasted_iota(jnp.int32, (n, n), 1)
    same = (row >> 6) == (col >> 6)
    rpos = row & (c - 1)
    cpos = col & (c - 1)
    strict = same & (cpos < rpos)
    incl = same & (cpos <= rpos)
    eye = (row == col).astype(f32)
    blk16 = (row >> 4) == (col >> 4)
    blk32 = (row >> 5) == (col >> 5)
    low32 = blk32 & jnp.logical_not(blk16)
    low64 = same & jnp.logical_not(blk32)
    tr = lax.broadcasted_iota(jnp.int32, (c, c), 0)
    tc = lax.broadcasted_iota(jnp.int32, (c, c), 1)
    tril1 = (tc <= tr).astype(bf16)
    m0 = lax.broadcasted_iota(jnp.int32, (c, n), 1) < c
    row0 = lax.broadcasted_iota(jnp.int32, (c, n), 0) == 0

    def expand(y):
        return jnp.concatenate([jnp.where(m0, y, 0.0), jnp.where(m0, 0.0, y)], axis=0)

    def collapse(y):
        return y[:c] + y[c:]

    def head_sum(y):
        lo = jnp.sum(jnp.where(m0, y, 0.0), axis=-1, keepdims=True)
        hi = jnp.sum(jnp.where(m0, 0.0, y), axis=-1, keepdims=True)
        return jnp.where(m0, lo, hi)

    def shifted(base, p):
        sl = slice(base + p * LANES, base + (p + 1) * LANES)
        x = p_ref[:, sl]
        xprev = jnp.where(row0, prev_ref[:, sl], pltpu.roll(x, 1, 0))
        prev_ref[:, sl] = x[c - 1:c, :]
        return x + mu_ref[:, sl] * (xprev - x)

    sls = [slice(p * LANES, (p + 1) * LANES) for p in pairs]
    r = [shifted(0, p) for p in pairs]
    k0 = [shifted(RW_WIDTH, p) for p in pairs]
    v = [shifted(2 * RW_WIDTH, p) for p in pairs]
    a = [a_ref[:, s] for s in sls]
    lw = [lw_ref[:, s] for s in sls]
    kkr = [k0[p] * kk_ref[:, sls[p]] for p in pairs]
    kk = [kkr[p] * lax.rsqrt(jnp.maximum(head_sum(kkr[p] * kkr[p]), 1e-24)) for p in pairs]
    k = [k0[p] * (1.0 + (a[p] - 1.0) * ka_ref[:, sls[p]]) for p in pairs]

    def cumsum(x):
        x1 = x.astype(bf16)
        res = x - x1.astype(f32)
        x2 = res.astype(bf16)
        x3 = (res - x2.astype(f32)).astype(bf16)
        return (jnp.dot(tril1, x1, preferred_element_type=f32) + jnp.dot(tril1, x2, preferred_element_type=f32)
                + jnp.dot(tril1, x3, preferred_element_type=f32))

    cum = [cumsum(lw[p]) for p in pairs]
    ctot = [cum[p][c - 1:c, :] for p in pairs]
    ginv = [jnp.exp(-cum[p]) for p in pairs]
    g2 = [jnp.exp(ctot[p] - cum[p]) for p in pairs]
    b = [kk[p] * a[p] for p in pairs]
    at = [-kk[p] * jnp.exp(cum[p] - lw[p]) for p in pairs]
    rt = [r[p] * jnp.exp(cum[p]) for p in pairs]
    bt = [b[p] * ginv[p] for p in pairs]
    kt = [k[p] * ginv[p] for p in pairs]
    pm = [_dot_nt(jnp.concatenate([expand(at[p]), expand(rt[p])], axis=0),
                  jnp.concatenate([expand(bt[p]), expand(kt[p])], axis=0)) for p in pairs]
    a_ab = [jnp.where(strict, pm[p][:n, :n], 0.0) for p in pairs]
    a_ak = [jnp.where(strict, pm[p][:n, n:], 0.0).astype(bf16) for p in pairs]
    m_rb = [jnp.where(incl, pm[p][n:, :n], 0.0).astype(bf16) for p in pairs]
    m_rk = [jnp.where(incl, pm[p][n:, n:], 0.0).astype(bf16) for p in pairs]
    ev = [expand(v[p]).astype(bf16) for p in pairs]
    akv = [_dot(a_ak[p], ev[p]) for p in pairs]
    o0 = [collapse(_dot(m_rk[p], ev[p])) for p in pairs]
    kv = [jnp.where(same, _dot_tn(v[p], k[p] * g2[p]), 0.0) for p in pairs]
    a0 = [jnp.where(blk16, a_ab[p], 0.0) for p in pairs]
    a2 = [_dot(a0[p], a0[p]) for p in pairs]
    p2 = [eye + a0[p] + a2[p] + _dot(a0[p], a2[p]) for p in pairs]
    a4 = [_dot(a2[p], a2[p]) for p in pairs]
    p3 = [p2[p] + _dot(p2[p], a4[p]) for p in pairs]
    a8 = [_dot(a4[p], a4[p]) for p in pairs]
    d = [p3[p] + _dot(p3[p], a8[p]) for p in pairs]
    x1 = [_dot(jnp.where(low32, a_ab[p], 0.0), d[p]) for p in pairs]
    d = [d[p] + _dot(d[p], x1[p]) for p in pairs]
    x2 = [_dot(jnp.where(low64, a_ab[p], 0.0), d[p]) for p in pairs]
    t = [(d[p] + _dot(d[p], x2[p])).astype(bf16) for p in pairs]
    tx = [_dot(t[p], jnp.concatenate([expand(at[p]), akv[p]], axis=1)) for p in pairs]
    at_hat = [collapse(tx[p][:, :n]) for p in pairs]
    u0 = [collapse(tx[p][:, n:]) for p in pairs]
    s0 = [state_ref[p] for p in pairs]
    so = [_dot_nt(jnp.concatenate([at_hat[p], rt[p]], axis=0), s0[p]) for p in pairs]
    u = [so[p][:c] + u0[p] for p in pairs]
    su = [jnp.where(same, _dot_tn(u[p], b[p] * g2[p]), 0.0) for p in pairs]
    mu_ = [collapse(_dot(m_rb[p], expand(u[p]))) for p in pairs]
    for p in pairs:
        state_ref[p] = s0[p] * jnp.exp(ctot[p]) + kv[p] + su[p]
    inv_n = 1.0 / RW_HEAD_DIM
    for p in pairs:
        o = so[p][c:] + o0[p] + mu_[p]
        mean = head_sum(o) * inv_n
        oc = o - mean
        var = head_sum(oc * oc) * inv_n
        y = oc * lax.rsqrt(var + RW_GN_EPS) * lng_ref[:, sls[p]] + lnb_ref[:, sls[p]]
        y = y + head_sum(r[p] * k[p] * rk_ref[:, sls[p]]) * v[p]
        o_ref[:, sls[p]] = (y * g_ref[:, sls[p]].astype(f32)).astype(o_ref.dtype)


def rwkv_mix(p_rw, lw, a, g, mu, k_k, k_a, r_k, ln_g, ln_b):
    bsz, s, _ = p_rw.shape
    spec = pl.BlockSpec((None, RW_CHUNK, RW_WIDTH), lambda i, j: (i, j, 0))
    vec = pl.BlockSpec((1, RW_WIDTH), lambda i, j: (0, 0))
    return pl.pallas_call(
        _rwkv_kernel,
        grid=(bsz, s // RW_CHUNK),
        in_specs=[pl.BlockSpec((None, RW_CHUNK, 3 * RW_WIDTH), lambda i, j: (i, j, 0)), spec, spec, spec,
                  pl.BlockSpec((1, 3 * RW_WIDTH), lambda i, j: (0, 0)), vec, vec, vec, vec, vec],
        out_specs=spec,
        out_shape=jax.ShapeDtypeStruct((bsz, s, RW_WIDTH), bf16),
        scratch_shapes=[pltpu.VMEM((RW_HEADS // 2, LANES, LANES), f32), pltpu.VMEM((1, 3 * RW_WIDTH), f32)],
        compiler_params=_cparams(("parallel", "arbitrary")),
        name="rwkv_mix",
    )(p_rw, lw, a, g, mu, k_k, k_a, r_k, ln_g, ln_b)


def _sgu_kernel(p_ref, lng_ref, lnb_ref, w_ref, bias_ref, o_ref):
    x = p_ref[...].astype(f32)
    g = 0.5 * x * (1.0 + jnp.tanh(0.7978845608028654 * (x + 0.044715 * (x * x * x))))
    u = g[:, :SG_WIDTH]
    v = g[:, SG_WIDTH:]
    mu = jnp.mean(v, axis=-1, keepdims=True)
    vc = v - mu
    var = jnp.mean(vc * vc, axis=-1, keepdims=True)
    vn = (vc * lax.rsqrt(var + 1e-5) * lng_ref[...] + lnb_ref[...]).astype(bf16)
    first = lax.broadcasted_iota(jnp.int32, (SG_CHUNK, LANES), 1) < (LANES // 2)
    for q in range(SG_GROUPS // 2):
        sl = slice(q * LANES, (q + 1) * LANES)
        vq = vn[:, sl]
        lo = jnp.dot(w_ref[2 * q], vq, preferred_element_type=f32)
        hi = jnp.dot(w_ref[2 * q + 1], vq, preferred_element_type=f32)
        mixed = jnp.where(first, lo, hi) + bias_ref[:, sl]
        o_ref[:, sl] = (u[:, sl] * mixed).astype(o_ref.dtype)


def sgu(p_sg, ln_g, ln_b, w_tril, bias_full):
    t = p_sg.shape[0]
    return pl.pallas_call(
        _sgu_kernel,
        grid=(t // SG_CHUNK,),
        in_specs=[pl.BlockSpec((SG_CHUNK, 2 * SG_WIDTH), lambda i: (i, 0)),
                  pl.BlockSpec((1, SG_WIDTH), lambda i: (0, 0)),
                  pl.BlockSpec((1, SG_WIDTH), lambda i: (0, 0)),
                  pl.BlockSpec((SG_GROUPS, SG_CHUNK, SG_CHUNK), lambda i: (0, 0, 0)),
                  pl.BlockSpec((SG_CHUNK, SG_WIDTH), lambda i: (0, 0))],
        out_specs=pl.BlockSpec((SG_CHUNK, SG_WIDTH), lambda i: (i, 0)),
        out_shape=jax.ShapeDtypeStruct((t, SG_WIDTH), bf16),
        compiler_params=_cparams(("parallel",)),
        name="sgu",
    )(p_sg, ln_g, ln_b, w_tril, bias_full)


MLA_TM = 512
ATT_QK_PAD = 256


def _mla_prep_kernel(p_ref, qg_ref, kvg_ref, wq_ref, wkv_ref, cs_ref, sn_ref, q_ref, kv_ref, kr_ref):
    x = p_ref[...]

    def rms(y, g):
        return y * lax.rsqrt(jnp.mean(y * y, axis=-1, keepdims=True) + NORM_EPS) * g

    q = _dot(rms(x[:, :MLA_Q_LORA], qg_ref[...]), wq_ref[...])
    kv_ref[...] = _dot(rms(x[:, MLA_Q_LORA:MLA_Q_LORA + MLA_KV_LORA], kvg_ref[...]), wkv_ref[...]).astype(bf16)
    cs = cs_ref[...]
    sn = sn_ref[...]
    half = lax.broadcasted_iota(jnp.int32, cs.shape, 1) < (MLA_QK_ROPE // 2)

    def rope(y):
        swapped = jnp.where(half, pltpu.roll(y, LANES - MLA_QK_ROPE // 2, 1), pltpu.roll(y, MLA_QK_ROPE // 2, 1))
        return y * cs + swapped * sn

    kr_ref[...] = rope(x[:, MLA_Q_LORA + MLA_KV_LORA:]).astype(bf16)
    for h in range(MLA_HEADS):
        base = h * ATT_QK_PAD
        q_ref[:, base:base + LANES] = q[:, base:base + LANES].astype(bf16)
        q_ref[:, base + LANES:base + 2 * LANES] = rope(q[:, base + LANES:base + 2 * LANES]).astype(bf16)


def mla_prep(p_at, qg, kvg, wq, wkv, cs, sn):
    t = p_at.shape[0]
    full = lambda shape: pl.BlockSpec(shape, lambda i: (0, 0))
    return pl.pallas_call(
        _mla_prep_kernel,
        grid=(t // MLA_TM,),
        in_specs=[pl.BlockSpec((MLA_TM, MLA_COLS_PAD), lambda i: (i, 0)),
                  full((1, MLA_Q_LORA)), full((1, MLA_KV_LORA)),
                  full((MLA_Q_LORA, MLA_HEADS * ATT_QK_PAD)), full((MLA_KV_LORA, MLA_HEADS * 256)),
                  pl.BlockSpec((MLA_TM, LANES), lambda i: (i, 0)), pl.BlockSpec((MLA_TM, LANES), lambda i: (i, 0))],
        out_specs=[pl.BlockSpec((MLA_TM, MLA_HEADS * ATT_QK_PAD), lambda i: (i, 0)),
                   pl.BlockSpec((MLA_TM, MLA_HEADS * 256), lambda i: (i, 0)),
                   pl.BlockSpec((MLA_TM, LANES), lambda i: (i, 0))],
        out_shape=[jax.ShapeDtypeStruct((t, MLA_HEADS * ATT_QK_PAD), bf16),
                   jax.ShapeDtypeStruct((t, MLA_HEADS * 256), bf16),
                   jax.ShapeDtypeStruct((t, LANES), bf16)],
        compiler_params=_cparams(("parallel",)),
        name="mla_prep",
    )(p_at, qg, kvg, wq, wkv, cs, sn)


ATT_TQ = 1024
ATT_TK = 512


def _attn_kernel(q_ref, kn_ref, kr_ref, v_ref, o_ref):
    i = pl.program_id(2)
    hq = ATT_TQ // 2
    qs = [q_ref[:hq, :], q_ref[hq:, :]]

    def keys(j):
        rows = pl.ds(pl.multiple_of(j * ATT_TK, ATT_TK), ATT_TK)
        return jnp.concatenate([kn_ref[rows, :], kr_ref[rows, :]], axis=1), v_ref[rows, :]

    def update(carry, s, vj):
        m_old, l_old, acc = carry
        m_new = jnp.maximum(m_old, jnp.max(s, axis=-1, keepdims=True))
        p = jnp.exp(s - m_new)
        alpha = jnp.exp(m_old - m_new)
        l_new = alpha * l_old + jnp.sum(p, axis=-1, keepdims=True)
        acc = alpha * acc + jnp.dot(p.astype(bf16), vj, preferred_element_type=f32)
        return m_new, l_new, acc

    def body(j, carry):
        kj, vj = keys(j)
        return tuple(update(carry[h], _dot_nt(qs[h], kj), vj) for h in range(2))

    init = tuple((jnp.full((hq, 1), -jnp.inf, f32), jnp.zeros((hq, 1), f32), jnp.zeros((hq, MLA_V_DIM), f32))
                 for _ in range(2))
    carry = lax.fori_loop(0, 2 * i, body, init)
    rowi = lax.broadcasted_iota(jnp.int32, (hq, ATT_TK), 0)
    coli = lax.broadcasted_iota(jnp.int32, (hq, ATT_TK), 1)
    diag = coli > rowi
    kj, vj = keys(2 * i)
    c0 = update(carry[0], jnp.where(diag, -jnp.inf, _dot_nt(qs[0], kj)), vj)
    c1 = update(carry[1], _dot_nt(qs[1], kj), vj)
    kj, vj = keys(2 * i + 1)
    c1 = update(c1, jnp.where(diag, -jnp.inf, _dot_nt(qs[1], kj)), vj)
    o_ref[:hq, :] = (c0[2] / c0[1]).astype(o_ref.dtype)
    o_ref[hq:, :] = (c1[2] / c1[1]).astype(o_ref.dtype)


def attention(q, kv, kr):
    b, s, _ = q.shape
    return pl.pallas_call(
        _attn_kernel,
        grid=(b, MLA_HEADS, s // ATT_TQ),
        in_specs=[pl.BlockSpec((None, ATT_TQ, ATT_QK_PAD), lambda bi, h, i: (bi, i, h)),
                  pl.BlockSpec((None, s, MLA_QK_NOPE), lambda bi, h, i: (bi, 0, 2 * h)),
                  pl.BlockSpec((None, s, LANES), lambda bi, h, i: (bi, 0, 0)),
                  pl.BlockSpec((None, s, MLA_V_DIM), lambda bi, h, i: (bi, 0, 2 * h + 1))],
        out_specs=pl.BlockSpec((None, ATT_TQ, MLA_V_DIM), lambda bi, h, i: (bi, i, h)),
        out_shape=jax.ShapeDtypeStruct((b, s, MLA_HEADS * MLA_V_DIM), bf16),
        compiler_params=_cparams(("parallel", "parallel", "parallel")),
        name="attention",
    )(q, kv, kr, kv)


OUT_TM = 512


def _merge_kernel(yr_ref, ys_ref, ya_ref, gate_ref, wr_ref, ws_ref, wa_ref, o_ref):
    d = D_MODEL
    acc = jax.nn.sigmoid(gate_ref[:, :d].astype(f32)) * jnp.dot(yr_ref[...], wr_ref[...], preferred_element_type=f32)
    acc += jax.nn.sigmoid(gate_ref[:, d:2 * d].astype(f32)) * jnp.dot(ys_ref[...], ws_ref[...],
                                                                      preferred_element_type=f32)
    acc += jax.nn.sigmoid(gate_ref[:, 2 * d:].astype(f32)) * jnp.dot(ya_ref[...], wa_ref[...],
                                                                     preferred_element_type=f32)
    o_ref[...] = acc.astype(o_ref.dtype)


def merge(y_rw, y_sg, y_at, p_gate, w_rw, w_sg, w_at):
    t = y_rw.shape[0]
    yspec = pl.BlockSpec((OUT_TM, RW_WIDTH), lambda i: (i, 0))
    wspec = pl.BlockSpec((RW_WIDTH, D_MODEL), lambda i: (0, 0), pipeline_mode=pl.Buffered(1))
    return pl.pallas_call(
        _merge_kernel,
        grid=(t // OUT_TM,),
        in_specs=[yspec, yspec, yspec, pl.BlockSpec((OUT_TM, 3 * D_MODEL), lambda i: (i, 0)), wspec, wspec, wspec],
        out_specs=pl.BlockSpec((OUT_TM, D_MODEL), lambda i: (i, 0)),
        out_shape=jax.ShapeDtypeStruct((t, D_MODEL), bf16),
        compiler_params=_cparams(("parallel",)),
        name="merge",
    )(y_rw, y_sg, y_at, p_gate, w_rw, w_sg, w_at)


def _out_kernel(mix_ref, wo_ref, x_ref, gate_ref, ng_ref, scale_ref, shift_ref, xo_ref, h_ref):
    xn = x_ref[...] + gate_ref[...] * jnp.dot(mix_ref[...], wo_ref[...], preferred_element_type=f32)
    xo_ref[...] = xn
    y = xn * lax.rsqrt(jnp.mean(xn * xn, axis=-1, keepdims=True) + NORM_EPS) * ng_ref[...]
    h_ref[...] = (y * (1.0 + scale_ref[...]) + shift_ref[...]).astype(h_ref.dtype)


def out_proj(mix, w_o, x, gate, norm_g, scale, shift, seq):
    t, d = x.shape
    per_b = pl.BlockSpec((None, 1, d), lambda i: (i * OUT_TM // seq, 0, 0))
    rows = pl.BlockSpec((OUT_TM, d), lambda i: (i, 0))
    return pl.pallas_call(
        _out_kernel,
        grid=(t // OUT_TM,),
        in_specs=[rows, pl.BlockSpec((d, d), lambda i: (0, 0), pipeline_mode=pl.Buffered(1)), rows, per_b,
                  pl.BlockSpec((1, d), lambda i: (0, 0)), per_b, per_b],
        out_specs=[rows, rows],
        out_shape=[jax.ShapeDtypeStruct((t, d), f32), jax.ShapeDtypeStruct((t, d), bf16)],
        compiler_params=_cparams(("parallel",)),
        name="out_proj",
    )(mix, w_o, x, gate, norm_g, scale, shift)


def _moe_kernel(be_ref, nu_ref, x_ref, rw_ref, wg_ref, wu_ref, wd_ref, o_ref):
    i = pl.program_id(0)

    @pl.when(i < nu_ref[0])
    def _():
        x = x_ref[...]
        g = jnp.dot(x, wg_ref[...].astype(bf16), preferred_element_type=f32)
        u = jnp.dot(x, wu_ref[...].astype(bf16), preferred_element_type=f32)
        hmid = (g * jax.nn.sigmoid(g) * u * rw_ref[...]).astype(bf16)
        o_ref[...] = jnp.dot(hmid, wd_ref[...].astype(bf16), preferred_element_type=f32).astype(o_ref.dtype)

    @pl.when(i >= nu_ref[0])
    def _():
        o_ref[...] = jnp.zeros_like(o_ref)


def moe_ffn(block_expert, n_used, xb, row_weight, w_gate, w_up, w_down, layer):
    rows, d = xb.shape
    nblk = rows // MOE_BLOCK

    def xmap(i, be, nu):
        return (jnp.minimum(i, nu[0] - 1), 0)

    def wmap(i, be, nu):
        return (layer, be[i], 0, 0)

    return pl.pallas_call(
        _moe_kernel,
        grid_spec=pltpu.PrefetchScalarGridSpec(
            num_scalar_prefetch=2,
            grid=(nblk,),
            in_specs=[pl.BlockSpec((MOE_BLOCK, d), xmap),
                      pl.BlockSpec((MOE_BLOCK, 1), xmap),
                      pl.BlockSpec((None, None, d, MOE_D_FF), wmap),
                      pl.BlockSpec((None, None, d, MOE_D_FF), wmap),
                      pl.BlockSpec((None, None, MOE_D_FF, d), wmap)],
            out_specs=pl.BlockSpec((MOE_BLOCK, d), lambda i, be, nu: (i, 0)),
        ),
        out_shape=jax.ShapeDtypeStruct((rows, d), bf16),
        compiler_params=_cparams(("arbitrary",)),
        name="moe_ffn",
    )(block_expert, n_used, xb, row_weight, w_gate, w_up, w_down)


def _rms(x, g):
    return x * lax.rsqrt(jnp.mean(x * x, axis=-1, keepdims=True) + NORM_EPS) * g


def _pad_cols(w, width):
    return jnp.pad(w, ((0, 0), (0, width - w.shape[1])))


def _pad_rows(w, height):
    return jnp.pad(w, ((0, height - w.shape[0]), (0, 0)))


def _moe(h, layer, g_w, g_b, e_w, e_b, w_gate, w_up, w_down):
    t, d = h.shape
    w_r = _pad_cols(jnp.concatenate([g_w, e_w], axis=1), LANES)
    logits = matmul(h, w_r, tm=2048, tn=LANES, name="router")
    group_logits = logits[:, :MOE_GROUPS] + g_b
    group = jnp.argmax(group_logits, axis=-1)
    group_w = jnp.take_along_axis(jax.nn.softmax(group_logits, axis=-1), group[:, None], axis=-1)
    exp_logits = (logits[:, MOE_GROUPS:MOE_GROUPS + MOE_EXPERTS] + e_b).reshape(t, MOE_GROUPS, MOE_EPG)
    in_group = jnp.take_along_axis(exp_logits, group[:, None, None], axis=1)[:, 0]
    top_logit, top_idx = lax.top_k(in_group, MOE_TOP_K)
    weights = (group_w * jax.nn.softmax(top_logit, axis=-1)).reshape(-1)
    expert_ids = (group[:, None] * MOE_EPG + top_idx).reshape(-1).astype(jnp.int32)
    n_assign = t * MOE_TOP_K
    n_blocks = -(-n_assign // MOE_BLOCK) + MOE_EXPERTS
    rows = n_blocks * MOE_BLOCK
    onehot = (expert_ids[:, None] == jnp.arange(MOE_EXPERTS, dtype=jnp.int32)[None, :]).astype(jnp.int32)
    csum = jnp.cumsum(onehot, axis=0)
    rank = jnp.take_along_axis(csum, expert_ids[:, None], axis=1)[:, 0] - 1
    counts = csum[-1]
    padded = (counts + MOE_BLOCK - 1) // MOE_BLOCK * MOE_BLOCK
    ends = jnp.cumsum(padded)
    starts = ends - padded
    dest = starts[expert_ids] + rank
    token_ids = jnp.arange(n_assign, dtype=jnp.int32) // MOE_TOP_K
    row_token = jnp.full((rows,), t, jnp.int32).at[dest].set(token_ids)
    row_weight = jnp.zeros((rows,), f32).at[dest].set(weights)
    block_start = jnp.arange(n_blocks, dtype=jnp.int32) * MOE_BLOCK
    block_expert = jnp.minimum(jnp.searchsorted(ends, block_start, side='right'),
                               MOE_EXPERTS - 1).astype(jnp.int32)
    n_used = (ends[-1] // MOE_BLOCK).astype(jnp.int32).reshape(1)
    h_pad = jnp.concatenate([h, jnp.zeros((1, d), h.dtype)], axis=0)
    xb = h_pad[row_token]
    yb = moe_ffn(block_expert, n_used, xb, row_weight[:, None], w_gate, w_up, w_down, layer)
    dest2 = dest.reshape(t, MOE_TOP_K)
    y0, y1 = lax.optimization_barrier((yb[dest2[:, 0]], yb[dest2[:, 1]]))
    return y0, y1


def kernel(x, c, positions, ada_w, ada_b, norm1_g, norm2_g, final_g, w_in, rw_mu, rw_w0, rw_w2, rw_a0, rw_a2, rw_g2, rw_k_k, rw_k_a, rw_r_k, rw_ln_g, rw_ln_b, sg_ln_g, sg_ln_b, sg_w, sg_b, mla_q_norm_g, mla_w_uq, mla_kv_norm_g, mla_w_ukv, p_rwkv, p_sgu, p_mla, w_o, router_g_w, router_g_b, router_e_w, router_e_b, exp_w_gate, exp_w_up, exp_w_down):
    bsz, seq, d = x.shape
    t = bsz * seq
    half = MLA_QK_ROPE // 2
    inv_freq = ROPE_BASE ** (-jnp.arange(half, dtype=f32) / half)
    ang = (positions.astype(f32)[..., None] * inv_freq).reshape(t, half)
    cos, sin = jnp.cos(ang), jnp.sin(ang)
    zpad = jnp.zeros((t, LANES - MLA_QK_ROPE), f32)
    rope_cs = jnp.concatenate([cos, cos, zpad], axis=1)
    rope_sn = jnp.concatenate([-sin, sin, zpad], axis=1)
    c_act = jnp.pad(jax.nn.silu(c), ((0, 8 - bsz), (0, 0)))
    x = x.reshape(t, d)
    row = lambda vec: vec[None, :]

    def per_batch(vec):
        return vec[:, None, :]

    mods = []
    for l in range(DEPTH):
        mod = matmul(c_act, ada_w, tm=8, tn=1024, layer=l, name="ada")[:bsz] + ada_b[l]
        mods.append(jnp.split(mod, 6, axis=-1))

    shift1, scale1 = mods[0][0], mods[0][1]
    h = ((_rms(x, norm1_g[0]).reshape(bsz, seq, d) * (1.0 + scale1[:, None, :]) + shift1[:, None, :])
         .reshape(t, d).astype(bf16))
    for l in range(DEPTH):
        _, _, gate1, shift2, scale2, gate2 = mods[l]
        wl = w_in[l]
        w_rw = jnp.concatenate([wl[:, :3072], _pad_cols(wl[:, 3072:3168], LANES),
                                _pad_cols(wl[:, 3168:3264], LANES), wl[:, 3264:3520]], axis=1).astype(bf16)
        mul = rw_mu[l]
        mu = jnp.concatenate([mul[:3072], jnp.pad(mul[3072:3168], (0, 32)),
                              jnp.pad(mul[3168:3264], (0, 32)), mul[3264:3520]])
        w_sg = wl[:, 3520:5568].astype(bf16)
        w_at = _pad_cols(wl[:, 5568:6400], MLA_COLS_PAD).astype(bf16)
        w_gt = wl[:, 6400:].astype(bf16)
        p_rw = matmul(h, w_rw, tm=1024, tn=896, name="in_rw").reshape(bsz, seq, RW_COLS_PAD)
        p_sg = matmul(h, w_sg, tm=1024, tn=1024, out_dtype=bf16, name="in_sg")
        p_at = matmul(h, w_at, tm=1024, tn=MLA_COLS_PAD, name="in_at")
        p_gate = matmul(h, w_gt, tm=1024, tn=1024, out_dtype=bf16, name="in_gate")
        lw, a, g = rw_lora(p_rw, row(mu[3 * RW_WIDTH:]), row(rw_w0[l]), row(rw_a0[l]),
                           _pad_rows(rw_w2[l], LANES).astype(bf16), _pad_rows(rw_a2[l], LANES).astype(bf16),
                           rw_g2[l].astype(bf16))
        y_rw = rwkv_mix(p_rw, lw, a, g, row(mu[:3 * RW_WIDTH]), row(rw_k_k[l]), row(rw_k_a[l]),
                        row(rw_r_k[l].reshape(-1)), row(rw_ln_g[l]), row(rw_ln_b[l])).reshape(t, RW_WIDTH)
        bias_full = jnp.repeat(sg_b[l].T, RW_HEAD_DIM, axis=1)
        y_sg = sgu(p_sg, row(sg_ln_g[l]), row(sg_ln_b[l]), jnp.tril(sg_w[l]).astype(bf16), bias_full)
        wq = mla_w_uq[l].reshape(MLA_Q_LORA, MLA_HEADS, MLA_QK_DIM) * (MLA_QK_DIM ** -0.5)
        wq = jnp.pad(wq, ((0, 0), (0, 0), (0, ATT_QK_PAD - MLA_QK_DIM))).reshape(MLA_Q_LORA, -1).astype(bf16)
        q, kv, kr = mla_prep(p_at, row(mla_q_norm_g[l]), row(mla_kv_norm_g[l]), wq, mla_w_ukv[l].astype(bf16),
                             rope_cs, rope_sn)
        y_at = attention(q.reshape(bsz, seq, -1), kv.reshape(bsz, seq, -1),
                         kr.reshape(bsz, seq, LANES)).reshape(t, -1)
        mix = merge(y_rw, y_sg, y_at, p_gate, p_rwkv[l].astype(bf16), p_sgu[l].astype(bf16), p_mla[l].astype(bf16))
        x, h2 = out_proj(mix, w_o[l].astype(bf16), x, per_batch(gate1), row(norm2_g[l]), per_batch(scale2),
                         per_batch(shift2), seq)
        y0, y1 = _moe(h2, l, router_g_w[l], router_g_b[l], router_e_w[l], router_e_b[l],
                      exp_w_gate, exp_w_up, exp_w_down)
        y = (y0.astype(f32) + y1.astype(f32)).reshape(bsz, seq, d)
        x = (x.reshape(bsz, seq, d) + gate2[:, None, :] * y).reshape(t, d)
        if l + 1 < DEPTH:
            shift1, scale1 = mods[l + 1][0], mods[l + 1][1]
            h = ((_rms(x, norm1_g[l + 1]).reshape(bsz, seq, d) * (1.0 + scale1[:, None, :]) + shift1[:, None, :])
                 .reshape(t, d).astype(bf16))
    return _rms(x, final_g).reshape(bsz, seq, d)
```

```python
import jax
import jax.numpy as jnp
from jax import lax
from jax.experimental import pallas as pl
from jax.experimental.pallas import tpu as pltpu

f32 = jnp.float32
bf16 = jnp.bfloat16

D_MODEL = 2048
DEPTH = 4
RW_HEADS = 16
RW_HEAD_DIM = 64
RW_WIDTH = 1024
RW_LORA_COLS = 512
RW_COLS_PAD = 3 * RW_WIDTH + RW_LORA_COLS
RW_GN_EPS = 64e-5
SG_CHUNK = 128
SG_GROUPS = 16
SG_WIDTH = 1024
MLA_HEADS = 8
MLA_Q_LORA = 512
MLA_KV_LORA = 256
MLA_QK_NOPE = 128
MLA_QK_ROPE = 64
MLA_QK_DIM = 192
MLA_V_DIM = 128
MLA_COLS_PAD = 896
ROPE_BASE = 10000.0
MOE_GROUPS = 8
MOE_EPG = 8
MOE_EXPERTS = 64
MOE_TOP_K = 2
MOE_D_FF = 384
MOE_BLOCK = 128
NORM_EPS = 1e-6
LOG2_E = 1.4426950408889634

LANES = 128
RW_CHUNK = 64
VMEM_LIMIT = 48 * 1024 * 1024


def _cparams(sem):
    return pltpu.CompilerParams(dimension_semantics=sem, vmem_limit_bytes=VMEM_LIMIT)


def _dot(a, b):
    return jnp.dot(a.astype(bf16), b.astype(bf16), preferred_element_type=f32)


def _dot_nt(a, b):
    return lax.dot_general(a.astype(bf16), b.astype(bf16), (((1,), (1,)), ((), ())),
                           preferred_element_type=f32)


def _dot_tn(a, b):
    return lax.dot_general(a.astype(bf16), b.astype(bf16), (((0,), (0,)), ((), ())),
                           preferred_element_type=f32)


def _mm_kernel(a_ref, w_ref, o_ref):
    o_ref[...] = _dot(a_ref[...], w_ref[...]).astype(o_ref.dtype)


def matmul(a, w, *, tm, tn, out_dtype=f32, layer=None, name="mm"):
    m, k = a.shape
    n = w.shape[-1]
    assert m % tm == 0 and n % tn == 0, (m, tm, n, tn)
    if layer is None:
        w_spec = pl.BlockSpec((k, tn), lambda j, i: (0, j))
    else:
        w_spec = pl.BlockSpec((None, k, tn), lambda j, i: (layer, 0, j))
    return pl.pallas_call(
        _mm_kernel,
        grid=(n // tn, m // tm),
        in_specs=[pl.BlockSpec((tm, k), lambda j, i: (i, 0)), w_spec],
        out_specs=pl.BlockSpec((tm, tn), lambda j, i: (i, j)),
        out_shape=jax.ShapeDtypeStruct((m, n), out_dtype),
        compiler_params=_cparams(("parallel", "parallel")),
        name=name,
    )(a, w)


RW_LORA_TM = 1024


def _softplus(z):
    return jnp.maximum(z, 0.0) + jnp.log(1.0 + jnp.exp(-jnp.abs(z)))


def _rw_lora_kernel(p_ref, mu_ref, w0_ref, a0_ref, ww_ref, wa_ref, wg_ref, lw_ref, a_ref, g_ref, prev_ref):
    @pl.when(pl.program_id(1) == 0)
    def _():
        prev_ref[...] = jnp.zeros_like(prev_ref)

    x = p_ref[...]
    rowi = lax.broadcasted_iota(jnp.int32, x.shape, 0)
    xprev = jnp.where(rowi == 0, prev_ref[...], pltpu.roll(x, 1, 0))
    prev_ref[...] = x[RW_LORA_TM - 1:RW_LORA_TM, :]
    xs = x + mu_ref[...] * (xprev - x)
    dec = _dot(jnp.tanh(xs[:, :LANES]), ww_ref[...])
    log_w = -_softplus(-(w0_ref[...] + dec)) - 0.5
    lw_ref[...] = -jnp.exp(log_w)
    a_ref[...] = jax.nn.sigmoid(a0_ref[...] + _dot(xs[:, LANES:2 * LANES], wa_ref[...]))
    g_ref[...] = _dot(jax.nn.sigmoid(xs[:, 2 * LANES:]), wg_ref[...]).astype(g_ref.dtype)


def rw_lora(p_rw, mu_lora, w0, a0, w_w2, w_a2, w_g2):
    b, s, _ = p_rw.shape
    cb = 3 * RW_WIDTH // RW_LORA_COLS
    vec = pl.BlockSpec((1, RW_WIDTH), lambda i, j: (0, 0))
    out = pl.BlockSpec((None, RW_LORA_TM, RW_WIDTH), lambda i, j: (i, j, 0))
    return pl.pallas_call(
        _rw_lora_kernel,
        grid=(b, s // RW_LORA_TM),
        in_specs=[pl.BlockSpec((None, RW_LORA_TM, RW_LORA_COLS), lambda i, j: (i, j, cb)),
                  pl.BlockSpec((1, RW_LORA_COLS), lambda i, j: (0, 0)), vec, vec,
                  pl.BlockSpec((LANES, RW_WIDTH), lambda i, j: (0, 0)),
                  pl.BlockSpec((LANES, RW_WIDTH), lambda i, j: (0, 0)),
                  pl.BlockSpec((2 * LANES, RW_WIDTH), lambda i, j: (0, 0))],
        out_specs=[out, out, out],
        out_shape=[jax.ShapeDtypeStruct((b, s, RW_WIDTH), f32), jax.ShapeDtypeStruct((b, s, RW_WIDTH), f32),
                   jax.ShapeDtypeStruct((b, s, RW_WIDTH), bf16)],
        scratch_shapes=[pltpu.VMEM((1, RW_LORA_COLS), f32)],
        compiler_params=_cparams(("parallel", "arbitrary")),
        name="rw_lora",
    )(p_rw, mu_lora, w0, a0, w_w2, w_a2, w_g2)


def _rwkv_kernel(p_ref, lw_ref, a_ref, g_ref, mu_ref, kk_ref, ka_ref, rk_ref, lng_ref, lnb_ref,
                 o_ref, state_ref, prev_ref):
    c = RW_CHUNK
    n = 2 * c
    pairs = range(RW_HEADS // 2)

    @pl.when(pl.program_id(1) == 0)
    def _():
        state_ref[...] = jnp.zeros_like(state_ref)
        prev_ref[...] = jnp.zeros_like(prev_ref)

    row = lax.broadcasted_iota(jnp.int32, (n, n), 0)
    col = lax.broadcasted_iota(jnp.int32, (n, n), 1)
    same = (row >> 6) == (col >> 6)
    rpos = row & (c - 1)
    cpos = col & (c - 1)
    strict = same & (cpos < rpos)
    incl = same & (cpos <= rpos)
    eye = (row == col).astype(f32)
    blk16 = (row >> 4) == (col >> 4)
    blk32 = (row >> 5) == (col >> 5)
    low32 = blk32 & jnp.logical_not(blk16)
    low64 = same & jnp.logical_not(blk32)
    tr = lax.broadcasted_iota(jnp.int32, (c, c), 0)
    tc = lax.broadcasted_iota(jnp.int32, (c, c), 1)
    tril1 = (tc <= tr).astype(bf16)
    m0 = lax.broadcasted_iota(jnp.int32, (c, n), 1) < c
    row0 = lax.broadcasted_iota(jnp.int32, (c, n), 0) == 0

    def expand(y):
        return jnp.concatenate([jnp.where(m0, y, 0.0), jnp.where(m0, 0.0, y)], axis=0)

    def collapse(y):
        return y[:c] + y[c:]

    def head_sum(y):
        lo = jnp.sum(jnp.where(m0, y, 0.0), axis=-1, keepdims=True)
        hi = jnp.sum(jnp.where(m0, 0.0, y), axis=-1, keepdims=True)
        return jnp.where(m0, lo, hi)

    def shifted(base, p):
        sl = slice(base + p * LANES, base + (p + 1) * LANES)
        x = p_ref[:, sl]
        xprev = jnp.where(row0, prev_ref[:, sl], pltpu.roll(x, 1, 0))
        prev_ref[:, sl] = x[c - 1:c, :]
        return x + mu_ref[:, sl] * (xprev - x)

    sls = [slice(p * LANES, (p + 1) * LANES) for p in pairs]
    r = [shifted(0, p) for p in pairs]
    k0 = [shifted(RW_WIDTH, p) for p in pairs]
    v = [shifted(2 * RW_WIDTH, p) for p in pairs]
    a = [a_ref[:, s] for s in sls]
    lw = [lw_ref[:, s] for s in sls]
    kkr = [k0[p] * kk_ref[:, sls[p]] for p in pairs]
    kk = [kkr[p] * lax.rsqrt(jnp.maximum(head_sum(kkr[p] * kkr[p]), 1e-24)) for p in pairs]
    k = [k0[p] * (1.0 + (a[p] - 1.0) * ka_ref[:, sls[p]]) for p in pairs]

    def cumsum(x):
        x1 = x.astype(bf16)
        res = x - x1.astype(f32)
        x2 = res.astype(bf16)
        x3 = (res - x2.astype(f32)).astype(bf16)
        return (jnp.dot(tril1, x1, preferred_element_type=f32) + jnp.dot(tril1, x2, preferred_element_type=f32)
                + jnp.dot(tril1, x3, preferred_element_type=f32))

    cum = [cumsum(lw[p]) for p in pairs]
    ctot = [cum[p][c - 1:c, :] for p in pairs]
    ginv = [jnp.exp(-cum[p]) for p in pairs]
    g2 = [jnp.exp(ctot[p] - cum[p]) for p in pairs]
    b = [kk[p] * a[p] for p in pairs]
    at = [-kk[p] * jnp.exp(cum[p] - lw[p]) for p in pairs]
    rt = [r[p] * jnp.exp(cum[p]) for p in pairs]
    bt = [b[p] * ginv[p] for p in pairs]
    kt = [k[p] * ginv[p] for p in pairs]
    pm = [_dot_nt(jnp.concatenate([expand(at[p]), expand(rt[p])], axis=0),
                  jnp.concatenate([expand(bt[p]), expand(kt[p])], axis=0)) for p in pairs]
    a_ab = [jnp.where(strict, pm[p][:n, :n], 0.0) for p in pairs]
    a_ak = [jnp.where(strict, pm[p][:n, n:], 0.0).astype(bf16) for p in pairs]
    m_rb = [jnp.where(incl, pm[p][n:, :n], 0.0).astype(bf16) for p in pairs]
    m_rk = [jnp.where(incl, pm[p][n:, n:], 0.0).astype(bf16) for p in pairs]
    ev = [expand(v[p]).astype(bf16) for p in pairs]
    akv = [_dot(a_ak[p], ev[p]) for p in pairs]
    o0 = [collapse(_dot(m_rk[p], ev[p])) for p in pairs]
    kv = [jnp.where(same, _dot_tn(v[p], k[p] * g2[p]), 0.0) for p in pairs]
    a0 = [jnp.where(blk16, a_ab[p], 0.0) for p in pairs]
    a2 = [_dot(a0[p], a0[p]) for p in pairs]
    p2 = [eye + a0[p] + a2[p] + _dot(a0[p], a2[p]) for p in pairs]
    a4 = [_dot(a2[p], a2[p]) for p in pairs]
    p3 = [p2[p] + _dot(p2[p], a4[p]) for p in pairs]
    a8 = [_dot(a4[p], a4[p]) for p in pairs]
    d = [p3[p] + _dot(p3[p], a8[p]) for p in pairs]
    x1 = [_dot(jnp.where(low32, a_ab[p], 0.0), d[p]) for p in pairs]
    d = [d[p] + _dot(d[p], x1[p]) for p in pairs]
    x2 = [_dot(jnp.where(low64, a_ab[p], 0.0), d[p]) for p in pairs]
    t = [(d[p] + _dot(d[p], x2[p])).astype(bf16) for p in pairs]
    tx = [_dot(t[p], jnp.concatenate([expand(at[p]), akv[p]], axis=1)) for p in pairs]
    at_hat = [collapse(tx[p][:, :n]) for p in pairs]
    u0 = [collapse(tx[p][:, n:]) for p in pairs]
    s0 = [state_ref[p] for p in pairs]
    so = [_dot_nt(jnp.concatenate([at_hat[p], rt[p]], axis=0), s0[p]) for p in pairs]
    u = [so[p][:c] + u0[p] for p in pairs]
    su = [jnp.where(same, _dot_tn(u[p], b[p] * g2[p]), 0.0) for p in pairs]
    mu_ = [collapse(_dot(m_rb[p], expand(u[p]))) for p in pairs]
    for p in pairs:
        state_ref[p] = s0[p] * jnp.exp(ctot[p]) + kv[p] + su[p]
    inv_n = 1.0 / RW_HEAD_DIM
    for p in pairs:
        o = so[p][c:] + o0[p] + mu_[p]
        mean = head_sum(o) * inv_n
        oc = o - mean
        var = head_sum(oc * oc) * inv_n
        y = oc * lax.rsqrt(var + RW_GN_EPS) * lng_ref[:, sls[p]] + lnb_ref[:, sls[p]]
        y = y + head_sum(r[p] * k[p] * rk_ref[:, sls[p]]) * v[p]
        o_ref[:, sls[p]] = (y * g_ref[:, sls[p]].astype(f32)).astype(o_ref.dtype)


def rwkv_mix(p_rw, lw, a, g, mu, k_k, k_a, r_k, ln_g, ln_b):
    bsz, s, _ = p_rw.shape
    spec = pl.BlockSpec((None, RW_CHUNK, RW_WIDTH), lambda i, j: (i, j, 0))
    vec = pl.BlockSpec((1, RW_WIDTH), lambda i, j: (0, 0))
    return pl.pallas_call(
        _rwkv_kernel,
        grid=(bsz, s // RW_CHUNK),
        in_specs=[pl.BlockSpec((None, RW_CHUNK, 3 * RW_WIDTH), lambda i, j: (i, j, 0)), spec, spec, spec,
                  pl.BlockSpec((1, 3 * RW_WIDTH), lambda i, j: (0, 0)), vec, vec, vec, vec, vec],
        out_specs=spec,
        out_shape=jax.ShapeDtypeStruct((bsz, s, RW_WIDTH), bf16),
        scratch_shapes=[pltpu.VMEM((RW_HEADS // 2, LANES, LANES), f32), pltpu.VMEM((1, 3 * RW_WIDTH), f32)],
        compiler_params=_cparams(("parallel", "arbitrary")),
        name="rwkv_mix",
    )(p_rw, lw, a, g, mu, k_k, k_a, r_k, ln_g, ln_b)


def _sgu_kernel(p_ref, lng_ref, lnb_ref, w_ref, bias_ref, o_ref):
    x = p_ref[...].astype(f32)
    g = 0.5 * x * (1.0 + jnp.tanh(0.7978845608028654 * (x + 0.044715 * (x * x * x))))
    u = g[:, :SG_WIDTH]
    v = g[:, SG_WIDTH:]
    mu = jnp.mean(v, axis=-1, keepdims=True)
    vc = v - mu
    var = jnp.mean(vc * vc, axis=-1, keepdims=True)
    vn = (vc * lax.rsqrt(var + 1e-5) * lng_ref[...] + lnb_ref[...]).astype(bf16)
    first = lax.broadcasted_iota(jnp.int32, (SG_CHUNK, LANES), 1) < (LANES // 2)
    for q in range(SG_GROUPS // 2):
        sl = slice(q * LANES, (q + 1) * LANES)
        vq = vn[:, sl]
        lo = jnp.dot(w_ref[2 * q], vq, preferred_element_type=f32)
        hi = jnp.dot(w_ref[2 * q + 1], vq, preferred_element_type=f32)
        mixed = jnp.where(first, lo, hi) + bias_ref[:, sl]
        o_ref[:, sl] = (u[:, sl] * mixed).astype(o_ref.dtype)


def sgu(p_sg, ln_g, ln_b, w_tril, bias_full):
    t = p_sg.shape[0]
    return pl.pallas_call(
        _sgu_kernel,
        grid=(t // SG_CHUNK,),
        in_specs=[pl.BlockSpec((SG_CHUNK, 2 * SG_WIDTH), lambda i: (i, 0)),
                  pl.BlockSpec((1, SG_WIDTH), lambda i: (0, 0)),
                  pl.BlockSpec((1, SG_WIDTH), lambda i: (0, 0)),
                  pl.BlockSpec((SG_GROUPS, SG_CHUNK, SG_CHUNK), lambda i: (0, 0, 0)),
                  pl.BlockSpec((SG_CHUNK, SG_WIDTH), lambda i: (0, 0))],
        out_specs=pl.BlockSpec((SG_CHUNK, SG_WIDTH), lambda i: (i, 0)),
        out_shape=jax.ShapeDtypeStruct((t, SG_WIDTH), bf16),
        compiler_params=_cparams(("parallel",)),
        name="sgu",
    )(p_sg, ln_g, ln_b, w_tril, bias_full)


MLA_TM = 512
ATT_QK_PAD = 256


def _mla_prep_kernel(p_ref, qg_ref, kvg_ref, wq_ref, wkv_ref, cs_ref, sn_ref, q_ref, kv_ref, kr_ref):
    x = p_ref[...]

    def rms(y, g):
        return y * lax.rsqrt(jnp.mean(y * y, axis=-1, keepdims=True) + NORM_EPS) * g

    q = _dot(rms(x[:, :MLA_Q_LORA], qg_ref[...]), wq_ref[...])
    kv_ref[...] = _dot(rms(x[:, MLA_Q_LORA:MLA_Q_LORA + MLA_KV_LORA], kvg_ref[...]), wkv_ref[...]).astype(bf16)
    cs = cs_ref[...]
    sn = sn_ref[...]
    half = lax.broadcasted_iota(jnp.int32, cs.shape, 1) < (MLA_QK_ROPE // 2)

    def rope(y):
        swapped = jnp.where(half, pltpu.roll(y, LANES - MLA_QK_ROPE // 2, 1), pltpu.roll(y, MLA_QK_ROPE // 2, 1))
        return y * cs + swapped * sn

    kr_ref[...] = rope(x[:, MLA_Q_LORA + MLA_KV_LORA:]).astype(bf16)
    for h in range(MLA_HEADS):
        base = h * ATT_QK_PAD
        q_ref[:, base:base + LANES] = q[:, base:base + LANES].astype(bf16)
        q_ref[:, base + LANES:base + 2 * LANES] = rope(q[:, base + LANES:base + 2 * LANES]).astype(bf16)


def mla_prep(p_at, qg, kvg, wq, wkv, cs, sn):
    t = p_at.shape[0]
    full = lambda shape: pl.BlockSpec(shape, lambda i: (0, 0))
    return pl.pallas_call(
        _mla_prep_kernel,
        grid=(t // MLA_TM,),
        in_specs=[pl.BlockSpec((MLA_TM, MLA_COLS_PAD), lambda i: (i, 0)),
                  full((1, MLA_Q_LORA)), full((1, MLA_KV_LORA)),
                  full((MLA_Q_LORA, MLA_HEADS * ATT_QK_PAD)), full((MLA_KV_LORA, MLA_HEADS * 256)),
                  pl.BlockSpec((MLA_TM, LANES), lambda i: (i, 0)), pl.BlockSpec((MLA_TM, LANES), lambda i: (i, 0))],
        out_specs=[pl.BlockSpec((MLA_TM, MLA_HEADS * ATT_QK_PAD), lambda i: (i, 0)),
                   pl.BlockSpec((MLA_TM, MLA_HEADS * 256), lambda i: (i, 0)),
                   pl.BlockSpec((MLA_TM, LANES), lambda i: (i, 0))],
        out_shape=[jax.ShapeDtypeStruct((t, MLA_HEADS * ATT_QK_PAD), bf16),
                   jax.ShapeDtypeStruct((t, MLA_HEADS * 256), bf16),
                   jax.ShapeDtypeStruct((t, LANES), bf16)],
        compiler_params=_cparams(("parallel",)),
        name="mla_prep",
    )(p_at, qg, kvg, wq, wkv, cs, sn)


ATT_TQ = 1024
ATT_TK = 512


def _attn_kernel(q_ref, kn_ref, kr_ref, v_ref, o_ref):
    i = pl.program_id(2)
    hq = ATT_TQ // 2
    qs = [q_ref[:hq, :], q_ref[hq:, :]]

    def keys(j):
        rows = pl.ds(pl.multiple_of(j * ATT_TK, ATT_TK), ATT_TK)
        return jnp.concatenate([kn_ref[rows, :], kr_ref[rows, :]], axis=1), v_ref[rows, :]

    def update(carry, s, vj):
        m_old, l_old, acc = carry
        m_new = jnp.maximum(m_old, jnp.max(s, axis=-1, keepdims=True))
        p = jnp.exp2(s - m_new)
        alpha = jnp.exp2(m_old - m_new)
        l_new = alpha * l_old + jnp.sum(p, axis=-1, keepdims=True)
        acc = alpha * acc + jnp.dot(p.astype(bf16), vj, preferred_element_type=f32)
        return m_new, l_new, acc

    def body(j, carry):
        ka, va = keys(2 * j)
        kb, vb = keys(2 * j + 1)
        sa = [_dot_nt(qs[h], ka) for h in range(2)]
        sb = [_dot_nt(qs[h], kb) for h in range(2)]
        carry = tuple(update(carry[h], sa[h], va) for h in range(2))
        return tuple(update(carry[h], sb[h], vb) for h in range(2))

    init = tuple((jnp.full((hq, 1), -jnp.inf, f32), jnp.zeros((hq, 1), f32), jnp.zeros((hq, MLA_V_DIM), f32))
                 for _ in range(2))
    carry = lax.fori_loop(0, i, body, init)
    rowi = lax.broadcasted_iota(jnp.int32, (hq, ATT_TK), 0)
    coli = lax.broadcasted_iota(jnp.int32, (hq, ATT_TK), 1)
    diag = coli > rowi
    kj, vj = keys(2 * i)
    c0 = update(carry[0], jnp.where(diag, -jnp.inf, _dot_nt(qs[0], kj)), vj)
    c1 = update(carry[1], _dot_nt(qs[1], kj), vj)
    kj, vj = keys(2 * i + 1)
    c1 = update(c1, jnp.where(diag, -jnp.inf, _dot_nt(qs[1], kj)), vj)
    o_ref[:hq, :] = (c0[2] / c0[1]).astype(o_ref.dtype)
    o_ref[hq:, :] = (c1[2] / c1[1]).astype(o_ref.dtype)


def attention(q, kv, kr):
    b, s, _ = q.shape
    return pl.pallas_call(
        _attn_kernel,
        grid=(b, MLA_HEADS, s // ATT_TQ),
        in_specs=[pl.BlockSpec((None, ATT_TQ, ATT_QK_PAD), lambda bi, h, i: (bi, i, h)),
                  pl.BlockSpec((None, s, MLA_QK_NOPE), lambda bi, h, i: (bi, 0, 2 * h)),
                  pl.BlockSpec((None, s, LANES), lambda bi, h, i: (bi, 0, 0)),
                  pl.BlockSpec((None, s, MLA_V_DIM), lambda bi, h, i: (bi, 0, 2 * h + 1))],
        out_specs=pl.BlockSpec((None, ATT_TQ, MLA_V_DIM), lambda bi, h, i: (bi, i, h)),
        out_shape=jax.ShapeDtypeStruct((b, s, MLA_HEADS * MLA_V_DIM), bf16),
        compiler_params=_cparams(("parallel", "parallel", "parallel")),
        name="attention",
    )(q, kv, kr, kv)


OUT_TM = 512


def _merge_kernel(yr_ref, ys_ref, ya_ref, gate_ref, wr_ref, ws_ref, wa_ref, o_ref):
    d = D_MODEL
    acc = jax.nn.sigmoid(gate_ref[:, :d].astype(f32)) * jnp.dot(yr_ref[...], wr_ref[...], preferred_element_type=f32)
    acc += jax.nn.sigmoid(gate_ref[:, d:2 * d].astype(f32)) * jnp.dot(ys_ref[...], ws_ref[...],
                                                                      preferred_element_type=f32)
    acc += jax.nn.sigmoid(gate_ref[:, 2 * d:].astype(f32)) * jnp.dot(ya_ref[...], wa_ref[...],
                                                                     preferred_element_type=f32)
    o_ref[...] = acc.astype(o_ref.dtype)


def merge(y_rw, y_sg, y_at, p_gate, w_rw, w_sg, w_at):
    t = y_rw.shape[0]
    yspec = pl.BlockSpec((OUT_TM, RW_WIDTH), lambda i: (i, 0))
    wspec = pl.BlockSpec((RW_WIDTH, D_MODEL), lambda i: (0, 0), pipeline_mode=pl.Buffered(1))
    return pl.pallas_call(
        _merge_kernel,
        grid=(t // OUT_TM,),
        in_specs=[yspec, yspec, yspec, pl.BlockSpec((OUT_TM, 3 * D_MODEL), lambda i: (i, 0)), wspec, wspec, wspec],
        out_specs=pl.BlockSpec((OUT_TM, D_MODEL), lambda i: (i, 0)),
        out_shape=jax.ShapeDtypeStruct((t, D_MODEL), bf16),
        compiler_params=_cparams(("parallel",)),
        name="merge",
    )(y_rw, y_sg, y_at, p_gate, w_rw, w_sg, w_at)


def _out_kernel(mix_ref, wo_ref, x_ref, gate_ref, ng_ref, scale_ref, shift_ref, xo_ref, h_ref):
    xn = x_ref[...] + gate_ref[...] * jnp.dot(mix_ref[...], wo_ref[...], preferred_element_type=f32)
    xo_ref[...] = xn
    y = xn * lax.rsqrt(jnp.mean(xn * xn, axis=-1, keepdims=True) + NORM_EPS) * ng_ref[...]
    h_ref[...] = (y * (1.0 + scale_ref[...]) + shift_ref[...]).astype(h_ref.dtype)


def out_proj(mix, w_o, x, gate, norm_g, scale, shift, seq):
    t, d = x.shape
    per_b = pl.BlockSpec((None, 1, d), lambda i: (i * OUT_TM // seq, 0, 0))
    rows = pl.BlockSpec((OUT_TM, d), lambda i: (i, 0))
    return pl.pallas_call(
        _out_kernel,
        grid=(t // OUT_TM,),
        in_specs=[rows, pl.BlockSpec((d, d), lambda i: (0, 0), pipeline_mode=pl.Buffered(1)), rows, per_b,
                  pl.BlockSpec((1, d), lambda i: (0, 0)), per_b, per_b],
        out_specs=[rows, rows],
        out_shape=[jax.ShapeDtypeStruct((t, d), f32), jax.ShapeDtypeStruct((t, d), bf16)],
        compiler_params=_cparams(("parallel",)),
        name="out_proj",
    )(mix, w_o, x, gate, norm_g, scale, shift)


def _moe_kernel(be_ref, slot_ref, nu_ref, x_ref, rw_ref, wg_ref, wu_ref, wd_ref, o_ref, wg_s, wu_s, wd_s):
    s = pl.program_id(0)
    nblk = pl.num_programs(0) - 1
    cur = jnp.minimum(s, nblk - 1)
    prv = jnp.maximum(s - 1, 0)

    @pl.when((s < nu_ref[0]) & ((s == 0) | (be_ref[cur] != be_ref[prv])))
    def _():
        k = slot_ref[cur]
        wg_s[k] = wg_ref[...].astype(bf16)
        wu_s[k] = wu_ref[...].astype(bf16)
        wd_s[k] = wd_ref[...].astype(bf16)

    @pl.when((s >= 1) & (s <= nu_ref[0]))
    def _():
        k = slot_ref[prv]
        x = x_ref[...]
        g = jnp.dot(x, wg_s[k], preferred_element_type=f32)
        u = jnp.dot(x, wu_s[k], preferred_element_type=f32)
        hmid = (g * jax.nn.sigmoid(g) * u * rw_ref[...]).astype(bf16)
        o_ref[...] = jnp.dot(hmid, wd_s[k], preferred_element_type=f32).astype(o_ref.dtype)

    @pl.when(s > nu_ref[0])
    def _():
        o_ref[...] = jnp.zeros_like(o_ref)


def moe_ffn(block_expert, n_used, xb, row_weight, w_gate, w_up, w_down, layer):
    rows, d = xb.shape
    nblk = rows // MOE_BLOCK
    changed = jnp.concatenate([jnp.zeros((1,), jnp.int32),
                               (block_expert[1:] != block_expert[:-1]).astype(jnp.int32)])
    slot = jnp.cumsum(changed) % 2

    def xmap(s, be, sl, nu):
        return (jnp.clip(s - 1, 0, nu[0] - 1), 0)

    def wmap(s, be, sl, nu):
        return (layer, be[jnp.minimum(s, nblk - 1)], 0, 0)

    return pl.pallas_call(
        _moe_kernel,
        grid_spec=pltpu.PrefetchScalarGridSpec(
            num_scalar_prefetch=3,
            grid=(nblk + 1,),
            in_specs=[pl.BlockSpec((MOE_BLOCK, d), xmap),
                      pl.BlockSpec((MOE_BLOCK, 1), xmap),
                      pl.BlockSpec((None, None, d, MOE_D_FF), wmap),
                      pl.BlockSpec((None, None, d, MOE_D_FF), wmap),
                      pl.BlockSpec((None, None, MOE_D_FF, d), wmap)],
            out_specs=pl.BlockSpec((MOE_BLOCK, d), lambda s, be, sl, nu: (jnp.maximum(s - 1, 0), 0)),
            scratch_shapes=[pltpu.VMEM((2, d, MOE_D_FF), bf16), pltpu.VMEM((2, d, MOE_D_FF), bf16),
                            pltpu.VMEM((2, MOE_D_FF, d), bf16)],
        ),
        out_shape=jax.ShapeDtypeStruct((rows, d), bf16),
        compiler_params=_cparams(("arbitrary",)),
        name="moe_ffn",
    )(block_expert, slot.astype(jnp.int32), n_used, xb, row_weight, w_gate, w_up, w_down)


def _rms(x, g):
    return x * lax.rsqrt(jnp.mean(x * x, axis=-1, keepdims=True) + NORM_EPS) * g


def _pad_cols(w, width):
    return jnp.pad(w, ((0, 0), (0, width - w.shape[1])))


def _pad_rows(w, height):
    return jnp.pad(w, ((0, height - w.shape[0]), (0, 0)))


def _moe(h, layer, g_w, g_b, e_w, e_b, w_gate, w_up, w_down):
    t, d = h.shape
    w_r = _pad_cols(jnp.concatenate([g_w, e_w], axis=1), LANES)
    logits = matmul(h, w_r, tm=2048, tn=LANES, name="router")
    group_logits = logits[:, :MOE_GROUPS] + g_b
    group = jnp.argmax(group_logits, axis=-1)
    group_w = jnp.take_along_axis(jax.nn.softmax(group_logits, axis=-1), group[:, None], axis=-1)
    exp_logits = (logits[:, MOE_GROUPS:MOE_GROUPS + MOE_EXPERTS] + e_b).reshape(t, MOE_GROUPS, MOE_EPG)
    in_group = jnp.take_along_axis(exp_logits, group[:, None, None], axis=1)[:, 0]
    top_logit, top_idx = lax.top_k(in_group, MOE_TOP_K)
    weights = (group_w * jax.nn.softmax(top_logit, axis=-1)).reshape(-1)
    expert_ids = (group[:, None] * MOE_EPG + top_idx).reshape(-1).astype(jnp.int32)
    n_assign = t * MOE_TOP_K
    n_blocks = -(-n_assign // MOE_BLOCK) + MOE_EXPERTS
    rows = n_blocks * MOE_BLOCK
    experts = jnp.arange(MOE_EXPERTS, dtype=jnp.int32)
    order = jnp.argsort(expert_ids).astype(jnp.int32)
    sorted_e = expert_ids[order]
    first = jnp.searchsorted(sorted_e, experts, side='left').astype(jnp.int32)
    counts = jnp.searchsorted(sorted_e, experts, side='right').astype(jnp.int32) - first
    padded = (counts + MOE_BLOCK - 1) // MOE_BLOCK * MOE_BLOCK
    ends = jnp.cumsum(padded)
    starts = ends - padded
    block_start = jnp.arange(n_blocks, dtype=jnp.int32) * MOE_BLOCK
    block_expert = jnp.minimum(jnp.searchsorted(ends, block_start, side='right'),
                               MOE_EXPERTS - 1).astype(jnp.int32)
    n_used = (ends[-1] // MOE_BLOCK).astype(jnp.int32).reshape(1)
    row_e = jnp.repeat(block_expert, MOE_BLOCK)
    idx = jnp.arange(rows, dtype=jnp.int32) - starts[row_e]
    valid = (idx >= 0) & (idx < counts[row_e])
    src = order[jnp.clip(first[row_e] + idx, 0, n_assign - 1)]
    row_token = jnp.where(valid, src // MOE_TOP_K, t)
    row_weight = jnp.where(valid, weights[src], 0.0)
    position = jnp.argsort(order).astype(jnp.int32)
    dest = starts[expert_ids] + position - first[expert_ids]
    h_pad = jnp.concatenate([h, jnp.zeros((1, d), h.dtype)], axis=0)
    xb = h_pad[row_token]
    yb = moe_ffn(block_expert, n_used, xb, row_weight[:, None], w_gate, w_up, w_down, layer)
    return yb[dest.reshape(t, MOE_TOP_K).T.reshape(-1)]


def _combine_kernel(y0_ref, y1_ref, x_ref, gate_ref, ng_ref, scale_ref, shift_ref, xo_ref, h_ref):
    xn = x_ref[...] + gate_ref[...] * (y0_ref[...].astype(f32) + y1_ref[...].astype(f32))
    xo_ref[...] = xn
    y = xn * lax.rsqrt(jnp.mean(xn * xn, axis=-1, keepdims=True) + NORM_EPS) * ng_ref[...]
    h_ref[...] = (y * (1.0 + scale_ref[...]) + shift_ref[...]).astype(h_ref.dtype)


def _final_kernel(y0_ref, y1_ref, x_ref, gate_ref, ng_ref, o_ref):
    xn = x_ref[...] + gate_ref[...] * (y0_ref[...].astype(f32) + y1_ref[...].astype(f32))
    o_ref[...] = xn * lax.rsqrt(jnp.mean(xn * xn, axis=-1, keepdims=True) + NORM_EPS) * ng_ref[...]


def combine(y01, x, gate, norm_g, scale, shift, seq):
    t, d = x.shape
    nb = t // OUT_TM
    per_b = pl.BlockSpec((None, 1, d), lambda i: (i * OUT_TM // seq, 0, 0))
    rows = pl.BlockSpec((OUT_TM, d), lambda i: (i, 0))
    second = pl.BlockSpec((OUT_TM, d), lambda i: (i + nb, 0))
    vec = pl.BlockSpec((1, d), lambda i: (0, 0))
    if scale is None:
        return pl.pallas_call(
            _final_kernel, grid=(nb,), in_specs=[rows, second, rows, per_b, vec], out_specs=rows,
            out_shape=jax.ShapeDtypeStruct((t, d), f32), compiler_params=_cparams(("parallel",)),
            name="final_norm")(y01, y01, x, gate, norm_g)
    return pl.pallas_call(
        _combine_kernel, grid=(nb,), in_specs=[rows, second, rows, per_b, vec, per_b, per_b],
        out_specs=[rows, rows],
        out_shape=[jax.ShapeDtypeStruct((t, d), f32), jax.ShapeDtypeStruct((t, d), bf16)],
        compiler_params=_cparams(("parallel",)), name="combine")(y01, y01, x, gate, norm_g, scale, shift)


def kernel(x, c, positions, ada_w, ada_b, norm1_g, norm2_g, final_g, w_in, rw_mu, rw_w0, rw_w2, rw_a0, rw_a2, rw_g2, rw_k_k, rw_k_a, rw_r_k, rw_ln_g, rw_ln_b, sg_ln_g, sg_ln_b, sg_w, sg_b, mla_q_norm_g, mla_w_uq, mla_kv_norm_g, mla_w_ukv, p_rwkv, p_sgu, p_mla, w_o, router_g_w, router_g_b, router_e_w, router_e_b, exp_w_gate, exp_w_up, exp_w_down):
    bsz, seq, d = x.shape
    t = bsz * seq
    half = MLA_QK_ROPE // 2
    inv_freq = ROPE_BASE ** (-jnp.arange(half, dtype=f32) / half)
    ang = (positions.astype(f32)[..., None] * inv_freq).reshape(t, half)
    cos, sin = jnp.cos(ang), jnp.sin(ang)
    zpad = jnp.zeros((t, LANES - MLA_QK_ROPE), f32)
    rope_cs = jnp.concatenate([cos, cos, zpad], axis=1)
    rope_sn = jnp.concatenate([-sin, sin, zpad], axis=1)
    c_act = jnp.pad(jax.nn.silu(c), ((0, 8 - bsz), (0, 0)))
    x = x.reshape(t, d)
    row = lambda vec: vec[None, :]

    def per_batch(vec):
        return vec[:, None, :]

    mods = []
    for l in range(DEPTH):
        mod = matmul(c_act, ada_w, tm=8, tn=1024, layer=l, name="ada")[:bsz] + ada_b[l]
        mods.append(jnp.split(mod, 6, axis=-1))

    shift1, scale1 = mods[0][0], mods[0][1]
    h = ((_rms(x, norm1_g[0]).reshape(bsz, seq, d) * (1.0 + scale1[:, None, :]) + shift1[:, None, :])
         .reshape(t, d).astype(bf16))
    for l in range(DEPTH):
        _, _, gate1, shift2, scale2, gate2 = mods[l]
        wl = w_in[l]
        w_rw = jnp.concatenate([wl[:, :3072], _pad_cols(wl[:, 3072:3168], LANES),
                                _pad_cols(wl[:, 3168:3264], LANES), wl[:, 3264:3520]], axis=1).astype(bf16)
        mul = rw_mu[l]
        mu = jnp.concatenate([mul[:3072], jnp.pad(mul[3072:3168], (0, 32)),
                              jnp.pad(mul[3168:3264], (0, 32)), mul[3264:3520]])
        w_sg = wl[:, 3520:5568].astype(bf16)
        w_at = _pad_cols(wl[:, 5568:6400], MLA_COLS_PAD).astype(bf16)
        w_gt = wl[:, 6400:].astype(bf16)
        p_rw = matmul(h, w_rw, tm=1024, tn=896, name="in_rw").reshape(bsz, seq, RW_COLS_PAD)
        p_sg = matmul(h, w_sg, tm=1024, tn=1024, out_dtype=bf16, name="in_sg")
        p_at = matmul(h, w_at, tm=1024, tn=MLA_COLS_PAD, name="in_at")
        p_gate = matmul(h, w_gt, tm=1024, tn=1024, out_dtype=bf16, name="in_gate")
        lw, a, g = rw_lora(p_rw, row(mu[3 * RW_WIDTH:]), row(rw_w0[l]), row(rw_a0[l]),
                           _pad_rows(rw_w2[l], LANES).astype(bf16), _pad_rows(rw_a2[l], LANES).astype(bf16),
                           rw_g2[l].astype(bf16))
        y_rw = rwkv_mix(p_rw, lw, a, g, row(mu[:3 * RW_WIDTH]), row(rw_k_k[l]), row(rw_k_a[l]),
                        row(rw_r_k[l].reshape(-1)), row(rw_ln_g[l]), row(rw_ln_b[l])).reshape(t, RW_WIDTH)
        bias_full = jnp.repeat(sg_b[l].T, RW_HEAD_DIM, axis=1)
        y_sg = sgu(p_sg, row(sg_ln_g[l]), row(sg_ln_b[l]), jnp.tril(sg_w[l]).astype(bf16), bias_full)
        wq = mla_w_uq[l].reshape(MLA_Q_LORA, MLA_HEADS, MLA_QK_DIM) * (MLA_QK_DIM ** -0.5 * LOG2_E)
        wq = jnp.pad(wq, ((0, 0), (0, 0), (0, ATT_QK_PAD - MLA_QK_DIM))).reshape(MLA_Q_LORA, -1).astype(bf16)
        q, kv, kr = mla_prep(p_at, row(mla_q_norm_g[l]), row(mla_kv_norm_g[l]), wq, mla_w_ukv[l].astype(bf16),
                             rope_cs, rope_sn)
        y_at = attention(q.reshape(bsz, seq, -1), kv.reshape(bsz, seq, -1),
                         kr.reshape(bsz, seq, LANES)).reshape(t, -1)
        mix = merge(y_rw, y_sg, y_at, p_gate, p_rwkv[l].astype(bf16), p_sgu[l].astype(bf16), p_mla[l].astype(bf16))
        x, h2 = out_proj(mix, w_o[l].astype(bf16), x, per_batch(gate1), row(norm2_g[l]), per_batch(scale2),
                         per_batch(shift2), seq)
        y01 = _moe(h2, l, router_g_w[l], router_g_b[l], router_e_w[l], router_e_b[l],
                   exp_w_gate, exp_w_up, exp_w_down)
        if l + 1 < DEPTH:
            x, h = combine(y01, x, per_batch(gate2), row(norm1_g[l + 1]), per_batch(mods[l + 1][1]),
                           per_batch(mods[l + 1][0]), seq)
    return combine(y01, x, per_batch(gate2), row(final_g), None, None, seq).reshape(bsz, seq, d)
```

```python
import jax
import jax.numpy as jnp
from jax import lax
from jax.experimental import pallas as pl
from jax.experimental.pallas import tpu as pltpu

f32 = jnp.float32
bf16 = jnp.bfloat16

D_MODEL = 2048
DEPTH = 4
RW_HEADS = 16
RW_HEAD_DIM = 64
RW_WIDTH = 1024
RW_LORA_COLS = 512
RW_COLS_PAD = 3 * RW_WIDTH + RW_LORA_COLS
RW_GN_EPS = 64e-5
SG_CHUNK = 128
SG_GROUPS = 16
SG_WIDTH = 1024
MLA_HEADS = 8
MLA_Q_LORA = 512
MLA_KV_LORA = 256
MLA_QK_NOPE = 128
MLA_QK_ROPE = 64
MLA_QK_DIM = 192
MLA_V_DIM = 128
MLA_COLS_PAD = 896
ROPE_BASE = 10000.0
MOE_GROUPS = 8
MOE_EPG = 8
MOE_EXPERTS = 64
MOE_TOP_K = 2
MOE_D_FF = 384
MOE_BLOCK = 256
NORM_EPS = 1e-6
LOG2_E = 1.4426950408889634

LANES = 128
RW_CHUNK = 64
VMEM_LIMIT = 48 * 1024 * 1024


def _cparams(sem):
    return pltpu.CompilerParams(dimension_semantics=sem, vmem_limit_bytes=VMEM_LIMIT)


def _dot(a, b):
    return jnp.dot(a.astype(bf16), b.astype(bf16), preferred_element_type=f32)


def _dot_nt(a, b):
    return lax.dot_general(a.astype(bf16), b.astype(bf16), (((1,), (1,)), ((), ())),
                           preferred_element_type=f32)


def _dot_tn(a, b):
    return lax.dot_general(a.astype(bf16), b.astype(bf16), (((0,), (0,)), ((), ())),
                           preferred_element_type=f32)


def _mm_kernel(a_ref, w_ref, o_ref):
    o_ref[...] = _dot(a_ref[...], w_ref[...]).astype(o_ref.dtype)


def matmul(a, w, *, tm, tn, out_dtype=f32, layer=None, name="mm"):
    m, k = a.shape
    n = w.shape[-1]
    assert m % tm == 0 and n % tn == 0, (m, tm, n, tn)
    if layer is None:
        w_spec = pl.BlockSpec((k, tn), lambda j, i: (0, j))
    else:
        w_spec = pl.BlockSpec((None, k, tn), lambda j, i: (layer, 0, j))
    return pl.pallas_call(
        _mm_kernel,
        grid=(n // tn, m // tm),
        in_specs=[pl.BlockSpec((tm, k), lambda j, i: (i, 0)), w_spec],
        out_specs=pl.BlockSpec((tm, tn), lambda j, i: (i, j)),
        out_shape=jax.ShapeDtypeStruct((m, n), out_dtype),
        compiler_params=_cparams(("parallel", "parallel")),
        name=name,
    )(a, w)


RW_LORA_TM = 1024


def _softplus(z):
    return jnp.maximum(z, 0.0) + jnp.log(1.0 + jnp.exp(-jnp.abs(z)))


def _rw_lora_kernel(p_ref, mu_ref, w0_ref, a0_ref, ww_ref, wa_ref, wg_ref, lw_ref, a_ref, g_ref, prev_ref):
    @pl.when(pl.program_id(1) == 0)
    def _():
        prev_ref[...] = jnp.zeros_like(prev_ref)

    x = p_ref[...]
    rowi = lax.broadcasted_iota(jnp.int32, x.shape, 0)
    xprev = jnp.where(rowi == 0, prev_ref[...], pltpu.roll(x, 1, 0))
    prev_ref[...] = x[RW_LORA_TM - 1:RW_LORA_TM, :]
    xs = x + mu_ref[...] * (xprev - x)
    dec = _dot(jnp.tanh(xs[:, :LANES]), ww_ref[...])
    log_w = -_softplus(-(w0_ref[...] + dec)) - 0.5
    lw_ref[...] = -jnp.exp(log_w)
    a_ref[...] = jax.nn.sigmoid(a0_ref[...] + _dot(xs[:, LANES:2 * LANES], wa_ref[...]))
    g_ref[...] = _dot(jax.nn.sigmoid(xs[:, 2 * LANES:]), wg_ref[...]).astype(g_ref.dtype)


def rw_lora(p_rw, mu_lora, w0, a0, w_w2, w_a2, w_g2):
    b, s, _ = p_rw.shape
    cb = 3 * RW_WIDTH // RW_LORA_COLS
    vec = pl.BlockSpec((1, RW_WIDTH), lambda i, j: (0, 0))
    out = pl.BlockSpec((None, RW_LORA_TM, RW_WIDTH), lambda i, j: (i, j, 0))
    return pl.pallas_call(
        _rw_lora_kernel,
        grid=(b, s // RW_LORA_TM),
        in_specs=[pl.BlockSpec((None, RW_LORA_TM, RW_LORA_COLS), lambda i, j: (i, j, cb)),
                  pl.BlockSpec((1, RW_LORA_COLS), lambda i, j: (0, 0)), vec, vec,
                  pl.BlockSpec((LANES, RW_WIDTH), lambda i, j: (0, 0)),
                  pl.BlockSpec((LANES, RW_WIDTH), lambda i, j: (0, 0)),
                  pl.BlockSpec((2 * LANES, RW_WIDTH), lambda i, j: (0, 0))],
        out_specs=[out, out, out],
        out_shape=[jax.ShapeDtypeStruct((b, s, RW_WIDTH), f32), jax.ShapeDtypeStruct((b, s, RW_WIDTH), f32),
                   jax.ShapeDtypeStruct((b, s, RW_WIDTH), bf16)],
        scratch_shapes=[pltpu.VMEM((1, RW_LORA_COLS), f32)],
        compiler_params=_cparams(("parallel", "arbitrary")),
        name="rw_lora",
    )(p_rw, mu_lora, w0, a0, w_w2, w_a2, w_g2)


def _rwkv_kernel(p_ref, lw_ref, a_ref, g_ref, mu_ref, kk_ref, ka_ref, rk_ref, lng_ref, lnb_ref,
                 o_ref, state_ref, prev_ref):
    c = RW_CHUNK
    n = 2 * c
    pairs = range(RW_HEADS // 2)

    @pl.when(pl.program_id(1) == 0)
    def _():
        state_ref[...] = jnp.zeros_like(state_ref)
        prev_ref[...] = jnp.zeros_like(prev_ref)

    row = lax.broadcasted_iota(jnp.int32, (n, n), 0)
    col = lax.broadcasted_iota(jnp.int32, (n, n), 1)
    same = (row >> 6) == (col >> 6)
    rpos = row & (c - 1)
    cpos = col & (c - 1)
    strict = same & (cpos < rpos)
    incl = same & (cpos <= rpos)
    eye = (row == col).astype(f32)
    blk16 = (row >> 4) == (col >> 4)
    blk32 = (row >> 5) == (col >> 5)
    low32 = blk32 & jnp.logical_not(blk16)
    low64 = same & jnp.logical_not(blk32)
    tr = lax.broadcasted_iota(jnp.int32, (c, c), 0)
    tc = lax.broadcasted_iota(jnp.int32, (c, c), 1)
    tril1 = (tc <= tr).astype(bf16)
    m0 = lax.broadcasted_iota(jnp.int32, (c, n), 1) < c
    row0 = lax.broadcasted_iota(jnp.int32, (c, n), 0) == 0

    def expand(y):
        return jnp.concatenate([jnp.where(m0, y, 0.0), jnp.where(m0, 0.0, y)], axis=0)

    def collapse(y):
        return y[:c] + y[c:]

    def head_sum(y):
        lo = jnp.sum(jnp.where(m0, y, 0.0), axis=-1, keepdims=True)
        hi = jnp.sum(jnp.where(m0, 0.0, y), axis=-1, keepdims=True)
        return jnp.where(m0, lo, hi)

    def shifted(base, p):
        sl = slice(base + p * LANES, base + (p + 1) * LANES)
        x = p_ref[:, sl]
        xprev = jnp.where(row0, prev_ref[:, sl], pltpu.roll(x, 1, 0))
        prev_ref[:, sl] = x[c - 1:c, :]
        return x + mu_ref[:, sl] * (xprev - x)

    sls = [slice(p * LANES, (p + 1) * LANES) for p in pairs]
    r = [shifted(0, p) for p in pairs]
    k0 = [shifted(RW_WIDTH, p) for p in pairs]
    v = [shifted(2 * RW_WIDTH, p) for p in pairs]
    a = [a_ref[:, s] for s in sls]
    lw = [lw_ref[:, s] for s in sls]
    kkr = [k0[p] * kk_ref[:, sls[p]] for p in pairs]
    kk = [kkr[p] * lax.rsqrt(jnp.maximum(head_sum(kkr[p] * kkr[p]), 1e-24)) for p in pairs]
    k = [k0[p] * (1.0 + (a[p] - 1.0) * ka_ref[:, sls[p]]) for p in pairs]

    def cumsum(x):
        x1 = x.astype(bf16)
        res = x - x1.astype(f32)
        x2 = res.astype(bf16)
        x3 = (res - x2.astype(f32)).astype(bf16)
        return (jnp.dot(tril1, x1, preferred_element_type=f32) + jnp.dot(tril1, x2, preferred_element_type=f32)
                + jnp.dot(tril1, x3, preferred_element_type=f32))

    cum = [cumsum(lw[p]) for p in pairs]
    ctot = [cum[p][c - 1:c, :] for p in pairs]
    ginv = [jnp.exp(-cum[p]) for p in pairs]
    g2 = [jnp.exp(ctot[p] - cum[p]) for p in pairs]
    b = [kk[p] * a[p] for p in pairs]
    at = [-kk[p] * jnp.exp(cum[p] - lw[p]) for p in pairs]
    rt = [r[p] * jnp.exp(cum[p]) for p in pairs]
    bt = [b[p] * ginv[p] for p in pairs]
    kt = [k[p] * ginv[p] for p in pairs]
    pm = [_dot_nt(jnp.concatenate([expand(at[p]), expand(rt[p])], axis=0),
                  jnp.concatenate([expand(bt[p]), expand(kt[p])], axis=0)) for p in pairs]
    a_ab = [jnp.where(strict, pm[p][:n, :n], 0.0) for p in pairs]
    a_ak = [jnp.where(strict, pm[p][:n, n:], 0.0).astype(bf16) for p in pairs]
    m_rb = [jnp.where(incl, pm[p][n:, :n], 0.0).astype(bf16) for p in pairs]
    m_rk = [jnp.where(incl, pm[p][n:, n:], 0.0).astype(bf16) for p in pairs]
    ev = [expand(v[p]).astype(bf16) for p in pairs]
    akv = [_dot(a_ak[p], ev[p]) for p in pairs]
    o0 = [collapse(_dot(m_rk[p], ev[p])) for p in pairs]
    kv = [jnp.where(same, _dot_tn(v[p], k[p] * g2[p]), 0.0) for p in pairs]
    a0 = [jnp.where(blk16, a_ab[p], 0.0) for p in pairs]
    a2 = [_dot(a0[p], a0[p]) for p in pairs]
    p2 = [eye + a0[p] + a2[p] + _dot(a0[p], a2[p]) for p in pairs]
    a4 = [_dot(a2[p], a2[p]) for p in pairs]
    p3 = [p2[p] + _dot(p2[p], a4[p]) for p in pairs]
    a8 = [_dot(a4[p], a4[p]) for p in pairs]
    d = [p3[p] + _dot(p3[p], a8[p]) for p in pairs]
    x1 = [_dot(jnp.where(low32, a_ab[p], 0.0), d[p]) for p in pairs]
    d = [d[p] + _dot(d[p], x1[p]) for p in pairs]
    x2 = [_dot(jnp.where(low64, a_ab[p], 0.0), d[p]) for p in pairs]
    t = [(d[p] + _dot(d[p], x2[p])).astype(bf16) for p in pairs]
    tx = [_dot(t[p], jnp.concatenate([expand(at[p]), akv[p]], axis=1)) for p in pairs]
    at_hat = [collapse(tx[p][:, :n]) for p in pairs]
    u0 = [collapse(tx[p][:, n:]) for p in pairs]
    s0 = [state_ref[p] for p in pairs]
    so = [_dot_nt(jnp.concatenate([at_hat[p], rt[p]], axis=0), s0[p]) for p in pairs]
    u = [so[p][:c] + u0[p] for p in pairs]
    su = [jnp.where(same, _dot_tn(u[p], b[p] * g2[p]), 0.0) for p in pairs]
    mu_ = [collapse(_dot(m_rb[p], expand(u[p]))) for p in pairs]
    for p in pairs:
        state_ref[p] = s0[p] * jnp.exp(ctot[p]) + kv[p] + su[p]
    inv_n = 1.0 / RW_HEAD_DIM
    for p in pairs:
        o = so[p][c:] + o0[p] + mu_[p]
        mean = head_sum(o) * inv_n
        oc = o - mean
        var = head_sum(oc * oc) * inv_n
        y = oc * lax.rsqrt(var + RW_GN_EPS) * lng_ref[:, sls[p]] + lnb_ref[:, sls[p]]
        y = y + head_sum(r[p] * k[p] * rk_ref[:, sls[p]]) * v[p]
        o_ref[:, sls[p]] = (y * g_ref[:, sls[p]].astype(f32)).astype(o_ref.dtype)


def rwkv_mix(p_rw, lw, a, g, mu, k_k, k_a, r_k, ln_g, ln_b):
    bsz, s, _ = p_rw.shape
    spec = pl.BlockSpec((None, RW_CHUNK, RW_WIDTH), lambda i, j: (i, j, 0))
    vec = pl.BlockSpec((1, RW_WIDTH), lambda i, j: (0, 0))
    return pl.pallas_call(
        _rwkv_kernel,
        grid=(bsz, s // RW_CHUNK),
        in_specs=[pl.BlockSpec((None, RW_CHUNK, 3 * RW_WIDTH), lambda i, j: (i, j, 0)), spec, spec, spec,
                  pl.BlockSpec((1, 3 * RW_WIDTH), lambda i, j: (0, 0)), vec, vec, vec, vec, vec],
        out_specs=spec,
        out_shape=jax.ShapeDtypeStruct((bsz, s, RW_WIDTH), bf16),
        scratch_shapes=[pltpu.VMEM((RW_HEADS // 2, LANES, LANES), f32), pltpu.VMEM((1, 3 * RW_WIDTH), f32)],
        compiler_params=_cparams(("parallel", "arbitrary")),
        name="rwkv_mix",
    )(p_rw, lw, a, g, mu, k_k, k_a, r_k, ln_g, ln_b)


def _sgu_kernel(p_ref, lng_ref, lnb_ref, w_ref, bias_ref, o_ref):
    x = p_ref[...].astype(f32)
    g = 0.5 * x * (1.0 + jnp.tanh(0.7978845608028654 * (x + 0.044715 * (x * x * x))))
    u = g[:, :SG_WIDTH]
    v = g[:, SG_WIDTH:]
    mu = jnp.mean(v, axis=-1, keepdims=True)
    vc = v - mu
    var = jnp.mean(vc * vc, axis=-1, keepdims=True)
    vn = (vc * lax.rsqrt(var + 1e-5) * lng_ref[...] + lnb_ref[...]).astype(bf16)
    first = lax.broadcasted_iota(jnp.int32, (SG_CHUNK, LANES), 1) < (LANES // 2)
    for q in range(SG_GROUPS // 2):
        sl = slice(q * LANES, (q + 1) * LANES)
        vq = vn[:, sl]
        lo = jnp.dot(w_ref[2 * q], vq, preferred_element_type=f32)
        hi = jnp.dot(w_ref[2 * q + 1], vq, preferred_element_type=f32)
        mixed = jnp.where(first, lo, hi) + bias_ref[:, sl]
        o_ref[:, sl] = (u[:, sl] * mixed).astype(o_ref.dtype)


def sgu(p_sg, ln_g, ln_b, w_tril, bias_full):
    t = p_sg.shape[0]
    return pl.pallas_call(
        _sgu_kernel,
        grid=(t // SG_CHUNK,),
        in_specs=[pl.BlockSpec((SG_CHUNK, 2 * SG_WIDTH), lambda i: (i, 0)),
                  pl.BlockSpec((1, SG_WIDTH), lambda i: (0, 0)),
                  pl.BlockSpec((1, SG_WIDTH), lambda i: (0, 0)),
                  pl.BlockSpec((SG_GROUPS, SG_CHUNK, SG_CHUNK), lambda i: (0, 0, 0)),
                  pl.BlockSpec((SG_CHUNK, SG_WIDTH), lambda i: (0, 0))],
        out_specs=pl.BlockSpec((SG_CHUNK, SG_WIDTH), lambda i: (i, 0)),
        out_shape=jax.ShapeDtypeStruct((t, SG_WIDTH), bf16),
        compiler_params=_cparams(("parallel",)),
        name="sgu",
    )(p_sg, ln_g, ln_b, w_tril, bias_full)


MLA_TM = 512
ATT_QK_PAD = 256


def _mla_prep_kernel(p_ref, qg_ref, kvg_ref, wq_ref, wkv_ref, cs_ref, sn_ref, q_ref, kv_ref, kr_ref):
    x = p_ref[...]

    def rms(y, g):
        return y * lax.rsqrt(jnp.mean(y * y, axis=-1, keepdims=True) + NORM_EPS) * g

    q = _dot(rms(x[:, :MLA_Q_LORA], qg_ref[...]), wq_ref[...])
    kv_ref[...] = _dot(rms(x[:, MLA_Q_LORA:MLA_Q_LORA + MLA_KV_LORA], kvg_ref[...]), wkv_ref[...]).astype(bf16)
    cs = cs_ref[...]
    sn = sn_ref[...]
    half = lax.broadcasted_iota(jnp.int32, cs.shape, 1) < (MLA_QK_ROPE // 2)

    def rope(y):
        swapped = jnp.where(half, pltpu.roll(y, LANES - MLA_QK_ROPE // 2, 1), pltpu.roll(y, MLA_QK_ROPE // 2, 1))
        return y * cs + swapped * sn

    kr_ref[...] = rope(x[:, MLA_Q_LORA + MLA_KV_LORA:]).astype(bf16)
    for h in range(MLA_HEADS):
        base = h * ATT_QK_PAD
        q_ref[:, base:base + LANES] = q[:, base:base + LANES].astype(bf16)
        q_ref[:, base + LANES:base + 2 * LANES] = rope(q[:, base + LANES:base + 2 * LANES]).astype(bf16)


def mla_prep(p_at, qg, kvg, wq, wkv, cs, sn):
    t = p_at.shape[0]
    full = lambda shape: pl.BlockSpec(shape, lambda i: (0, 0))
    return pl.pallas_call(
        _mla_prep_kernel,
        grid=(t // MLA_TM,),
        in_specs=[pl.BlockSpec((MLA_TM, MLA_COLS_PAD), lambda i: (i, 0)),
                  full((1, MLA_Q_LORA)), full((1, MLA_KV_LORA)),
                  full((MLA_Q_LORA, MLA_HEADS * ATT_QK_PAD)), full((MLA_KV_LORA, MLA_HEADS * 256)),
                  pl.BlockSpec((MLA_TM, LANES), lambda i: (i, 0)), pl.BlockSpec((MLA_TM, LANES), lambda i: (i, 0))],
        out_specs=[pl.BlockSpec((MLA_TM, MLA_HEADS * ATT_QK_PAD), lambda i: (i, 0)),
                   pl.BlockSpec((MLA_TM, MLA_HEADS * 256), lambda i: (i, 0)),
                   pl.BlockSpec((MLA_TM, LANES), lambda i: (i, 0))],
        out_shape=[jax.ShapeDtypeStruct((t, MLA_HEADS * ATT_QK_PAD), bf16),
                   jax.ShapeDtypeStruct((t, MLA_HEADS * 256), bf16),
                   jax.ShapeDtypeStruct((t, LANES), bf16)],
        compiler_params=_cparams(("parallel",)),
        name="mla_prep",
    )(p_at, qg, kvg, wq, wkv, cs, sn)


ATT_TQ = 1024
ATT_TK = 512


def _attn_kernel(q_ref, kn_ref, kr_ref, v_ref, o_ref):
    i = pl.program_id(2)
    hq = ATT_TQ // 2
    qs = [q_ref[:hq, :], q_ref[hq:, :]]

    def keys(j):
        rows = pl.ds(pl.multiple_of(j * ATT_TK, ATT_TK), ATT_TK)
        return jnp.concatenate([kn_ref[rows, :], kr_ref[rows, :]], axis=1), v_ref[rows, :]

    def update(carry, s, vj):
        m_old, l_old, acc = carry
        m_new = jnp.maximum(m_old, jnp.max(s, axis=-1, keepdims=True))
        p = jnp.exp2(s - m_new)
        alpha = jnp.exp2(m_old - m_new)
        l_new = alpha * l_old + jnp.sum(p, axis=-1, keepdims=True)
        acc = alpha * acc + jnp.dot(p.astype(bf16), vj, preferred_element_type=f32)
        return m_new, l_new, acc

    def body(j, carry):
        ka, va = keys(2 * j)
        kb, vb = keys(2 * j + 1)
        sa = [_dot_nt(qs[h], ka) for h in range(2)]
        sb = [_dot_nt(qs[h], kb) for h in range(2)]
        carry = tuple(update(carry[h], sa[h], va) for h in range(2))
        return tuple(update(carry[h], sb[h], vb) for h in range(2))

    init = tuple((jnp.full((hq, 1), -jnp.inf, f32), jnp.zeros((hq, 1), f32), jnp.zeros((hq, MLA_V_DIM), f32))
                 for _ in range(2))
    carry = lax.fori_loop(0, i, body, init)
    rowi = lax.broadcasted_iota(jnp.int32, (hq, ATT_TK), 0)
    coli = lax.broadcasted_iota(jnp.int32, (hq, ATT_TK), 1)
    diag = coli > rowi
    kj, vj = keys(2 * i)
    c0 = update(carry[0], jnp.where(diag, -jnp.inf, _dot_nt(qs[0], kj)), vj)
    c1 = update(carry[1], _dot_nt(qs[1], kj), vj)
    kj, vj = keys(2 * i + 1)
    c1 = update(c1, jnp.where(diag, -jnp.inf, _dot_nt(qs[1], kj)), vj)
    o_ref[:hq, :] = (c0[2] / c0[1]).astype(o_ref.dtype)
    o_ref[hq:, :] = (c1[2] / c1[1]).astype(o_ref.dtype)


def attention(q, kv, kr):
    b, s, _ = q.shape
    return pl.pallas_call(
        _attn_kernel,
        grid=(b, MLA_HEADS, s // ATT_TQ),
        in_specs=[pl.BlockSpec((None, ATT_TQ, ATT_QK_PAD), lambda bi, h, i: (bi, i, h)),
                  pl.BlockSpec((None, s, MLA_QK_NOPE), lambda bi, h, i: (bi, 0, 2 * h)),
                  pl.BlockSpec((None, s, LANES), lambda bi, h, i: (bi, 0, 0)),
                  pl.BlockSpec((None, s, MLA_V_DIM), lambda bi, h, i: (bi, 0, 2 * h + 1))],
        out_specs=pl.BlockSpec((None, ATT_TQ, MLA_V_DIM), lambda bi, h, i: (bi, i, h)),
        out_shape=jax.ShapeDtypeStruct((b, s, MLA_HEADS * MLA_V_DIM), bf16),
        compiler_params=_cparams(("parallel", "parallel", "parallel")),
        name="attention",
    )(q, kv, kr, kv)


OUT_TM = 512


def _merge_kernel(yr_ref, ys_ref, ya_ref, gate_ref, wr_ref, ws_ref, wa_ref, o_ref):
    d = D_MODEL
    acc = jax.nn.sigmoid(gate_ref[:, :d].astype(f32)) * jnp.dot(yr_ref[...], wr_ref[...], preferred_element_type=f32)
    acc += jax.nn.sigmoid(gate_ref[:, d:2 * d].astype(f32)) * jnp.dot(ys_ref[...], ws_ref[...],
                                                                      preferred_element_type=f32)
    acc += jax.nn.sigmoid(gate_ref[:, 2 * d:].astype(f32)) * jnp.dot(ya_ref[...], wa_ref[...],
                                                                     preferred_element_type=f32)
    o_ref[...] = acc.astype(o_ref.dtype)


def merge(y_rw, y_sg, y_at, p_gate, w_rw, w_sg, w_at):
    t = y_rw.shape[0]
    yspec = pl.BlockSpec((OUT_TM, RW_WIDTH), lambda i: (i, 0))
    wspec = pl.BlockSpec((RW_WIDTH, D_MODEL), lambda i: (0, 0), pipeline_mode=pl.Buffered(1))
    return pl.pallas_call(
        _merge_kernel,
        grid=(t // OUT_TM,),
        in_specs=[yspec, yspec, yspec, pl.BlockSpec((OUT_TM, 3 * D_MODEL), lambda i: (i, 0)), wspec, wspec, wspec],
        out_specs=pl.BlockSpec((OUT_TM, D_MODEL), lambda i: (i, 0)),
        out_shape=jax.ShapeDtypeStruct((t, D_MODEL), bf16),
        compiler_params=_cparams(("parallel",)),
        name="merge",
    )(y_rw, y_sg, y_at, p_gate, w_rw, w_sg, w_at)


def _out_kernel(mix_ref, wo_ref, x_ref, gate_ref, ng_ref, scale_ref, shift_ref, xo_ref, h_ref):
    xn = x_ref[...] + gate_ref[...] * jnp.dot(mix_ref[...], wo_ref[...], preferred_element_type=f32)
    xo_ref[...] = xn
    y = xn * lax.rsqrt(jnp.mean(xn * xn, axis=-1, keepdims=True) + NORM_EPS) * ng_ref[...]
    h_ref[...] = (y * (1.0 + scale_ref[...]) + shift_ref[...]).astype(h_ref.dtype)


def out_proj(mix, w_o, x, gate, norm_g, scale, shift, seq):
    t, d = x.shape
    per_b = pl.BlockSpec((None, 1, d), lambda i: (i * OUT_TM // seq, 0, 0))
    rows = pl.BlockSpec((OUT_TM, d), lambda i: (i, 0))
    return pl.pallas_call(
        _out_kernel,
        grid=(t // OUT_TM,),
        in_specs=[rows, pl.BlockSpec((d, d), lambda i: (0, 0), pipeline_mode=pl.Buffered(1)), rows, per_b,
                  pl.BlockSpec((1, d), lambda i: (0, 0)), per_b, per_b],
        out_specs=[rows, rows],
        out_shape=[jax.ShapeDtypeStruct((t, d), f32), jax.ShapeDtypeStruct((t, d), bf16)],
        compiler_params=_cparams(("parallel",)),
        name="out_proj",
    )(mix, w_o, x, gate, norm_g, scale, shift)


def _moe_kernel(be_ref, slot_ref, nu_ref, x_ref, rw_ref, wg_ref, wu_ref, wd_ref, o_ref, wgu_s, wd_s):
    s = pl.program_id(0)
    nblk = pl.num_programs(0) - 1
    cur = jnp.minimum(s, nblk - 1)
    prv = jnp.maximum(s - 1, 0)

    @pl.when((s < nu_ref[0]) & ((s == 0) | (be_ref[cur] != be_ref[prv])))
    def _():
        k = slot_ref[cur]
        wgu_s[k, :, :MOE_D_FF] = wg_ref[...].astype(bf16)
        wgu_s[k, :, MOE_D_FF:] = wu_ref[...].astype(bf16)
        wd_s[k] = wd_ref[...].astype(bf16)

    @pl.when((s >= 1) & (s <= nu_ref[0]))
    def _():
        k = slot_ref[prv]
        gu = jnp.dot(x_ref[...], wgu_s[k], preferred_element_type=f32)
        g = gu[:, :MOE_D_FF]
        hmid = (g * jax.nn.sigmoid(g) * gu[:, MOE_D_FF:] * rw_ref[...]).astype(bf16)
        o_ref[...] = jnp.dot(hmid, wd_s[k], preferred_element_type=f32).astype(o_ref.dtype)

    @pl.when(s > nu_ref[0])
    def _():
        o_ref[...] = jnp.zeros_like(o_ref)


def moe_ffn(block_expert, n_used, xb, row_weight, w_gate, w_up, w_down, layer):
    rows, d = xb.shape
    nblk = rows // MOE_BLOCK
    changed = jnp.concatenate([jnp.zeros((1,), jnp.int32),
                               (block_expert[1:] != block_expert[:-1]).astype(jnp.int32)])
    slot = jnp.cumsum(changed) % 2

    def xmap(s, be, sl, nu):
        return (jnp.clip(s - 1, 0, nu[0] - 1), 0)

    def wmap(s, be, sl, nu):
        return (layer, be[jnp.minimum(s, nblk - 1)], 0, 0)

    return pl.pallas_call(
        _moe_kernel,
        grid_spec=pltpu.PrefetchScalarGridSpec(
            num_scalar_prefetch=3,
            grid=(nblk + 1,),
            in_specs=[pl.BlockSpec((MOE_BLOCK, d), xmap),
                      pl.BlockSpec((MOE_BLOCK, 1), xmap),
                      pl.BlockSpec((None, None, d, MOE_D_FF), wmap),
                      pl.BlockSpec((None, None, d, MOE_D_FF), wmap),
                      pl.BlockSpec((None, None, MOE_D_FF, d), wmap)],
            out_specs=pl.BlockSpec((MOE_BLOCK, d), lambda s, be, sl, nu: (jnp.maximum(s - 1, 0), 0)),
            scratch_shapes=[pltpu.VMEM((2, d, 2 * MOE_D_FF), bf16), pltpu.VMEM((2, MOE_D_FF, d), bf16)],
        ),
        out_shape=jax.ShapeDtypeStruct((rows, d), f32),
        compiler_params=_cparams(("arbitrary",)),
        name="moe_ffn",
    )(block_expert, slot.astype(jnp.int32), n_used, xb, row_weight, w_gate, w_up, w_down)


def _rms(x, g):
    return x * lax.rsqrt(jnp.mean(x * x, axis=-1, keepdims=True) + NORM_EPS) * g


def _pad_cols(w, width):
    return jnp.pad(w, ((0, 0), (0, width - w.shape[1])))


def _pad_rows(w, height):
    return jnp.pad(w, ((0, height - w.shape[0]), (0, 0)))


def _moe(h, layer, g_w, g_b, e_w, e_b, w_gate, w_up, w_down):
    t, d = h.shape
    w_r = _pad_cols(jnp.concatenate([g_w, e_w], axis=1), LANES)
    logits = matmul(h, w_r, tm=2048, tn=LANES, name="router")
    group_logits = logits[:, :MOE_GROUPS] + g_b
    group = jnp.argmax(group_logits, axis=-1)
    group_w = jnp.take_along_axis(jax.nn.softmax(group_logits, axis=-1), group[:, None], axis=-1)
    exp_logits = (logits[:, MOE_GROUPS:MOE_GROUPS + MOE_EXPERTS] + e_b).reshape(t, MOE_GROUPS, MOE_EPG)
    in_group = jnp.take_along_axis(exp_logits, group[:, None, None], axis=1)[:, 0]
    top_logit, top_idx = lax.top_k(in_group, MOE_TOP_K)
    weights = (group_w * jax.nn.softmax(top_logit, axis=-1)).reshape(-1)
    expert_ids = (group[:, None] * MOE_EPG + top_idx).reshape(-1).astype(jnp.int32)
    n_assign = t * MOE_TOP_K
    n_blocks = -(-n_assign // MOE_BLOCK) + MOE_EXPERTS
    rows = n_blocks * MOE_BLOCK
    onehot = (expert_ids[:, None] == jnp.arange(MOE_EXPERTS, dtype=jnp.int32)[None, :]).astype(jnp.int32)
    csum = jnp.cumsum(onehot, axis=0)
    rank = jnp.take_along_axis(csum, expert_ids[:, None], axis=1)[:, 0] - 1
    counts = csum[-1]
    padded = (counts + MOE_BLOCK - 1) // MOE_BLOCK * MOE_BLOCK
    ends = jnp.cumsum(padded)
    starts = ends - padded
    dest = starts[expert_ids] + rank
    token_ids = jnp.arange(n_assign, dtype=jnp.int32) // MOE_TOP_K
    row_token = jnp.full((rows,), t, jnp.int32).at[dest].set(token_ids)
    row_weight = jnp.zeros((rows,), f32).at[dest].set(weights)
    block_start = jnp.arange(n_blocks, dtype=jnp.int32) * MOE_BLOCK
    block_expert = jnp.minimum(jnp.searchsorted(ends, block_start, side='right'),
                               MOE_EXPERTS - 1).astype(jnp.int32)
    n_used = (ends[-1] // MOE_BLOCK).astype(jnp.int32).reshape(1)
    h_pad = jnp.concatenate([h, jnp.zeros((1, d), h.dtype)], axis=0)
    xb = h_pad[row_token]
    yb = moe_ffn(block_expert, n_used, xb, row_weight[:, None], w_gate, w_up, w_down, layer)
    dest2 = dest.reshape(t, MOE_TOP_K)
    return yb, dest2[:, 0], dest2[:, 1]


def _gather_rows(d0_ref, d1_ref, yb_ref, buf, sem):
    def issue(r, carry):
        pltpu.make_async_copy(yb_ref.at[pl.ds(d0_ref[0, r], 1), :], buf.at[0, pl.ds(r, 1), :], sem.at[0]).start()
        pltpu.make_async_copy(yb_ref.at[pl.ds(d1_ref[0, r], 1), :], buf.at[1, pl.ds(r, 1), :], sem.at[1]).start()
        return carry

    lax.fori_loop(0, OUT_TM, issue, 0, unroll=8)
    for k in range(MOE_TOP_K):
        pltpu.make_async_copy(yb_ref.at[pl.ds(0, OUT_TM), :], buf.at[k], sem.at[k]).wait()
    return buf[0] + buf[1]


def _combine_kernel(d0_ref, d1_ref, yb_ref, x_ref, gate_ref, ng_ref, scale_ref, shift_ref, xo_ref, h_ref,
                    buf, sem):
    xn = x_ref[...] + gate_ref[...] * _gather_rows(d0_ref, d1_ref, yb_ref, buf, sem)
    xo_ref[...] = xn
    y = xn * lax.rsqrt(jnp.mean(xn * xn, axis=-1, keepdims=True) + NORM_EPS) * ng_ref[...]
    h_ref[...] = (y * (1.0 + scale_ref[...]) + shift_ref[...]).astype(h_ref.dtype)


def _final_kernel(d0_ref, d1_ref, yb_ref, x_ref, gate_ref, ng_ref, o_ref, buf, sem):
    xn = x_ref[...] + gate_ref[...] * _gather_rows(d0_ref, d1_ref, yb_ref, buf, sem)
    o_ref[...] = xn * lax.rsqrt(jnp.mean(xn * xn, axis=-1, keepdims=True) + NORM_EPS) * ng_ref[...]


def combine(yb, dest0, dest1, x, gate, norm_g, scale, shift, seq):
    t, d = x.shape
    nb = t // OUT_TM
    per_b = pl.BlockSpec((None, 1, d), lambda i: (i * OUT_TM // seq, 0, 0))
    rows = pl.BlockSpec((OUT_TM, d), lambda i: (i, 0))
    vec = pl.BlockSpec((1, d), lambda i: (0, 0))
    idx = pl.BlockSpec((None, 1, OUT_TM), lambda i: (i, 0, 0), memory_space=pltpu.SMEM)
    any_spec = pl.BlockSpec(memory_space=pl.ANY)
    scratch = [pltpu.VMEM((MOE_TOP_K, OUT_TM, d), f32), pltpu.SemaphoreType.DMA((MOE_TOP_K,))]
    d0 = dest0.reshape(nb, 1, OUT_TM)
    d1 = dest1.reshape(nb, 1, OUT_TM)
    if scale is None:
        return pl.pallas_call(
            _final_kernel, grid=(nb,), in_specs=[idx, idx, any_spec, rows, per_b, vec], out_specs=rows,
            out_shape=jax.ShapeDtypeStruct((t, d), f32), scratch_shapes=scratch,
            compiler_params=_cparams(("arbitrary",)), name="final_norm")(d0, d1, yb, x, gate, norm_g)
    return pl.pallas_call(
        _combine_kernel, grid=(nb,), in_specs=[idx, idx, any_spec, rows, per_b, vec, per_b, per_b],
        out_specs=[rows, rows],
        out_shape=[jax.ShapeDtypeStruct((t, d), f32), jax.ShapeDtypeStruct((t, d), bf16)],
        scratch_shapes=scratch, compiler_params=_cparams(("arbitrary",)),
        name="combine")(d0, d1, yb, x, gate, norm_g, scale, shift)


def kernel(x, c, positions, ada_w, ada_b, norm1_g, norm2_g, final_g, w_in, rw_mu, rw_w0, rw_w2, rw_a0, rw_a2, rw_g2, rw_k_k, rw_k_a, rw_r_k, rw_ln_g, rw_ln_b, sg_ln_g, sg_ln_b, sg_w, sg_b, mla_q_norm_g, mla_w_uq, mla_kv_norm_g, mla_w_ukv, p_rwkv, p_sgu, p_mla, w_o, router_g_w, router_g_b, router_e_w, router_e_b, exp_w_gate, exp_w_up, exp_w_down):
    bsz, seq, d = x.shape
    t = bsz * seq
    half = MLA_QK_ROPE // 2
    inv_freq = ROPE_BASE ** (-jnp.arange(half, dtype=f32) / half)
    ang = (positions.astype(f32)[..., None] * inv_freq).reshape(t, half)
    cos, sin = jnp.cos(ang), jnp.sin(ang)
    zpad = jnp.zeros((t, LANES - MLA_QK_ROPE), f32)
    rope_cs = jnp.concatenate([cos, cos, zpad], axis=1)
    rope_sn = jnp.concatenate([-sin, sin, zpad], axis=1)
    c_act = jnp.pad(jax.nn.silu(c), ((0, 8 - bsz), (0, 0)))
    x = x.reshape(t, d)
    row = lambda vec: vec[None, :]

    def per_batch(vec):
        return vec[:, None, :]

    mods = []
    for l in range(DEPTH):
        mod = matmul(c_act, ada_w, tm=8, tn=1024, layer=l, name="ada")[:bsz] + ada_b[l]
        mods.append(jnp.split(mod, 6, axis=-1))

    shift1, scale1 = mods[0][0], mods[0][1]
    h = ((_rms(x, norm1_g[0]).reshape(bsz, seq, d) * (1.0 + scale1[:, None, :]) + shift1[:, None, :])
         .reshape(t, d).astype(bf16))
    for l in range(DEPTH):
        _, _, gate1, shift2, scale2, gate2 = mods[l]
        wl = w_in[l]
        w_rw = jnp.concatenate([wl[:, :3072], _pad_cols(wl[:, 3072:3168], LANES),
                                _pad_cols(wl[:, 3168:3264], LANES), wl[:, 3264:3520]], axis=1).astype(bf16)
        mul = rw_mu[l]
        mu = jnp.concatenate([mul[:3072], jnp.pad(mul[3072:3168], (0, 32)),
                              jnp.pad(mul[3168:3264], (0, 32)), mul[3264:3520]])
        w_sg = wl[:, 3520:5568].astype(bf16)
        w_at = _pad_cols(wl[:, 5568:6400], MLA_COLS_PAD).astype(bf16)
        w_gt = wl[:, 6400:].astype(bf16)
        p_rw = matmul(h, w_rw, tm=1024, tn=896, name="in_rw").reshape(bsz, seq, RW_COLS_PAD)
        p_sg = matmul(h, w_sg, tm=1024, tn=1024, out_dtype=bf16, name="in_sg")
        p_at = matmul(h, w_at, tm=1024, tn=MLA_COLS_PAD, name="in_at")
        p_gate = matmul(h, w_gt, tm=1024, tn=1024, out_dtype=bf16, name="in_gate")
        lw, a, g = rw_lora(p_rw, row(mu[3 * RW_WIDTH:]), row(rw_w0[l]), row(rw_a0[l]),
                           _pad_rows(rw_w2[l], LANES).astype(bf16), _pad_rows(rw_a2[l], LANES).astype(bf16),
                           rw_g2[l].astype(bf16))
        y_rw = rwkv_mix(p_rw, lw, a, g, row(mu[:3 * RW_WIDTH]), row(rw_k_k[l]), row(rw_k_a[l]),
                        row(rw_r_k[l].reshape(-1)), row(rw_ln_g[l]), row(rw_ln_b[l])).reshape(t, RW_WIDTH)
        bias_full = jnp.repeat(sg_b[l].T, RW_HEAD_DIM, axis=1)
        y_sg = sgu(p_sg, row(sg_ln_g[l]), row(sg_ln_b[l]), jnp.tril(sg_w[l]).astype(bf16), bias_full)
        wq = mla_w_uq[l].reshape(MLA_Q_LORA, MLA_HEADS, MLA_QK_DIM) * (MLA_QK_DIM ** -0.5 * LOG2_E)
        wq = jnp.pad(wq, ((0, 0), (0, 0), (0, ATT_QK_PAD - MLA_QK_DIM))).reshape(MLA_Q_LORA, -1).astype(bf16)
        q, kv, kr = mla_prep(p_at, row(mla_q_norm_g[l]), row(mla_kv_norm_g[l]), wq, mla_w_ukv[l].astype(bf16),
                             rope_cs, rope_sn)
        y_at = attention(q.reshape(bsz, seq, -1), kv.reshape(bsz, seq, -1),
                         kr.reshape(bsz, seq, LANES)).reshape(t, -1)
        mix = merge(y_rw, y_sg, y_at, p_gate, p_rwkv[l].astype(bf16), p_sgu[l].astype(bf16), p_mla[l].astype(bf16))
        x, h2 = out_proj(mix, w_o[l].astype(bf16), x, per_batch(gate1), row(norm2_g[l]), per_batch(scale2),
                         per_batch(shift2), seq)
        yb, dest0, dest1 = _moe(h2, l, router_g_w[l], router_g_b[l], router_e_w[l], router_e_b[l],
                                exp_w_gate, exp_w_up, exp_w_down)
        if l + 1 < DEPTH:
            x, h = combine(yb, dest0, dest1, x, per_batch(gate2), row(norm1_g[l + 1]), per_batch(mods[l + 1][1]),
                           per_batch(mods[l + 1][0]), seq)
    return combine(yb, dest0, dest1, x, per_batch(gate2), row(final_g), None, None, seq).reshape(bsz, seq, d)
```

```python
import jax
import jax.numpy as jnp
from jax import lax
from jax.experimental import pallas as pl
from jax.experimental.pallas import tpu as pltpu

f32 = jnp.float32
bf16 = jnp.bfloat16

D_MODEL = 2048
DEPTH = 4
RW_HEADS = 16
RW_HEAD_DIM = 64
RW_WIDTH = 1024
RW_LORA_COLS = 512
RW_COLS_PAD = 3 * RW_WIDTH + RW_LORA_COLS
RW_GN_EPS = 64e-5
SG_CHUNK = 128
SG_GROUPS = 16
SG_WIDTH = 1024
MLA_HEADS = 8
MLA_Q_LORA = 512
MLA_KV_LORA = 256
MLA_QK_NOPE = 128
MLA_QK_ROPE = 64
MLA_QK_DIM = 192
MLA_V_DIM = 128
MLA_COLS_PAD = 896
ROPE_BASE = 10000.0
MOE_GROUPS = 8
MOE_EPG = 8
MOE_EXPERTS = 64
MOE_TOP_K = 2
MOE_D_FF = 384
MOE_BLOCK = 256
NORM_EPS = 1e-6
LOG2_E = 1.4426950408889634

LANES = 128
RW_CHUNK = 64
VMEM_LIMIT = 48 * 1024 * 1024


def _cparams(sem):
    return pltpu.CompilerParams(dimension_semantics=sem, vmem_limit_bytes=VMEM_LIMIT)


def _dot(a, b):
    return jnp.dot(a.astype(bf16), b.astype(bf16), preferred_element_type=f32)


def _dot_nt(a, b):
    return lax.dot_general(a.astype(bf16), b.astype(bf16), (((1,), (1,)), ((), ())),
                           preferred_element_type=f32)


def _dot_tn(a, b):
    return lax.dot_general(a.astype(bf16), b.astype(bf16), (((0,), (0,)), ((), ())),
                           preferred_element_type=f32)


def _mm_kernel(a_ref, w_ref, o_ref):
    o_ref[...] = _dot(a_ref[...], w_ref[...]).astype(o_ref.dtype)


def matmul(a, w, *, tm, tn, out_dtype=f32, layer=None, name="mm"):
    m, k = a.shape
    n = w.shape[-1]
    assert m % tm == 0 and n % tn == 0, (m, tm, n, tn)
    if layer is None:
        w_spec = pl.BlockSpec((k, tn), lambda j, i: (0, j))
    else:
        w_spec = pl.BlockSpec((None, k, tn), lambda j, i: (layer, 0, j))
    return pl.pallas_call(
        _mm_kernel,
        grid=(n // tn, m // tm),
        in_specs=[pl.BlockSpec((tm, k), lambda j, i: (i, 0)), w_spec],
        out_specs=pl.BlockSpec((tm, tn), lambda j, i: (i, j)),
        out_shape=jax.ShapeDtypeStruct((m, n), out_dtype),
        compiler_params=_cparams(("parallel", "parallel")),
        name=name,
    )(a, w)


RW_LORA_TM = 1024


def _softplus(z):
    return jnp.maximum(z, 0.0) + jnp.log(1.0 + jnp.exp(-jnp.abs(z)))


def _rw_lora_kernel(p_ref, mu_ref, w0_ref, a0_ref, ww_ref, wa_ref, wg_ref, lw_ref, a_ref, g_ref, prev_ref):
    @pl.when(pl.program_id(1) == 0)
    def _():
        prev_ref[...] = jnp.zeros_like(prev_ref)

    x = p_ref[...]
    rowi = lax.broadcasted_iota(jnp.int32, x.shape, 0)
    xprev = jnp.where(rowi == 0, prev_ref[...], pltpu.roll(x, 1, 0))
    prev_ref[...] = x[RW_LORA_TM - 1:RW_LORA_TM, :]
    xs = x + mu_ref[...] * (xprev - x)
    dec = _dot(jnp.tanh(xs[:, :LANES]), ww_ref[...])
    log_w = -_softplus(-(w0_ref[...] + dec)) - 0.5
    lw_ref[...] = -jnp.exp(log_w)
    a_ref[...] = jax.nn.sigmoid(a0_ref[...] + _dot(xs[:, LANES:2 * LANES], wa_ref[...]))
    g_ref[...] = _dot(jax.nn.sigmoid(xs[:, 2 * LANES:]), wg_ref[...]).astype(g_ref.dtype)


def rw_lora(p_rw, mu_lora, w0, a0, w_w2, w_a2, w_g2):
    b, s, _ = p_rw.shape
    cb = 3 * RW_WIDTH // RW_LORA_COLS
    vec = pl.BlockSpec((1, RW_WIDTH), lambda i, j: (0, 0))
    out = pl.BlockSpec((None, RW_LORA_TM, RW_WIDTH), lambda i, j: (i, j, 0))
    return pl.pallas_call(
        _rw_lora_kernel,
        grid=(b, s // RW_LORA_TM),
        in_specs=[pl.BlockSpec((None, RW_LORA_TM, RW_LORA_COLS), lambda i, j: (i, j, cb)),
                  pl.BlockSpec((1, RW_LORA_COLS), lambda i, j: (0, 0)), vec, vec,
                  pl.BlockSpec((LANES, RW_WIDTH), lambda i, j: (0, 0)),
                  pl.BlockSpec((LANES, RW_WIDTH), lambda i, j: (0, 0)),
                  pl.BlockSpec((2 * LANES, RW_WIDTH), lambda i, j: (0, 0))],
        out_specs=[out, out, out],
        out_shape=[jax.ShapeDtypeStruct((b, s, RW_WIDTH), f32), jax.ShapeDtypeStruct((b, s, RW_WIDTH), f32),
                   jax.ShapeDtypeStruct((b, s, RW_WIDTH), bf16)],
        scratch_shapes=[pltpu.VMEM((1, RW_LORA_COLS), f32)],
        compiler_params=_cparams(("parallel", "arbitrary")),
        name="rw_lora",
    )(p_rw, mu_lora, w0, a0, w_w2, w_a2, w_g2)


def _rwkv_kernel(p_ref, lw_ref, a_ref, g_ref, mu_ref, kk_ref, ka_ref, rk_ref, lng_ref, lnb_ref,
                 o_ref, state_ref, prev_ref):
    c = RW_CHUNK
    n = 2 * c
    pairs = range(RW_HEADS // 2)

    @pl.when(pl.program_id(1) == 0)
    def _():
        state_ref[...] = jnp.zeros_like(state_ref)
        prev_ref[...] = jnp.zeros_like(prev_ref)

    row = lax.broadcasted_iota(jnp.int32, (n, n), 0)
    col = lax.broadcasted_iota(jnp.int32, (n, n), 1)
    same = (row >> 6) == (col >> 6)
    rpos = row & (c - 1)
    cpos = col & (c - 1)
    strict = same & (cpos < rpos)
    incl = same & (cpos <= rpos)
    eye = (row == col).astype(f32)
    blk16 = (row >> 4) == (col >> 4)
    blk32 = (row >> 5) == (col >> 5)
    low32 = blk32 & jnp.logical_not(blk16)
    low64 = same & jnp.logical_not(blk32)
    tr = lax.broadcasted_iota(jnp.int32, (c, c), 0)
    tc = lax.broadcasted_iota(jnp.int32, (c, c), 1)
    tril1 = (tc <= tr).astype(bf16)
    m0 = lax.broadcasted_iota(jnp.int32, (c, n), 1) < c
    row0 = lax.broadcasted_iota(jnp.int32, (c, n), 0) == 0

    def expand(y):
        return jnp.concatenate([jnp.where(m0, y, 0.0), jnp.where(m0, 0.0, y)], axis=0)

    def collapse(y):
        return y[:c] + y[c:]

    def head_sum(y):
        lo = jnp.sum(jnp.where(m0, y, 0.0), axis=-1, keepdims=True)
        hi = jnp.sum(jnp.where(m0, 0.0, y), axis=-1, keepdims=True)
        return jnp.where(m0, lo, hi)

    def shifted(base, p):
        sl = slice(base + p * LANES, base + (p + 1) * LANES)
        x = p_ref[:, sl]
        xprev = jnp.where(row0, prev_ref[:, sl], pltpu.roll(x, 1, 0))
        prev_ref[:, sl] = x[c - 1:c, :]
        return x + mu_ref[:, sl] * (xprev - x)

    sls = [slice(p * LANES, (p + 1) * LANES) for p in pairs]
    r = [shifted(0, p) for p in pairs]
    k0 = [shifted(RW_WIDTH, p) for p in pairs]
    v = [shifted(2 * RW_WIDTH, p) for p in pairs]
    a = [a_ref[:, s] for s in sls]
    lw = [lw_ref[:, s] for s in sls]
    kkr = [k0[p] * kk_ref[:, sls[p]] for p in pairs]
    kk = [kkr[p] * lax.rsqrt(jnp.maximum(head_sum(kkr[p] * kkr[p]), 1e-24)) for p in pairs]
    k = [k0[p] * (1.0 + (a[p] - 1.0) * ka_ref[:, sls[p]]) for p in pairs]

    def cumsum(x):
        x1 = x.astype(bf16)
        res = x - x1.astype(f32)
        x2 = res.astype(bf16)
        x3 = (res - x2.astype(f32)).astype(bf16)
        return (jnp.dot(tril1, x1, preferred_element_type=f32) + jnp.dot(tril1, x2, preferred_element_type=f32)
                + jnp.dot(tril1, x3, preferred_element_type=f32))

    cum = [cumsum(lw[p]) for p in pairs]
    ctot = [cum[p][c - 1:c, :] for p in pairs]
    ginv = [jnp.exp(-cum[p]) for p in pairs]
    g2 = [jnp.exp(ctot[p] - cum[p]) for p in pairs]
    b = [kk[p] * a[p] for p in pairs]
    at = [-kk[p] * jnp.exp(cum[p] - lw[p]) for p in pairs]
    rt = [r[p] * jnp.exp(cum[p]) for p in pairs]
    bt = [b[p] * ginv[p] for p in pairs]
    kt = [k[p] * ginv[p] for p in pairs]
    pm = [_dot_nt(jnp.concatenate([expand(at[p]), expand(rt[p])], axis=0),
                  jnp.concatenate([expand(bt[p]), expand(kt[p])], axis=0)) for p in pairs]
    a_ab = [jnp.where(strict, pm[p][:n, :n], 0.0) for p in pairs]
    a_ak = [jnp.where(strict, pm[p][:n, n:], 0.0).astype(bf16) for p in pairs]
    m_rb = [jnp.where(incl, pm[p][n:, :n], 0.0).astype(bf16) for p in pairs]
    m_rk = [jnp.where(incl, pm[p][n:, n:], 0.0).astype(bf16) for p in pairs]
    ev = [expand(v[p]).astype(bf16) for p in pairs]
    akv = [_dot(a_ak[p], ev[p]) for p in pairs]
    o0 = [collapse(_dot(m_rk[p], ev[p])) for p in pairs]
    kv = [jnp.where(same, _dot_tn(v[p], k[p] * g2[p]), 0.0) for p in pairs]
    a0 = [jnp.where(blk16, a_ab[p], 0.0) for p in pairs]
    a2 = [_dot(a0[p], a0[p]) for p in pairs]
    p2 = [eye + a0[p] + a2[p] + _dot(a0[p], a2[p]) for p in pairs]
    a4 = [_dot(a2[p], a2[p]) for p in pairs]
    p3 = [p2[p] + _dot(p2[p], a4[p]) for p in pairs]
    a8 = [_dot(a4[p], a4[p]) for p in pairs]
    d = [p3[p] + _dot(p3[p], a8[p]) for p in pairs]
    x1 = [_dot(jnp.where(low32, a_ab[p], 0.0), d[p]) for p in pairs]
    d = [d[p] + _dot(d[p], x1[p]) for p in pairs]
    x2 = [_dot(jnp.where(low64, a_ab[p], 0.0), d[p]) for p in pairs]
    t = [(d[p] + _dot(d[p], x2[p])).astype(bf16) for p in pairs]
    tx = [_dot(t[p], jnp.concatenate([expand(at[p]), akv[p]], axis=1)) for p in pairs]
    at_hat = [collapse(tx[p][:, :n]) for p in pairs]
    u0 = [collapse(tx[p][:, n:]) for p in pairs]
    s0 = [state_ref[p] for p in pairs]
    so = [_dot_nt(jnp.concatenate([at_hat[p], rt[p]], axis=0), s0[p]) for p in pairs]
    u = [so[p][:c] + u0[p] for p in pairs]
    su = [jnp.where(same, _dot_tn(u[p], b[p] * g2[p]), 0.0) for p in pairs]
    mu_ = [collapse(_dot(m_rb[p], expand(u[p]))) for p in pairs]
    for p in pairs:
        state_ref[p] = s0[p] * jnp.exp(ctot[p]) + kv[p] + su[p]
    inv_n = 1.0 / RW_HEAD_DIM
    for p in pairs:
        o = so[p][c:] + o0[p] + mu_[p]
        mean = head_sum(o) * inv_n
        oc = o - mean
        var = head_sum(oc * oc) * inv_n
        y = oc * lax.rsqrt(var + RW_GN_EPS) * lng_ref[:, sls[p]] + lnb_ref[:, sls[p]]
        y = y + head_sum(r[p] * k[p] * rk_ref[:, sls[p]]) * v[p]
        o_ref[:, sls[p]] = (y * g_ref[:, sls[p]].astype(f32)).astype(o_ref.dtype)


def rwkv_mix(p_rw, lw, a, g, mu, k_k, k_a, r_k, ln_g, ln_b):
    bsz, s, _ = p_rw.shape
    spec = pl.BlockSpec((None, RW_CHUNK, RW_WIDTH), lambda i, j: (i, j, 0))
    vec = pl.BlockSpec((1, RW_WIDTH), lambda i, j: (0, 0))
    return pl.pallas_call(
        _rwkv_kernel,
        grid=(bsz, s // RW_CHUNK),
        in_specs=[pl.BlockSpec((None, RW_CHUNK, 3 * RW_WIDTH), lambda i, j: (i, j, 0)), spec, spec, spec,
                  pl.BlockSpec((1, 3 * RW_WIDTH), lambda i, j: (0, 0)), vec, vec, vec, vec, vec],
        out_specs=spec,
        out_shape=jax.ShapeDtypeStruct((bsz, s, RW_WIDTH), bf16),
        scratch_shapes=[pltpu.VMEM((RW_HEADS // 2, LANES, LANES), f32), pltpu.VMEM((1, 3 * RW_WIDTH), f32)],
        compiler_params=_cparams(("parallel", "arbitrary")),
        name="rwkv_mix",
    )(p_rw, lw, a, g, mu, k_k, k_a, r_k, ln_g, ln_b)


def _sgu_kernel(p_ref, lng_ref, lnb_ref, w_ref, bias_ref, o_ref):
    x = p_ref[...].astype(f32)
    g = 0.5 * x * (1.0 + jnp.tanh(0.7978845608028654 * (x + 0.044715 * (x * x * x))))
    u = g[:, :SG_WIDTH]
    v = g[:, SG_WIDTH:]
    mu = jnp.mean(v, axis=-1, keepdims=True)
    vc = v - mu
    var = jnp.mean(vc * vc, axis=-1, keepdims=True)
    vn = (vc * lax.rsqrt(var + 1e-5) * lng_ref[...] + lnb_ref[...]).astype(bf16)
    first = lax.broadcasted_iota(jnp.int32, (SG_CHUNK, LANES), 1) < (LANES // 2)
    for q in range(SG_GROUPS // 2):
        sl = slice(q * LANES, (q + 1) * LANES)
        vq = vn[:, sl]
        lo = jnp.dot(w_ref[2 * q], vq, preferred_element_type=f32)
        hi = jnp.dot(w_ref[2 * q + 1], vq, preferred_element_type=f32)
        mixed = jnp.where(first, lo, hi) + bias_ref[:, sl]
        o_ref[:, sl] = (u[:, sl] * mixed).astype(o_ref.dtype)


def sgu(p_sg, ln_g, ln_b, w_tril, bias_full):
    t = p_sg.shape[0]
    return pl.pallas_call(
        _sgu_kernel,
        grid=(t // SG_CHUNK,),
        in_specs=[pl.BlockSpec((SG_CHUNK, 2 * SG_WIDTH), lambda i: (i, 0)),
                  pl.BlockSpec((1, SG_WIDTH), lambda i: (0, 0)),
                  pl.BlockSpec((1, SG_WIDTH), lambda i: (0, 0)),
                  pl.BlockSpec((SG_GROUPS, SG_CHUNK, SG_CHUNK), lambda i: (0, 0, 0)),
                  pl.BlockSpec((SG_CHUNK, SG_WIDTH), lambda i: (0, 0))],
        out_specs=pl.BlockSpec((SG_CHUNK, SG_WIDTH), lambda i: (i, 0)),
        out_shape=jax.ShapeDtypeStruct((t, SG_WIDTH), bf16),
        compiler_params=_cparams(("parallel",)),
        name="sgu",
    )(p_sg, ln_g, ln_b, w_tril, bias_full)


MLA_TM = 512
ATT_QK_PAD = 256


def _mla_prep_kernel(p_ref, qg_ref, kvg_ref, wq_ref, wkv_ref, cs_ref, sn_ref, q_ref, kv_ref, kr_ref):
    x = p_ref[...]

    def rms(y, g):
        return y * lax.rsqrt(jnp.mean(y * y, axis=-1, keepdims=True) + NORM_EPS) * g

    q = _dot(rms(x[:, :MLA_Q_LORA], qg_ref[...]), wq_ref[...])
    kv_ref[...] = _dot(rms(x[:, MLA_Q_LORA:MLA_Q_LORA + MLA_KV_LORA], kvg_ref[...]), wkv_ref[...]).astype(bf16)
    cs = cs_ref[...]
    sn = sn_ref[...]
    half = lax.broadcasted_iota(jnp.int32, cs.shape, 1) < (MLA_QK_ROPE // 2)

    def rope(y):
        swapped = jnp.where(half, pltpu.roll(y, LANES - MLA_QK_ROPE // 2, 1), pltpu.roll(y, MLA_QK_ROPE // 2, 1))
        return y * cs + swapped * sn

    kr_ref[...] = rope(x[:, MLA_Q_LORA + MLA_KV_LORA:]).astype(bf16)
    for h in range(MLA_HEADS):
        base = h * ATT_QK_PAD
        q_ref[:, base:base + LANES] = q[:, base:base + LANES].astype(bf16)
        q_ref[:, base + LANES:base + 2 * LANES] = rope(q[:, base + LANES:base + 2 * LANES]).astype(bf16)


def mla_prep(p_at, qg, kvg, wq, wkv, cs, sn):
    t = p_at.shape[0]
    full = lambda shape: pl.BlockSpec(shape, lambda i: (0, 0))
    return pl.pallas_call(
        _mla_prep_kernel,
        grid=(t // MLA_TM,),
        in_specs=[pl.BlockSpec((MLA_TM, MLA_COLS_PAD), lambda i: (i, 0)),
                  full((1, MLA_Q_LORA)), full((1, MLA_KV_LORA)),
                  full((MLA_Q_LORA, MLA_HEADS * ATT_QK_PAD)), full((MLA_KV_LORA, MLA_HEADS * 256)),
                  pl.BlockSpec((MLA_TM, LANES), lambda i: (i, 0)), pl.BlockSpec((MLA_TM, LANES), lambda i: (i, 0))],
        out_specs=[pl.BlockSpec((MLA_TM, MLA_HEADS * ATT_QK_PAD), lambda i: (i, 0)),
                   pl.BlockSpec((MLA_TM, MLA_HEADS * 256), lambda i: (i, 0)),
                   pl.BlockSpec((MLA_TM, LANES), lambda i: (i, 0))],
        out_shape=[jax.ShapeDtypeStruct((t, MLA_HEADS * ATT_QK_PAD), bf16),
                   jax.ShapeDtypeStruct((t, MLA_HEADS * 256), bf16),
                   jax.ShapeDtypeStruct((t, LANES), bf16)],
        compiler_params=_cparams(("parallel",)),
        name="mla_prep",
    )(p_at, qg, kvg, wq, wkv, cs, sn)


ATT_TQ = 1024
ATT_TK = 512


def _attn_kernel(q_ref, kn_ref, kr_ref, v_ref, o_ref):
    i = pl.program_id(2)
    hq = ATT_TQ // 2
    qs = [q_ref[:hq, :], q_ref[hq:, :]]

    def keys(j):
        rows = pl.ds(pl.multiple_of(j * ATT_TK, ATT_TK), ATT_TK)
        return jnp.concatenate([kn_ref[rows, :], kr_ref[rows, :]], axis=1), v_ref[rows, :]

    def update(carry, s, vj):
        m_old, l_old, acc = carry
        m_new = jnp.maximum(m_old, jnp.max(s, axis=-1, keepdims=True))
        p = jnp.exp2(s - m_new)
        alpha = jnp.exp2(m_old - m_new)
        l_new = alpha * l_old + jnp.sum(p, axis=-1, keepdims=True)
        acc = alpha * acc + jnp.dot(p.astype(bf16), vj, preferred_element_type=f32)
        return m_new, l_new, acc

    def body(j, carry):
        ka, va = keys(2 * j)
        kb, vb = keys(2 * j + 1)
        sa = [_dot_nt(qs[h], ka) for h in range(2)]
        sb = [_dot_nt(qs[h], kb) for h in range(2)]
        carry = tuple(update(carry[h], sa[h], va) for h in range(2))
        return tuple(update(carry[h], sb[h], vb) for h in range(2))

    init = tuple((jnp.full((hq, 1), -jnp.inf, f32), jnp.zeros((hq, 1), f32), jnp.zeros((hq, MLA_V_DIM), f32))
                 for _ in range(2))
    carry = lax.fori_loop(0, i, body, init)
    rowi = lax.broadcasted_iota(jnp.int32, (hq, ATT_TK), 0)
    coli = lax.broadcasted_iota(jnp.int32, (hq, ATT_TK), 1)
    diag = coli > rowi
    kj, vj = keys(2 * i)
    c0 = update(carry[0], jnp.where(diag, -jnp.inf, _dot_nt(qs[0], kj)), vj)
    c1 = update(carry[1], _dot_nt(qs[1], kj), vj)
    kj, vj = keys(2 * i + 1)
    c1 = update(c1, jnp.where(diag, -jnp.inf, _dot_nt(qs[1], kj)), vj)
    o_ref[:hq, :] = (c0[2] / c0[1]).astype(o_ref.dtype)
    o_ref[hq:, :] = (c1[2] / c1[1]).astype(o_ref.dtype)


def attention(q, kv, kr):
    b, s, _ = q.shape
    return pl.pallas_call(
        _attn_kernel,
        grid=(b, MLA_HEADS, s // ATT_TQ),
        in_specs=[pl.BlockSpec((None, ATT_TQ, ATT_QK_PAD), lambda bi, h, i: (bi, i, h)),
                  pl.BlockSpec((None, s, MLA_QK_NOPE), lambda bi, h, i: (bi, 0, 2 * h)),
                  pl.BlockSpec((None, s, LANES), lambda bi, h, i: (bi, 0, 0)),
                  pl.BlockSpec((None, s, MLA_V_DIM), lambda bi, h, i: (bi, 0, 2 * h + 1))],
        out_specs=pl.BlockSpec((None, ATT_TQ, MLA_V_DIM), lambda bi, h, i: (bi, i, h)),
        out_shape=jax.ShapeDtypeStruct((b, s, MLA_HEADS * MLA_V_DIM), bf16),
        compiler_params=_cparams(("parallel", "parallel", "parallel")),
        name="attention",
    )(q, kv, kr, kv)


OUT_TM = 512


def _merge_kernel(yr_ref, ys_ref, ya_ref, gate_ref, wr_ref, ws_ref, wa_ref, o_ref):
    d = D_MODEL
    acc = jax.nn.sigmoid(gate_ref[:, :d].astype(f32)) * jnp.dot(yr_ref[...], wr_ref[...], preferred_element_type=f32)
    acc += jax.nn.sigmoid(gate_ref[:, d:2 * d].astype(f32)) * jnp.dot(ys_ref[...], ws_ref[...],
                                                                      preferred_element_type=f32)
    acc += jax.nn.sigmoid(gate_ref[:, 2 * d:].astype(f32)) * jnp.dot(ya_ref[...], wa_ref[...],
                                                                     preferred_element_type=f32)
    o_ref[...] = acc.astype(o_ref.dtype)


def merge(y_rw, y_sg, y_at, p_gate, w_rw, w_sg, w_at):
    t = y_rw.shape[0]
    yspec = pl.BlockSpec((OUT_TM, RW_WIDTH), lambda i: (i, 0))
    wspec = pl.BlockSpec((RW_WIDTH, D_MODEL), lambda i: (0, 0), pipeline_mode=pl.Buffered(1))
    return pl.pallas_call(
        _merge_kernel,
        grid=(t // OUT_TM,),
        in_specs=[yspec, yspec, yspec, pl.BlockSpec((OUT_TM, 3 * D_MODEL), lambda i: (i, 0)), wspec, wspec, wspec],
        out_specs=pl.BlockSpec((OUT_TM, D_MODEL), lambda i: (i, 0)),
        out_shape=jax.ShapeDtypeStruct((t, D_MODEL), bf16),
        compiler_params=_cparams(("parallel",)),
        name="merge",
    )(y_rw, y_sg, y_at, p_gate, w_rw, w_sg, w_at)


def _out_kernel(mix_ref, wo_ref, x_ref, gate_ref, ng_ref, scale_ref, shift_ref, xo_ref, h_ref):
    xn = x_ref[...] + gate_ref[...] * jnp.dot(mix_ref[...], wo_ref[...], preferred_element_type=f32)
    xo_ref[...] = xn
    y = xn * lax.rsqrt(jnp.mean(xn * xn, axis=-1, keepdims=True) + NORM_EPS) * ng_ref[...]
    h_ref[...] = (y * (1.0 + scale_ref[...]) + shift_ref[...]).astype(h_ref.dtype)


def out_proj(mix, w_o, x, gate, norm_g, scale, shift, seq):
    t, d = x.shape
    per_b = pl.BlockSpec((None, 1, d), lambda i: (i * OUT_TM // seq, 0, 0))
    rows = pl.BlockSpec((OUT_TM, d), lambda i: (i, 0))
    return pl.pallas_call(
        _out_kernel,
        grid=(t // OUT_TM,),
        in_specs=[rows, pl.BlockSpec((d, d), lambda i: (0, 0), pipeline_mode=pl.Buffered(1)), rows, per_b,
                  pl.BlockSpec((1, d), lambda i: (0, 0)), per_b, per_b],
        out_specs=[rows, rows],
        out_shape=[jax.ShapeDtypeStruct((t, d), f32), jax.ShapeDtypeStruct((t, d), f32)],
        compiler_params=_cparams(("parallel",)),
        name="out_proj",
    )(mix, w_o, x, gate, norm_g, scale, shift)


def _moe_kernel(be_ref, slot_ref, nu_ref, tok_ref, h_ref, rw_ref, wg_ref, wu_ref, wd_ref, o_ref,
                wgu_s, wd_s, xbuf, sem):
    s = pl.program_id(0)
    nblk = pl.num_programs(0) - 1
    cur = jnp.minimum(s, nblk - 1)
    prv = jnp.maximum(s - 1, 0)
    compute = (s >= 1) & (s <= nu_ref[0])

    @pl.when(compute)
    def _():
        b = prv % 2
        pltpu.make_async_copy(h_ref.at[pl.ds(0, MOE_BLOCK), :], xbuf.at[b], sem.at[b]).wait()

    @pl.when(s < nu_ref[0])
    def _():
        b = s % 2

        def issue(r, carry):
            pltpu.make_async_copy(h_ref.at[pl.ds(tok_ref[0, r], 1), :], xbuf.at[b, pl.ds(r, 1), :],
                                  sem.at[b]).start()
            return carry

        lax.fori_loop(0, MOE_BLOCK, issue, 0, unroll=8)

    @pl.when((s < nu_ref[0]) & ((s == 0) | (be_ref[cur] != be_ref[prv])))
    def _():
        k = slot_ref[cur]
        wgu_s[k, :, :MOE_D_FF] = wg_ref[...].astype(bf16)
        wgu_s[k, :, MOE_D_FF:] = wu_ref[...].astype(bf16)
        wd_s[k] = wd_ref[...].astype(bf16)

    @pl.when(compute)
    def _():
        k = slot_ref[prv]
        gu = jnp.dot(xbuf[prv % 2].astype(bf16), wgu_s[k], preferred_element_type=f32)
        g = gu[:, :MOE_D_FF]
        hmid = (g * jax.nn.sigmoid(g) * gu[:, MOE_D_FF:] * rw_ref[...]).astype(bf16)
        o_ref[...] = jnp.dot(hmid, wd_s[k], preferred_element_type=f32).astype(o_ref.dtype)

    @pl.when(s > nu_ref[0])
    def _():
        o_ref[...] = jnp.zeros_like(o_ref)


def moe_ffn(block_expert, n_used, h, row_token, row_weight, w_gate, w_up, w_down, layer):
    rows = row_token.shape[0]
    d = h.shape[1]
    nblk = rows // MOE_BLOCK
    changed = jnp.concatenate([jnp.zeros((1,), jnp.int32),
                               (block_expert[1:] != block_expert[:-1]).astype(jnp.int32)])
    slot = jnp.cumsum(changed) % 2

    def cur_block(s, be, sl, nu):
        return (jnp.minimum(s, nblk - 1), 0, 0)

    def prev_rows(s, be, sl, nu):
        return (jnp.clip(s - 1, 0, nu[0] - 1), 0)

    def wmap(s, be, sl, nu):
        return (layer, be[jnp.minimum(s, nblk - 1)], 0, 0)

    return pl.pallas_call(
        _moe_kernel,
        grid_spec=pltpu.PrefetchScalarGridSpec(
            num_scalar_prefetch=3,
            grid=(nblk + 1,),
            in_specs=[pl.BlockSpec((None, 1, MOE_BLOCK), cur_block, memory_space=pltpu.SMEM),
                      pl.BlockSpec(memory_space=pl.ANY),
                      pl.BlockSpec((MOE_BLOCK, 1), prev_rows),
                      pl.BlockSpec((None, None, d, MOE_D_FF), wmap),
                      pl.BlockSpec((None, None, d, MOE_D_FF), wmap),
                      pl.BlockSpec((None, None, MOE_D_FF, d), wmap)],
            out_specs=pl.BlockSpec((MOE_BLOCK, d), lambda s, be, sl, nu: (jnp.maximum(s - 1, 0), 0)),
            scratch_shapes=[pltpu.VMEM((2, d, 2 * MOE_D_FF), bf16), pltpu.VMEM((2, MOE_D_FF, d), bf16),
                            pltpu.VMEM((2, MOE_BLOCK, d), f32), pltpu.SemaphoreType.DMA((2,))],
        ),
        out_shape=jax.ShapeDtypeStruct((rows, d), f32),
        compiler_params=_cparams(("arbitrary",)),
        name="moe_ffn",
    )(block_expert, slot.astype(jnp.int32), n_used, row_token.reshape(nblk, 1, MOE_BLOCK), h, row_weight,
      w_gate, w_up, w_down)


def _rms(x, g):
    return x * lax.rsqrt(jnp.mean(x * x, axis=-1, keepdims=True) + NORM_EPS) * g


def _pad_cols(w, width):
    return jnp.pad(w, ((0, 0), (0, width - w.shape[1])))


def _pad_rows(w, height):
    return jnp.pad(w, ((0, height - w.shape[0]), (0, 0)))


def _moe(h, layer, g_w, g_b, e_w, e_b, w_gate, w_up, w_down):
    t, d = h.shape
    w_r = _pad_cols(jnp.concatenate([g_w, e_w], axis=1), LANES)
    logits = matmul(h, w_r, tm=1024, tn=LANES, name="router")
    group_logits = logits[:, :MOE_GROUPS] + g_b
    group = jnp.argmax(group_logits, axis=-1)
    group_w = jnp.take_along_axis(jax.nn.softmax(group_logits, axis=-1), group[:, None], axis=-1)
    exp_logits = (logits[:, MOE_GROUPS:MOE_GROUPS + MOE_EXPERTS] + e_b).reshape(t, MOE_GROUPS, MOE_EPG)
    in_group = jnp.take_along_axis(exp_logits, group[:, None, None], axis=1)[:, 0]
    top_logit, top_idx = lax.top_k(in_group, MOE_TOP_K)
    weights = (group_w * jax.nn.softmax(top_logit, axis=-1)).reshape(-1)
    expert_ids = (group[:, None] * MOE_EPG + top_idx).reshape(-1).astype(jnp.int32)
    n_assign = t * MOE_TOP_K
    n_blocks = -(-n_assign // MOE_BLOCK) + MOE_EXPERTS
    rows = n_blocks * MOE_BLOCK
    onehot = (expert_ids[:, None] == jnp.arange(MOE_EXPERTS, dtype=jnp.int32)[None, :]).astype(jnp.int32)
    csum = jnp.cumsum(onehot, axis=0)
    rank = jnp.take_along_axis(csum, expert_ids[:, None], axis=1)[:, 0] - 1
    counts = csum[-1]
    padded = (counts + MOE_BLOCK - 1) // MOE_BLOCK * MOE_BLOCK
    ends = jnp.cumsum(padded)
    starts = ends - padded
    dest = starts[expert_ids] + rank
    token_ids = jnp.arange(n_assign, dtype=jnp.int32) // MOE_TOP_K
    row_token = jnp.zeros((rows,), jnp.int32).at[dest].set(token_ids)
    row_weight = jnp.zeros((rows,), f32).at[dest].set(weights)
    block_start = jnp.arange(n_blocks, dtype=jnp.int32) * MOE_BLOCK
    block_expert = jnp.minimum(jnp.searchsorted(ends, block_start, side='right'),
                               MOE_EXPERTS - 1).astype(jnp.int32)
    n_used = (ends[-1] // MOE_BLOCK).astype(jnp.int32).reshape(1)
    yb = moe_ffn(block_expert, n_used, h, row_token, row_weight[:, None], w_gate, w_up, w_down, layer)
    dest2 = dest.reshape(t, MOE_TOP_K)
    return yb, dest2[:, 0], dest2[:, 1]


def _gather_rows(d0_ref, d1_ref, yb_ref, buf, sem):
    def issue(r, carry):
        pltpu.make_async_copy(yb_ref.at[pl.ds(d0_ref[0, r], 1), :], buf.at[0, pl.ds(r, 1), :], sem.at[0]).start()
        pltpu.make_async_copy(yb_ref.at[pl.ds(d1_ref[0, r], 1), :], buf.at[1, pl.ds(r, 1), :], sem.at[1]).start()
        return carry

    lax.fori_loop(0, OUT_TM, issue, 0, unroll=8)
    for k in range(MOE_TOP_K):
        pltpu.make_async_copy(yb_ref.at[pl.ds(0, OUT_TM), :], buf.at[k], sem.at[k]).wait()
    return buf[0] + buf[1]


def _combine_kernel(d0_ref, d1_ref, yb_ref, x_ref, gate_ref, ng_ref, scale_ref, shift_ref, xo_ref, h_ref,
                    buf, sem):
    xn = x_ref[...] + gate_ref[...] * _gather_rows(d0_ref, d1_ref, yb_ref, buf, sem)
    xo_ref[...] = xn
    y = xn * lax.rsqrt(jnp.mean(xn * xn, axis=-1, keepdims=True) + NORM_EPS) * ng_ref[...]
    h_ref[...] = (y * (1.0 + scale_ref[...]) + shift_ref[...]).astype(h_ref.dtype)


def _final_kernel(d0_ref, d1_ref, yb_ref, x_ref, gate_ref, ng_ref, o_ref, buf, sem):
    xn = x_ref[...] + gate_ref[...] * _gather_rows(d0_ref, d1_ref, yb_ref, buf, sem)
    o_ref[...] = xn * lax.rsqrt(jnp.mean(xn * xn, axis=-1, keepdims=True) + NORM_EPS) * ng_ref[...]


def combine(yb, dest0, dest1, x, gate, norm_g, scale, shift, seq):
    t, d = x.shape
    nb = t // OUT_TM
    per_b = pl.BlockSpec((None, 1, d), lambda i: (i * OUT_TM // seq, 0, 0))
    rows = pl.BlockSpec((OUT_TM, d), lambda i: (i, 0))
    vec = pl.BlockSpec((1, d), lambda i: (0, 0))
    idx = pl.BlockSpec((None, 1, OUT_TM), lambda i: (i, 0, 0), memory_space=pltpu.SMEM)
    any_spec = pl.BlockSpec(memory_space=pl.ANY)
    scratch = [pltpu.VMEM((MOE_TOP_K, OUT_TM, d), f32), pltpu.SemaphoreType.DMA((MOE_TOP_K,))]
    d0 = dest0.reshape(nb, 1, OUT_TM)
    d1 = dest1.reshape(nb, 1, OUT_TM)
    if scale is None:
        return pl.pallas_call(
            _final_kernel, grid=(nb,), in_specs=[idx, idx, any_spec, rows, per_b, vec], out_specs=rows,
            out_shape=jax.ShapeDtypeStruct((t, d), f32), scratch_shapes=scratch,
            compiler_params=_cparams(("arbitrary",)), name="final_norm")(d0, d1, yb, x, gate, norm_g)
    return pl.pallas_call(
        _combine_kernel, grid=(nb,), in_specs=[idx, idx, any_spec, rows, per_b, vec, per_b, per_b],
        out_specs=[rows, rows],
        out_shape=[jax.ShapeDtypeStruct((t, d), f32), jax.ShapeDtypeStruct((t, d), bf16)],
        scratch_shapes=scratch, compiler_params=_cparams(("arbitrary",)),
        name="combine")(d0, d1, yb, x, gate, norm_g, scale, shift)


def kernel(x, c, positions, ada_w, ada_b, norm1_g, norm2_g, final_g, w_in, rw_mu, rw_w0, rw_w2, rw_a0, rw_a2, rw_g2, rw_k_k, rw_k_a, rw_r_k, rw_ln_g, rw_ln_b, sg_ln_g, sg_ln_b, sg_w, sg_b, mla_q_norm_g, mla_w_uq, mla_kv_norm_g, mla_w_ukv, p_rwkv, p_sgu, p_mla, w_o, router_g_w, router_g_b, router_e_w, router_e_b, exp_w_gate, exp_w_up, exp_w_down):
    bsz, seq, d = x.shape
    t = bsz * seq
    half = MLA_QK_ROPE // 2
    inv_freq = ROPE_BASE ** (-jnp.arange(half, dtype=f32) / half)
    ang = (positions.astype(f32)[..., None] * inv_freq).reshape(t, half)
    cos, sin = jnp.cos(ang), jnp.sin(ang)
    zpad = jnp.zeros((t, LANES - MLA_QK_ROPE), f32)
    rope_cs = jnp.concatenate([cos, cos, zpad], axis=1)
    rope_sn = jnp.concatenate([-sin, sin, zpad], axis=1)
    c_act = jnp.pad(jax.nn.silu(c), ((0, 8 - bsz), (0, 0)))
    x = x.reshape(t, d)
    row = lambda vec: vec[None, :]

    def per_batch(vec):
        return vec[:, None, :]

    mods = []
    for l in range(DEPTH):
        mod = matmul(c_act, ada_w, tm=8, tn=1024, layer=l, name="ada")[:bsz] + ada_b[l]
        mods.append(jnp.split(mod, 6, axis=-1))

    shift1, scale1 = mods[0][0], mods[0][1]
    h = ((_rms(x, norm1_g[0]).reshape(bsz, seq, d) * (1.0 + scale1[:, None, :]) + shift1[:, None, :])
         .reshape(t, d).astype(bf16))
    for l in range(DEPTH):
        _, _, gate1, shift2, scale2, gate2 = mods[l]
        wl = w_in[l]
        w_rw = jnp.concatenate([wl[:, :3072], _pad_cols(wl[:, 3072:3168], LANES),
                                _pad_cols(wl[:, 3168:3264], LANES), wl[:, 3264:3520]], axis=1).astype(bf16)
        mul = rw_mu[l]
        mu = jnp.concatenate([mul[:3072], jnp.pad(mul[3072:3168], (0, 32)),
                              jnp.pad(mul[3168:3264], (0, 32)), mul[3264:3520]])
        w_sg = wl[:, 3520:5568].astype(bf16)
        w_at = _pad_cols(wl[:, 5568:6400], MLA_COLS_PAD).astype(bf16)
        w_gt = wl[:, 6400:].astype(bf16)
        p_rw = matmul(h, w_rw, tm=1024, tn=896, name="in_rw").reshape(bsz, seq, RW_COLS_PAD)
        p_sg = matmul(h, w_sg, tm=1024, tn=1024, out_dtype=bf16, name="in_sg")
        p_at = matmul(h, w_at, tm=1024, tn=MLA_COLS_PAD, name="in_at")
        p_gate = matmul(h, w_gt, tm=1024, tn=1024, out_dtype=bf16, name="in_gate")
        lw, a, g = rw_lora(p_rw, row(mu[3 * RW_WIDTH:]), row(rw_w0[l]), row(rw_a0[l]),
                           _pad_rows(rw_w2[l], LANES).astype(bf16), _pad_rows(rw_a2[l], LANES).astype(bf16),
                           rw_g2[l].astype(bf16))
        y_rw = rwkv_mix(p_rw, lw, a, g, row(mu[:3 * RW_WIDTH]), row(rw_k_k[l]), row(rw_k_a[l]),
                        row(rw_r_k[l].reshape(-1)), row(rw_ln_g[l]), row(rw_ln_b[l])).reshape(t, RW_WIDTH)
        bias_full = jnp.repeat(sg_b[l].T, RW_HEAD_DIM, axis=1)
        y_sg = sgu(p_sg, row(sg_ln_g[l]), row(sg_ln_b[l]), jnp.tril(sg_w[l]).astype(bf16), bias_full)
        wq = mla_w_uq[l].reshape(MLA_Q_LORA, MLA_HEADS, MLA_QK_DIM) * (MLA_QK_DIM ** -0.5 * LOG2_E)
        wq = jnp.pad(wq, ((0, 0), (0, 0), (0, ATT_QK_PAD - MLA_QK_DIM))).reshape(MLA_Q_LORA, -1).astype(bf16)
        q, kv, kr = mla_prep(p_at, row(mla_q_norm_g[l]), row(mla_kv_norm_g[l]), wq, mla_w_ukv[l].astype(bf16),
                             rope_cs, rope_sn)
        y_at = attention(q.reshape(bsz, seq, -1), kv.reshape(bsz, seq, -1),
                         kr.reshape(bsz, seq, LANES)).reshape(t, -1)
        mix = merge(y_rw, y_sg, y_at, p_gate, p_rwkv[l].astype(bf16), p_sgu[l].astype(bf16), p_mla[l].astype(bf16))
        x, h2 = out_proj(mix, w_o[l].astype(bf16), x, per_batch(gate1), row(norm2_g[l]), per_batch(scale2),
                         per_batch(shift2), seq)
        yb, dest0, dest1 = _moe(h2, l, router_g_w[l], router_g_b[l], router_e_w[l], router_e_b[l],
                                exp_w_gate, exp_w_up, exp_w_down)
        if l + 1 < DEPTH:
            x, h = combine(yb, dest0, dest1, x, per_batch(gate2), row(norm1_g[l + 1]), per_batch(mods[l + 1][1]),
                           per_batch(mods[l + 1][0]), seq)
    return combine(yb, dest0, dest1, x, per_batch(gate2), row(final_g), None, None, seq).reshape(bsz, seq, d)
```

```python
import jax
import jax.numpy as jnp
from jax import lax
from jax.experimental import pallas as pl
from jax.experimental.pallas import tpu as pltpu

f32 = jnp.float32
bf16 = jnp.bfloat16

D_MODEL = 2048
DEPTH = 4
RW_HEADS = 16
RW_HEAD_DIM = 64
RW_WIDTH = 1024
RW_LORA_COLS = 512
RW_COLS_PAD = 3 * RW_WIDTH + RW_LORA_COLS
RW_GN_EPS = 64e-5
SG_CHUNK = 128
SG_GROUPS = 16
SG_WIDTH = 1024
MLA_HEADS = 8
MLA_Q_LORA = 512
MLA_KV_LORA = 256
MLA_QK_NOPE = 128
MLA_QK_ROPE = 64
MLA_QK_DIM = 192
MLA_V_DIM = 128
MLA_COLS_PAD = 896
ROPE_BASE = 10000.0
MOE_GROUPS = 8
MOE_EPG = 8
MOE_EXPERTS = 64
MOE_TOP_K = 2
MOE_D_FF = 384
MOE_BLOCK = 256
NORM_EPS = 1e-6
LOG2_E = 1.4426950408889634

LANES = 128
RW_CHUNK = 64
RW_SUB = 2
VMEM_LIMIT = 48 * 1024 * 1024


def _cparams(sem):
    return pltpu.CompilerParams(dimension_semantics=sem, vmem_limit_bytes=VMEM_LIMIT)


def _dot(a, b):
    return jnp.dot(a.astype(bf16), b.astype(bf16), preferred_element_type=f32)


def _dot_nt(a, b):
    return lax.dot_general(a.astype(bf16), b.astype(bf16), (((1,), (1,)), ((), ())),
                           preferred_element_type=f32)


def _dot_tn(a, b):
    return lax.dot_general(a.astype(bf16), b.astype(bf16), (((0,), (0,)), ((), ())),
                           preferred_element_type=f32)


def _mm_kernel(a_ref, w_ref, o_ref):
    o_ref[...] = _dot(a_ref[...], w_ref[...]).astype(o_ref.dtype)


def matmul(a, w, *, tm, tn, out_dtype=f32, layer=None, name="mm"):
    m, k = a.shape
    n = w.shape[-1]
    assert m % tm == 0 and n % tn == 0, (m, tm, n, tn)
    if layer is None:
        w_spec = pl.BlockSpec((k, tn), lambda j, i: (0, j))
    else:
        w_spec = pl.BlockSpec((None, k, tn), lambda j, i: (layer, 0, j))
    return pl.pallas_call(
        _mm_kernel,
        grid=(n // tn, m // tm),
        in_specs=[pl.BlockSpec((tm, k), lambda j, i: (i, 0)), w_spec],
        out_specs=pl.BlockSpec((tm, tn), lambda j, i: (i, j)),
        out_shape=jax.ShapeDtypeStruct((m, n), out_dtype),
        compiler_params=_cparams(("parallel", "parallel")),
        name=name,
    )(a, w)


RW_LORA_TM = 1024


def _softplus(z):
    return jnp.maximum(z, 0.0) + jnp.log(1.0 + jnp.exp(-jnp.abs(z)))


def _rw_lora_kernel(p_ref, mu_ref, w0_ref, a0_ref, ww_ref, wa_ref, wg_ref, lw_ref, a_ref, g_ref, prev_ref):
    @pl.when(pl.program_id(1) == 0)
    def _():
        prev_ref[...] = jnp.zeros_like(prev_ref)

    x = p_ref[...]
    rowi = lax.broadcasted_iota(jnp.int32, x.shape, 0)
    xprev = jnp.where(rowi == 0, prev_ref[...], pltpu.roll(x, 1, 0))
    prev_ref[...] = x[RW_LORA_TM - 1:RW_LORA_TM, :]
    xs = x + mu_ref[...] * (xprev - x)
    dec = _dot(jnp.tanh(xs[:, :LANES]), ww_ref[...])
    log_w = -_softplus(-(w0_ref[...] + dec)) - 0.5
    lw_ref[...] = -jnp.exp(log_w)
    a_ref[...] = jax.nn.sigmoid(a0_ref[...] + _dot(xs[:, LANES:2 * LANES], wa_ref[...]))
    g_ref[...] = _dot(jax.nn.sigmoid(xs[:, 2 * LANES:]), wg_ref[...]).astype(g_ref.dtype)


def rw_lora(p_rw, mu_lora, w0, a0, w_w2, w_a2, w_g2):
    b, s, _ = p_rw.shape
    cb = 3 * RW_WIDTH // RW_LORA_COLS
    vec = pl.BlockSpec((1, RW_WIDTH), lambda i, j: (0, 0))
    out = pl.BlockSpec((None, RW_LORA_TM, RW_WIDTH), lambda i, j: (i, j, 0))
    return pl.pallas_call(
        _rw_lora_kernel,
        grid=(b, s // RW_LORA_TM),
        in_specs=[pl.BlockSpec((None, RW_LORA_TM, RW_LORA_COLS), lambda i, j: (i, j, cb)),
                  pl.BlockSpec((1, RW_LORA_COLS), lambda i, j: (0, 0)), vec, vec,
                  pl.BlockSpec((LANES, RW_WIDTH), lambda i, j: (0, 0)),
                  pl.BlockSpec((LANES, RW_WIDTH), lambda i, j: (0, 0)),
                  pl.BlockSpec((2 * LANES, RW_WIDTH), lambda i, j: (0, 0))],
        out_specs=[out, out, out],
        out_shape=[jax.ShapeDtypeStruct((b, s, RW_WIDTH), f32), jax.ShapeDtypeStruct((b, s, RW_WIDTH), f32),
                   jax.ShapeDtypeStruct((b, s, RW_WIDTH), bf16)],
        scratch_shapes=[pltpu.VMEM((1, RW_LORA_COLS), f32)],
        compiler_params=_cparams(("parallel", "arbitrary")),
        name="rw_lora",
    )(p_rw, mu_lora, w0, a0, w_w2, w_a2, w_g2)


def _rwkv_kernel(p_ref, lw_ref, a_ref, g_ref, mu_ref, kk_ref, ka_ref, rk_ref, lng_ref, lnb_ref,
                 o_ref, state_ref, prev_ref):
    c = RW_CHUNK
    n = 2 * c
    nrow = RW_SUB * c
    pairs = range(RW_HEADS // 2)
    units = [(q, p) for q in range(RW_SUB) for p in pairs]

    @pl.when(pl.program_id(1) == 0)
    def _():
        state_ref[...] = jnp.zeros_like(state_ref)
        prev_ref[...] = jnp.zeros_like(prev_ref)

    row = lax.broadcasted_iota(jnp.int32, (n, n), 0)
    col = lax.broadcasted_iota(jnp.int32, (n, n), 1)
    same = (row >> 6) == (col >> 6)
    rpos = row & (c - 1)
    cpos = col & (c - 1)
    strict = same & (cpos < rpos)
    incl = same & (cpos <= rpos)
    eye = (row == col).astype(f32)
    blk16 = (row >> 4) == (col >> 4)
    blk32 = (row >> 5) == (col >> 5)
    low32 = blk32 & jnp.logical_not(blk16)
    low64 = same & jnp.logical_not(blk32)
    tr = lax.broadcasted_iota(jnp.int32, (c, c), 0)
    tc = lax.broadcasted_iota(jnp.int32, (c, c), 1)
    tril1 = (tc <= tr).astype(bf16)
    m0 = lax.broadcasted_iota(jnp.int32, (c, n), 1) < c
    m0f = lax.broadcasted_iota(jnp.int32, (nrow, n), 1) < c
    row0 = lax.broadcasted_iota(jnp.int32, (nrow, n), 0) == 0

    def expand(y):
        return jnp.concatenate([jnp.where(m0, y, 0.0), jnp.where(m0, 0.0, y)], axis=0)

    def collapse(y):
        return y[:c] + y[c:]

    def head_sum(y, mask):
        lo = jnp.sum(jnp.where(mask, y, 0.0), axis=-1, keepdims=True)
        hi = jnp.sum(jnp.where(mask, 0.0, y), axis=-1, keepdims=True)
        return jnp.where(mask, lo, hi)

    def shifted(base, p):
        sl = slice(base + p * LANES, base + (p + 1) * LANES)
        x = p_ref[:, sl]
        xprev = jnp.where(row0, prev_ref[:, sl], pltpu.roll(x, 1, 0))
        prev_ref[:, sl] = x[nrow - 1:nrow, :]
        return x + mu_ref[:, sl] * (xprev - x)

    sls = [slice(p * LANES, (p + 1) * LANES) for p in pairs]
    rf = [shifted(0, p) for p in pairs]
    k0f = [shifted(RW_WIDTH, p) for p in pairs]
    vf = [shifted(2 * RW_WIDTH, p) for p in pairs]
    af = [a_ref[:, s] for s in sls]
    kkrf = [k0f[p] * kk_ref[:, sls[p]] for p in pairs]
    kkf = [kkrf[p] * lax.rsqrt(jnp.maximum(head_sum(kkrf[p] * kkrf[p], m0f), 1e-24)) for p in pairs]
    kf = [k0f[p] * (1.0 + (af[p] - 1.0) * ka_ref[:, sls[p]]) for p in pairs]
    bonus = [head_sum(rf[p] * kf[p] * rk_ref[:, sls[p]], m0f) * vf[p] for p in pairs]

    def chunk(xs, q, p):
        return xs[p][q * c:(q + 1) * c, :]

    r = [chunk(rf, q, p) for q, p in units]
    v = [chunk(vf, q, p) for q, p in units]
    a = [chunk(af, q, p) for q, p in units]
    kk = [chunk(kkf, q, p) for q, p in units]
    k = [chunk(kf, q, p) for q, p in units]
    lw = [lw_ref[q * c:(q + 1) * c, sls[p]] for q, p in units]
    pairs = range(len(units))

    def cumsum(x):
        x1 = x.astype(bf16)
        res = x - x1.astype(f32)
        x2 = res.astype(bf16)
        x3 = (res - x2.astype(f32)).astype(bf16)
        return (jnp.dot(tril1, x1, preferred_element_type=f32) + jnp.dot(tril1, x2, preferred_element_type=f32)
                + jnp.dot(tril1, x3, preferred_element_type=f32))

    cum = [cumsum(lw[p]) for p in pairs]
    ctot = [cum[p][c - 1:c, :] for p in pairs]
    ginv = [jnp.exp(-cum[p]) for p in pairs]
    g2 = [jnp.exp(ctot[p] - cum[p]) for p in pairs]
    b = [kk[p] * a[p] for p in pairs]
    at = [-kk[p] * jnp.exp(cum[p] - lw[p]) for p in pairs]
    rt = [r[p] * jnp.exp(cum[p]) for p in pairs]
    bt = [b[p] * ginv[p] for p in pairs]
    kt = [k[p] * ginv[p] for p in pairs]
    pm = [_dot_nt(jnp.concatenate([expand(at[p]), expand(rt[p])], axis=0),
                  jnp.concatenate([expand(bt[p]), expand(kt[p])], axis=0)) for p in pairs]
    a_ab = [jnp.where(strict, pm[p][:n, :n], 0.0) for p in pairs]
    a_ak = [jnp.where(strict, pm[p][:n, n:], 0.0).astype(bf16) for p in pairs]
    m_rb = [jnp.where(incl, pm[p][n:, :n], 0.0).astype(bf16) for p in pairs]
    m_rk = [jnp.where(incl, pm[p][n:, n:], 0.0).astype(bf16) for p in pairs]
    ev = [expand(v[p]).astype(bf16) for p in pairs]
    akv = [_dot(a_ak[p], ev[p]) for p in pairs]
    o0 = [collapse(_dot(m_rk[p], ev[p])) for p in pairs]
    kv = [jnp.where(same, _dot_tn(v[p], k[p] * g2[p]), 0.0) for p in pairs]
    a0 = [jnp.where(blk16, a_ab[p], 0.0) for p in pairs]
    a2 = [_dot(a0[p], a0[p]) for p in pairs]
    p2 = [eye + a0[p] + a2[p] + _dot(a0[p], a2[p]) for p in pairs]
    a4 = [_dot(a2[p], a2[p]) for p in pairs]
    p3 = [p2[p] + _dot(p2[p], a4[p]) for p in pairs]
    a8 = [_dot(a4[p], a4[p]) for p in pairs]
    d = [p3[p] + _dot(p3[p], a8[p]) for p in pairs]
    x1 = [_dot(jnp.where(low32, a_ab[p], 0.0), d[p]) for p in pairs]
    d = [d[p] + _dot(d[p], x1[p]) for p in pairs]
    x2 = [_dot(jnp.where(low64, a_ab[p], 0.0), d[p]) for p in pairs]
    t = [(d[p] + _dot(d[p], x2[p])).astype(bf16) for p in pairs]
    tx = [_dot(t[p], jnp.concatenate([expand(at[p]), akv[p]], axis=1)) for p in pairs]
    at_hat = [collapse(tx[p][:, :n]) for p in pairs]
    u0 = [collapse(tx[p][:, n:]) for p in pairs]
    inv_n = 1.0 / RW_HEAD_DIM
    npair = RW_HEADS // 2
    state = [state_ref[h] for h in range(npair)]
    for q in range(RW_SUB):
        ids = [q * npair + h for h in range(npair)]
        so = [_dot_nt(jnp.concatenate([at_hat[i], rt[i]], axis=0), state[h]) for h, i in enumerate(ids)]
        u = [so[h][:c] + u0[i] for h, i in enumerate(ids)]
        su = [jnp.where(same, _dot_tn(u[h], b[i] * g2[i]), 0.0) for h, i in enumerate(ids)]
        mu_ = [collapse(_dot(m_rb[i], expand(u[h]))) for h, i in enumerate(ids)]
        state = [state[h] * jnp.exp(ctot[i]) + kv[i] + su[h] for h, i in enumerate(ids)]
        rows = slice(q * c, (q + 1) * c)
        for h, i in enumerate(ids):
            o = so[h][c:] + o0[i] + mu_[h]
            mean = head_sum(o, m0) * inv_n
            oc = o - mean
            var = head_sum(oc * oc, m0) * inv_n
            y = oc * lax.rsqrt(var + RW_GN_EPS) * lng_ref[:, sls[h]] + lnb_ref[:, sls[h]]
            y = y + bonus[h][rows, :]
            o_ref[rows, sls[h]] = (y * g_ref[rows, sls[h]].astype(f32)).astype(o_ref.dtype)
    for h in range(npair):
        state_ref[h] = state[h]


def rwkv_mix(p_rw, lw, a, g, mu, k_k, k_a, r_k, ln_g, ln_b):
    bsz, s, _ = p_rw.shape
    nrow = RW_SUB * RW_CHUNK
    spec = pl.BlockSpec((None, nrow, RW_WIDTH), lambda i, j: (i, j, 0))
    vec = pl.BlockSpec((1, RW_WIDTH), lambda i, j: (0, 0))
    return pl.pallas_call(
        _rwkv_kernel,
        grid=(bsz, s // nrow),
        in_specs=[pl.BlockSpec((None, nrow, 3 * RW_WIDTH), lambda i, j: (i, j, 0)), spec, spec, spec,
                  pl.BlockSpec((1, 3 * RW_WIDTH), lambda i, j: (0, 0)), vec, vec, vec, vec, vec],
        out_specs=spec,
        out_shape=jax.ShapeDtypeStruct((bsz, s, RW_WIDTH), bf16),
        scratch_shapes=[pltpu.VMEM((RW_HEADS // 2, LANES, LANES), f32), pltpu.VMEM((1, 3 * RW_WIDTH), f32)],
        compiler_params=_cparams(("parallel", "arbitrary")),
        name="rwkv_mix",
    )(p_rw, lw, a, g, mu, k_k, k_a, r_k, ln_g, ln_b)


def _sgu_kernel(p_ref, lng_ref, lnb_ref, w_ref, bias_ref, o_ref):
    x = p_ref[...].astype(f32)
    g = 0.5 * x * (1.0 + jnp.tanh(0.7978845608028654 * (x + 0.044715 * (x * x * x))))
    u = g[:, :SG_WIDTH]
    v = g[:, SG_WIDTH:]
    mu = jnp.mean(v, axis=-1, keepdims=True)
    vc = v - mu
    var = jnp.mean(vc * vc, axis=-1, keepdims=True)
    vn = (vc * lax.rsqrt(var + 1e-5) * lng_ref[...] + lnb_ref[...]).astype(bf16)
    first = lax.broadcasted_iota(jnp.int32, (SG_CHUNK, LANES), 1) < (LANES // 2)
    for q in range(SG_GROUPS // 2):
        sl = slice(q * LANES, (q + 1) * LANES)
        vq = vn[:, sl]
        lo = jnp.dot(w_ref[2 * q], vq, preferred_element_type=f32)
        hi = jnp.dot(w_ref[2 * q + 1], vq, preferred_element_type=f32)
        mixed = jnp.where(first, lo, hi) + bias_ref[:, sl]
        o_ref[:, sl] = (u[:, sl] * mixed).astype(o_ref.dtype)


def sgu(p_sg, ln_g, ln_b, w_tril, bias_full):
    t = p_sg.shape[0]
    return pl.pallas_call(
        _sgu_kernel,
        grid=(t // SG_CHUNK,),
        in_specs=[pl.BlockSpec((SG_CHUNK, 2 * SG_WIDTH), lambda i: (i, 0)),
                  pl.BlockSpec((1, SG_WIDTH), lambda i: (0, 0)),
                  pl.BlockSpec((1, SG_WIDTH), lambda i: (0, 0)),
                  pl.BlockSpec((SG_GROUPS, SG_CHUNK, SG_CHUNK), lambda i: (0, 0, 0)),
                  pl.BlockSpec((SG_CHUNK, SG_WIDTH), lambda i: (0, 0))],
        out_specs=pl.BlockSpec((SG_CHUNK, SG_WIDTH), lambda i: (i, 0)),
        out_shape=jax.ShapeDtypeStruct((t, SG_WIDTH), bf16),
        compiler_params=_cparams(("parallel",)),
        name="sgu",
    )(p_sg, ln_g, ln_b, w_tril, bias_full)


MLA_TM = 512
ATT_QK_PAD = 256


def _mla_prep_kernel(p_ref, qg_ref, kvg_ref, wq_ref, wkv_ref, cs_ref, sn_ref, q_ref, kv_ref, kr_ref):
    x = p_ref[...]

    def rms(y, g):
        return y * lax.rsqrt(jnp.mean(y * y, axis=-1, keepdims=True) + NORM_EPS) * g

    q = _dot(rms(x[:, :MLA_Q_LORA], qg_ref[...]), wq_ref[...])
    kv_ref[...] = _dot(rms(x[:, MLA_Q_LORA:MLA_Q_LORA + MLA_KV_LORA], kvg_ref[...]), wkv_ref[...]).astype(bf16)
    cs = cs_ref[...]
    sn = sn_ref[...]
    half = lax.broadcasted_iota(jnp.int32, cs.shape, 1) < (MLA_QK_ROPE // 2)

    def rope(y):
        swapped = jnp.where(half, pltpu.roll(y, LANES - MLA_QK_ROPE // 2, 1), pltpu.roll(y, MLA_QK_ROPE // 2, 1))
        return y * cs + swapped * sn

    kr_ref[...] = rope(x[:, MLA_Q_LORA + MLA_KV_LORA:]).astype(bf16)
    for h in range(MLA_HEADS):
        base = h * ATT_QK_PAD
        q_ref[:, base:base + LANES] = q[:, base:base + LANES].astype(bf16)
        q_ref[:, base + LANES:base + 2 * LANES] = rope(q[:, base + LANES:base + 2 * LANES]).astype(bf16)


def mla_prep(p_at, qg, kvg, wq, wkv, cs, sn):
    t = p_at.shape[0]
    full = lambda shape: pl.BlockSpec(shape, lambda i: (0, 0))
    return pl.pallas_call(
        _mla_prep_kernel,
        grid=(t // MLA_TM,),
        in_specs=[pl.BlockSpec((MLA_TM, MLA_COLS_PAD), lambda i: (i, 0)),
                  full((1, MLA_Q_LORA)), full((1, MLA_KV_LORA)),
                  full((MLA_Q_LORA, MLA_HEADS * ATT_QK_PAD)), full((MLA_KV_LORA, MLA_HEADS * 256)),
                  pl.BlockSpec((MLA_TM, LANES), lambda i: (i, 0)), pl.BlockSpec((MLA_TM, LANES), lambda i: (i, 0))],
        out_specs=[pl.BlockSpec((MLA_TM, MLA_HEADS * ATT_QK_PAD), lambda i: (i, 0)),
                   pl.BlockSpec((MLA_TM, MLA_HEADS * 256), lambda i: (i, 0)),
                   pl.BlockSpec((MLA_TM, LANES), lambda i: (i, 0))],
        out_shape=[jax.ShapeDtypeStruct((t, MLA_HEADS * ATT_QK_PAD), bf16),
                   jax.ShapeDtypeStruct((t, MLA_HEADS * 256), bf16),
                   jax.ShapeDtypeStruct((t, LANES), bf16)],
        compiler_params=_cparams(("parallel",)),
        name="mla_prep",
    )(p_at, qg, kvg, wq, wkv, cs, sn)


ATT_TQ = 1024
ATT_TK = 512


def _attn_kernel(q_ref, kn_ref, kr_ref, v_ref, o_ref):
    i = pl.program_id(2)
    hq = ATT_TQ // 2
    qs = [q_ref[:hq, :], q_ref[hq:, :]]

    def keys(j):
        rows = pl.ds(pl.multiple_of(j * ATT_TK, ATT_TK), ATT_TK)
        return jnp.concatenate([kn_ref[rows, :], kr_ref[rows, :]], axis=1), v_ref[rows, :]

    def update(carry, s, vj):
        m_old, l_old, acc = carry
        m_new = jnp.maximum(m_old, jnp.max(s, axis=-1, keepdims=True))
        p = jnp.exp2(s - m_new)
        alpha = jnp.exp2(m_old - m_new)
        l_new = alpha * l_old + jnp.sum(p, axis=-1, keepdims=True)
        acc = alpha * acc + jnp.dot(p.astype(bf16), vj, preferred_element_type=f32)
        return m_new, l_new, acc

    def body(j, carry):
        ka, va = keys(2 * j)
        kb, vb = keys(2 * j + 1)
        sa = [_dot_nt(qs[h], ka) for h in range(2)]
        sb = [_dot_nt(qs[h], kb) for h in range(2)]
        carry = tuple(update(carry[h], sa[h], va) for h in range(2))
        return tuple(update(carry[h], sb[h], vb) for h in range(2))

    init = tuple((jnp.full((hq, 1), -jnp.inf, f32), jnp.zeros((hq, 1), f32), jnp.zeros((hq, MLA_V_DIM), f32))
                 for _ in range(2))
    carry = lax.fori_loop(0, i, body, init)
    rowi = lax.broadcasted_iota(jnp.int32, (hq, ATT_TK), 0)
    coli = lax.broadcasted_iota(jnp.int32, (hq, ATT_TK), 1)
    diag = coli > rowi
    kj, vj = keys(2 * i)
    c0 = update(carry[0], jnp.where(diag, -jnp.inf, _dot_nt(qs[0], kj)), vj)
    c1 = update(carry[1], _dot_nt(qs[1], kj), vj)
    kj, vj = keys(2 * i + 1)
    c1 = update(c1, jnp.where(diag, -jnp.inf, _dot_nt(qs[1], kj)), vj)
    o_ref[:hq, :] = (c0[2] / c0[1]).astype(o_ref.dtype)
    o_ref[hq:, :] = (c1[2] / c1[1]).astype(o_ref.dtype)


def attention(q, kv, kr):
    b, s, _ = q.shape
    return pl.pallas_call(
        _attn_kernel,
        grid=(b, MLA_HEADS, s // ATT_TQ),
        in_specs=[pl.BlockSpec((None, ATT_TQ, ATT_QK_PAD), lambda bi, h, i: (bi, i, h)),
                  pl.BlockSpec((None, s, MLA_QK_NOPE), lambda bi, h, i: (bi, 0, 2 * h)),
                  pl.BlockSpec((None, s, LANES), lambda bi, h, i: (bi, 0, 0)),
                  pl.BlockSpec((None, s, MLA_V_DIM), lambda bi, h, i: (bi, 0, 2 * h + 1))],
        out_specs=pl.BlockSpec((None, ATT_TQ, MLA_V_DIM), lambda bi, h, i: (bi, i, h)),
        out_shape=jax.ShapeDtypeStruct((b, s, MLA_HEADS * MLA_V_DIM), bf16),
        compiler_params=_cparams(("parallel", "parallel", "parallel")),
        name="attention",
    )(q, kv, kr, kv)


OUT_TM = 512


def _merge_kernel(yr_ref, ys_ref, ya_ref, gate_ref, wr_ref, ws_ref, wa_ref, o_ref):
    d = D_MODEL
    acc = jax.nn.sigmoid(gate_ref[:, :d].astype(f32)) * jnp.dot(yr_ref[...], wr_ref[...], preferred_element_type=f32)
    acc += jax.nn.sigmoid(gate_ref[:, d:2 * d].astype(f32)) * jnp.dot(ys_ref[...], ws_ref[...],
                                                                      preferred_element_type=f32)
    acc += jax.nn.sigmoid(gate_ref[:, 2 * d:].astype(f32)) * jnp.dot(ya_ref[...], wa_ref[...],
                                                                     preferred_element_type=f32)
    o_ref[...] = acc.astype(o_ref.dtype)


def merge(y_rw, y_sg, y_at, p_gate, w_rw, w_sg, w_at):
    t = y_rw.shape[0]
    yspec = pl.BlockSpec((OUT_TM, RW_WIDTH), lambda i: (i, 0))
    wspec = pl.BlockSpec((RW_WIDTH, D_MODEL), lambda i: (0, 0), pipeline_mode=pl.Buffered(1))
    return pl.pallas_call(
        _merge_kernel,
        grid=(t // OUT_TM,),
        in_specs=[yspec, yspec, yspec, pl.BlockSpec((OUT_TM, 3 * D_MODEL), lambda i: (i, 0)), wspec, wspec, wspec],
        out_specs=pl.BlockSpec((OUT_TM, D_MODEL), lambda i: (i, 0)),
        out_shape=jax.ShapeDtypeStruct((t, D_MODEL), bf16),
        compiler_params=_cparams(("parallel",)),
        name="merge",
    )(y_rw, y_sg, y_at, p_gate, w_rw, w_sg, w_at)


def _out_kernel(mix_ref, wo_ref, x_ref, gate_ref, ng_ref, scale_ref, shift_ref, xo_ref, h_ref):
    xn = x_ref[...] + gate_ref[...] * jnp.dot(mix_ref[...], wo_ref[...], preferred_element_type=f32)
    xo_ref[...] = xn
    y = xn * lax.rsqrt(jnp.mean(xn * xn, axis=-1, keepdims=True) + NORM_EPS) * ng_ref[...]
    h_ref[...] = (y * (1.0 + scale_ref[...]) + shift_ref[...]).astype(h_ref.dtype)


def out_proj(mix, w_o, x, gate, norm_g, scale, shift, seq):
    t, d = x.shape
    per_b = pl.BlockSpec((None, 1, d), lambda i: (i * OUT_TM // seq, 0, 0))
    rows = pl.BlockSpec((OUT_TM, d), lambda i: (i, 0))
    return pl.pallas_call(
        _out_kernel,
        grid=(t // OUT_TM,),
        in_specs=[rows, pl.BlockSpec((d, d), lambda i: (0, 0), pipeline_mode=pl.Buffered(1)), rows, per_b,
                  pl.BlockSpec((1, d), lambda i: (0, 0)), per_b, per_b],
        out_specs=[rows, rows],
        out_shape=[jax.ShapeDtypeStruct((t, d), f32), jax.ShapeDtypeStruct((t, d), f32)],
        compiler_params=_cparams(("parallel",)),
        name="out_proj",
    )(mix, w_o, x, gate, norm_g, scale, shift)


def _moe_kernel(be_ref, slot_ref, nu_ref, tok_ref, h_ref, rw_ref, wg_ref, wu_ref, wd_ref, o_ref,
                wgu_s, wd_s, xbuf, sem):
    s = pl.program_id(0)
    nblk = pl.num_programs(0) - 1
    cur = jnp.minimum(s, nblk - 1)
    prv = jnp.maximum(s - 1, 0)
    compute = (s >= 1) & (s <= nu_ref[0])

    @pl.when(compute)
    def _():
        b = prv % 2
        pltpu.make_async_copy(h_ref.at[pl.ds(0, MOE_BLOCK), :], xbuf.at[b], sem.at[b]).wait()

    @pl.when(s < nu_ref[0])
    def _():
        b = s % 2

        def issue(r, carry):
            pltpu.make_async_copy(h_ref.at[pl.ds(tok_ref[0, r], 1), :], xbuf.at[b, pl.ds(r, 1), :],
                                  sem.at[b]).start(priority=1)
            return carry

        lax.fori_loop(0, MOE_BLOCK, issue, 0, unroll=8)

    @pl.when((s < nu_ref[0]) & ((s == 0) | (be_ref[cur] != be_ref[prv])))
    def _():
        k = slot_ref[cur]
        wgu_s[k, :, :MOE_D_FF] = wg_ref[...].astype(bf16)
        wgu_s[k, :, MOE_D_FF:] = wu_ref[...].astype(bf16)
        wd_s[k] = wd_ref[...].astype(bf16)

    @pl.when(compute)
    def _():
        k = slot_ref[prv]
        gu = jnp.dot(xbuf[prv % 2].astype(bf16), wgu_s[k], preferred_element_type=f32)
        g = gu[:, :MOE_D_FF]
        hmid = (g * jax.nn.sigmoid(g) * gu[:, MOE_D_FF:] * rw_ref[...]).astype(bf16)
        o_ref[...] = jnp.dot(hmid, wd_s[k], preferred_element_type=f32).astype(o_ref.dtype)

    @pl.when(s > nu_ref[0])
    def _():
        o_ref[...] = jnp.zeros_like(o_ref)


def moe_ffn(block_expert, n_used, h, row_token, row_weight, w_gate, w_up, w_down, layer):
    rows = row_token.shape[0]
    d = h.shape[1]
    nblk = rows // MOE_BLOCK
    changed = jnp.concatenate([jnp.zeros((1,), jnp.int32),
                               (block_expert[1:] != block_expert[:-1]).astype(jnp.int32)])
    slot = jnp.cumsum(changed) % 2

    def cur_block(s, be, sl, nu):
        return (jnp.minimum(s, nblk - 1), 0, 0)

    def prev_rows(s, be, sl, nu):
        return (jnp.clip(s - 1, 0, nu[0] - 1), 0)

    def wmap(s, be, sl, nu):
        return (layer, be[jnp.minimum(s, nblk - 1)], 0, 0)

    return pl.pallas_call(
        _moe_kernel,
        grid_spec=pltpu.PrefetchScalarGridSpec(
            num_scalar_prefetch=3,
            grid=(nblk + 1,),
            in_specs=[pl.BlockSpec((None, 1, MOE_BLOCK), cur_block, memory_space=pltpu.SMEM),
                      pl.BlockSpec(memory_space=pl.ANY),
                      pl.BlockSpec((MOE_BLOCK, 1), prev_rows),
                      pl.BlockSpec((None, None, d, MOE_D_FF), wmap),
                      pl.BlockSpec((None, None, d, MOE_D_FF), wmap),
                      pl.BlockSpec((None, None, MOE_D_FF, d), wmap)],
            out_specs=pl.BlockSpec((MOE_BLOCK, d), lambda s, be, sl, nu: (jnp.maximum(s - 1, 0), 0)),
            scratch_shapes=[pltpu.VMEM((2, d, 2 * MOE_D_FF), bf16), pltpu.VMEM((2, MOE_D_FF, d), bf16),
                            pltpu.VMEM((2, MOE_BLOCK, d), f32), pltpu.SemaphoreType.DMA((2,))],
        ),
        out_shape=jax.ShapeDtypeStruct((rows, d), f32),
        compiler_params=_cparams(("arbitrary",)),
        name="moe_ffn",
    )(block_expert, slot.astype(jnp.int32), n_used, row_token.reshape(nblk, 1, MOE_BLOCK), h, row_weight,
      w_gate, w_up, w_down)


def _rms(x, g):
    return x * lax.rsqrt(jnp.mean(x * x, axis=-1, keepdims=True) + NORM_EPS) * g


def _pad_cols(w, width):
    return jnp.pad(w, ((0, 0), (0, width - w.shape[1])))


def _pad_rows(w, height):
    return jnp.pad(w, ((0, height - w.shape[0]), (0, 0)))


def _moe(h, layer, g_w, g_b, e_w, e_b, w_gate, w_up, w_down):
    t, d = h.shape
    w_r = _pad_cols(jnp.concatenate([g_w, e_w], axis=1), LANES)
    logits = matmul(h, w_r, tm=1024, tn=LANES, name="router")
    group_logits = logits[:, :MOE_GROUPS] + g_b
    group = jnp.argmax(group_logits, axis=-1)
    group_w = jnp.take_along_axis(jax.nn.softmax(group_logits, axis=-1), group[:, None], axis=-1)
    exp_logits = (logits[:, MOE_GROUPS:MOE_GROUPS + MOE_EXPERTS] + e_b).reshape(t, MOE_GROUPS, MOE_EPG)
    in_group = jnp.take_along_axis(exp_logits, group[:, None, None], axis=1)[:, 0]
    top_logit, top_idx = lax.top_k(in_group, MOE_TOP_K)
    weights = (group_w * jax.nn.softmax(top_logit, axis=-1)).reshape(-1)
    expert_ids = (group[:, None] * MOE_EPG + top_idx).reshape(-1).astype(jnp.int32)
    n_assign = t * MOE_TOP_K
    n_blocks = -(-n_assign // MOE_BLOCK) + MOE_EXPERTS
    rows = n_blocks * MOE_BLOCK
    onehot = (expert_ids[:, None] == jnp.arange(MOE_EXPERTS, dtype=jnp.int32)[None, :]).astype(jnp.int32)
    csum = jnp.cumsum(onehot, axis=0)
    rank = jnp.take_along_axis(csum, expert_ids[:, None], axis=1)[:, 0] - 1
    counts = csum[-1]
    padded = (counts + MOE_BLOCK - 1) // MOE_BLOCK * MOE_BLOCK
    ends = jnp.cumsum(padded)
    starts = ends - padded
    dest = starts[expert_ids] + rank
    token_ids = jnp.arange(n_assign, dtype=jnp.int32) // MOE_TOP_K
    row_token = jnp.zeros((rows,), jnp.int32).at[dest].set(token_ids)
    row_weight = jnp.zeros((rows,), f32).at[dest].set(weights)
    block_start = jnp.arange(n_blocks, dtype=jnp.int32) * MOE_BLOCK
    block_expert = jnp.minimum(jnp.searchsorted(ends, block_start, side='right'),
                               MOE_EXPERTS - 1).astype(jnp.int32)
    n_used = (ends[-1] // MOE_BLOCK).astype(jnp.int32).reshape(1)
    yb = moe_ffn(block_expert, n_used, h, row_token, row_weight[:, None], w_gate, w_up, w_down, layer)
    dest2 = dest.reshape(t, MOE_TOP_K)
    return yb, dest2[:, 0], dest2[:, 1]


def _gather_rows(d0_ref, d1_ref, yb_ref, buf, sem):
    def issue(r, carry):
        pltpu.make_async_copy(yb_ref.at[pl.ds(d0_ref[0, r], 1), :], buf.at[0, pl.ds(r, 1), :], sem.at[0]).start()
        pltpu.make_async_copy(yb_ref.at[pl.ds(d1_ref[0, r], 1), :], buf.at[1, pl.ds(r, 1), :], sem.at[1]).start()
        return carry

    lax.fori_loop(0, OUT_TM, issue, 0, unroll=8)
    for k in range(MOE_TOP_K):
        pltpu.make_async_copy(yb_ref.at[pl.ds(0, OUT_TM), :], buf.at[k], sem.at[k]).wait()
    return buf[0] + buf[1]


def _combine_kernel(d0_ref, d1_ref, yb_ref, x_ref, gate_ref, ng_ref, scale_ref, shift_ref, xo_ref, h_ref,
                    buf, sem):
    xn = x_ref[...] + gate_ref[...] * _gather_rows(d0_ref, d1_ref, yb_ref, buf, sem)
    xo_ref[...] = xn
    y = xn * lax.rsqrt(jnp.mean(xn * xn, axis=-1, keepdims=True) + NORM_EPS) * ng_ref[...]
    h_ref[...] = (y * (1.0 + scale_ref[...]) + shift_ref[...]).astype(h_ref.dtype)


def _final_kernel(d0_ref, d1_ref, yb_ref, x_ref, gate_ref, ng_ref, o_ref, buf, sem):
    xn = x_ref[...] + gate_ref[...] * _gather_rows(d0_ref, d1_ref, yb_ref, buf, sem)
    o_ref[...] = xn * lax.rsqrt(jnp.mean(xn * xn, axis=-1, keepdims=True) + NORM_EPS) * ng_ref[...]


def combine(yb, dest0, dest1, x, gate, norm_g, scale, shift, seq):
    t, d = x.shape
    nb = t // OUT_TM
    per_b = pl.BlockSpec((None, 1, d), lambda i: (i * OUT_TM // seq, 0, 0))
    rows = pl.BlockSpec((OUT_TM, d), lambda i: (i, 0))
    vec = pl.BlockSpec((1, d), lambda i: (0, 0))
    idx = pl.BlockSpec((None, 1, OUT_TM), lambda i: (i, 0, 0), memory_space=pltpu.SMEM)
    any_spec = pl.BlockSpec(memory_space=pl.ANY)
    scratch = [pltpu.VMEM((MOE_TOP_K, OUT_TM, d), f32), pltpu.SemaphoreType.DMA((MOE_TOP_K,))]
    d0 = dest0.reshape(nb, 1, OUT_TM)
    d1 = dest1.reshape(nb, 1, OUT_TM)
    if scale is None:
        return pl.pallas_call(
            _final_kernel, grid=(nb,), in_specs=[idx, idx, any_spec, rows, per_b, vec], out_specs=rows,
            out_shape=jax.ShapeDtypeStruct((t, d), f32), scratch_shapes=scratch,
            compiler_params=_cparams(("arbitrary",)), name="final_norm")(d0, d1, yb, x, gate, norm_g)
    return pl.pallas_call(
        _combine_kernel, grid=(nb,), in_specs=[idx, idx, any_spec, rows, per_b, vec, per_b, per_b],
        out_specs=[rows, rows],
        out_shape=[jax.ShapeDtypeStruct((t, d), f32), jax.ShapeDtypeStruct((t, d), bf16)],
        scratch_shapes=scratch, compiler_params=_cparams(("arbitrary",)),
        name="combine")(d0, d1, yb, x, gate, norm_g, scale, shift)


def kernel(x, c, positions, ada_w, ada_b, norm1_g, norm2_g, final_g, w_in, rw_mu, rw_w0, rw_w2, rw_a0, rw_a2, rw_g2, rw_k_k, rw_k_a, rw_r_k, rw_ln_g, rw_ln_b, sg_ln_g, sg_ln_b, sg_w, sg_b, mla_q_norm_g, mla_w_uq, mla_kv_norm_g, mla_w_ukv, p_rwkv, p_sgu, p_mla, w_o, router_g_w, router_g_b, router_e_w, router_e_b, exp_w_gate, exp_w_up, exp_w_down):
    bsz, seq, d = x.shape
    t = bsz * seq
    half = MLA_QK_ROPE // 2
    inv_freq = ROPE_BASE ** (-jnp.arange(half, dtype=f32) / half)
    ang = (positions.astype(f32)[..., None] * inv_freq).reshape(t, half)
    cos, sin = jnp.cos(ang), jnp.sin(ang)
    zpad = jnp.zeros((t, LANES - MLA_QK_ROPE), f32)
    rope_cs = jnp.concatenate([cos, cos, zpad], axis=1)
    rope_sn = jnp.concatenate([-sin, sin, zpad], axis=1)
    c_act = jnp.pad(jax.nn.silu(c), ((0, 8 - bsz), (0, 0)))
    x = x.reshape(t, d)
    row = lambda vec: vec[None, :]

    def per_batch(vec):
        return vec[:, None, :]

    mods = []
    for l in range(DEPTH):
        mod = matmul(c_act, ada_w, tm=8, tn=1024, layer=l, name="ada")[:bsz] + ada_b[l]
        mods.append(jnp.split(mod, 6, axis=-1))

    shift1, scale1 = mods[0][0], mods[0][1]
    h = ((_rms(x, norm1_g[0]).reshape(bsz, seq, d) * (1.0 + scale1[:, None, :]) + shift1[:, None, :])
         .reshape(t, d).astype(bf16))
    for l in range(DEPTH):
        _, _, gate1, shift2, scale2, gate2 = mods[l]
        wl = w_in[l]
        w_rw = jnp.concatenate([wl[:, :3072], _pad_cols(wl[:, 3072:3168], LANES),
                                _pad_cols(wl[:, 3168:3264], LANES), wl[:, 3264:3520]], axis=1).astype(bf16)
        mul = rw_mu[l]
        mu = jnp.concatenate([mul[:3072], jnp.pad(mul[3072:3168], (0, 32)),
                              jnp.pad(mul[3168:3264], (0, 32)), mul[3264:3520]])
        w_sg = wl[:, 3520:5568].astype(bf16)
        w_at = _pad_cols(wl[:, 5568:6400], MLA_COLS_PAD).astype(bf16)
        w_gt = wl[:, 6400:].astype(bf16)
        p_rw = matmul(h, w_rw, tm=1024, tn=896, name="in_rw").reshape(bsz, seq, RW_COLS_PAD)
        p_sg = matmul(h, w_sg, tm=1024, tn=1024, out_dtype=bf16, name="in_sg")
        p_at = matmul(h, w_at, tm=1024, tn=MLA_COLS_PAD, name="in_at")
        p_gate = matmul(h, w_gt, tm=1024, tn=1024, out_dtype=bf16, name="in_gate")
        lw, a, g = rw_lora(p_rw, row(mu[3 * RW_WIDTH:]), row(rw_w0[l]), row(rw_a0[l]),
                           _pad_rows(rw_w2[l], LANES).astype(bf16), _pad_rows(rw_a2[l], LANES).astype(bf16),
                           rw_g2[l].astype(bf16))
        y_rw = rwkv_mix(p_rw, lw, a, g, row(mu[:3 * RW_WIDTH]), row(rw_k_k[l]), row(rw_k_a[l]),
                        row(rw_r_k[l].reshape(-1)), row(rw_ln_g[l]), row(rw_ln_b[l])).reshape(t, RW_WIDTH)
        bias_full = jnp.repeat(sg_b[l].T, RW_HEAD_DIM, axis=1)
        y_sg = sgu(p_sg, row(sg_ln_g[l]), row(sg_ln_b[l]), jnp.tril(sg_w[l]).astype(bf16), bias_full)
        wq = mla_w_uq[l].reshape(MLA_Q_LORA, MLA_HEADS, MLA_QK_DIM) * (MLA_QK_DIM ** -0.5 * LOG2_E)
        wq = jnp.pad(wq, ((0, 0), (0, 0), (0, ATT_QK_PAD - MLA_QK_DIM))).reshape(MLA_Q_LORA, -1).astype(bf16)
        q, kv, kr = mla_prep(p_at, row(mla_q_norm_g[l]), row(mla_kv_norm_g[l]), wq, mla_w_ukv[l].astype(bf16),
                             rope_cs, rope_sn)
        y_at = attention(q.reshape(bsz, seq, -1), kv.reshape(bsz, seq, -1),
                         kr.reshape(bsz, seq, LANES)).reshape(t, -1)
        mix = merge(y_rw, y_sg, y_at, p_gate, p_rwkv[l].astype(bf16), p_sgu[l].astype(bf16), p_mla[l].astype(bf16))
        x, h2 = out_proj(mix, w_o[l].astype(bf16), x, per_batch(gate1), row(norm2_g[l]), per_batch(scale2),
                         per_batch(shift2), seq)
        yb, dest0, dest1 = _moe(h2, l, router_g_w[l], router_g_b[l], router_e_w[l], router_e_b[l],
                                exp_w_gate, exp_w_up, exp_w_down)
        if l + 1 < DEPTH:
            x, h = combine(yb, dest0, dest1, x, per_batch(gate2), row(norm1_g[l + 1]), per_batch(mods[l + 1][1]),
                           per_batch(mods[l + 1][0]), seq)
    return combine(yb, dest0, dest1, x, per_batch(gate2), row(final_g), None, None, seq).reshape(bsz, seq, d)
```

```python
import functools

import jax
import jax.numpy as jnp
from jax import lax
from jax.experimental import pallas as pl
from jax.experimental.pallas import tpu as pltpu

f32 = jnp.float32
bf16 = jnp.bfloat16

D_MODEL = 2048
DEPTH = 4
RW_HEADS = 16
RW_HEAD_DIM = 64
RW_WIDTH = 1024
RW_LORA_COLS = 512
RW_COLS_PAD = 3 * RW_WIDTH + RW_LORA_COLS
RW_GN_EPS = 64e-5
SG_CHUNK = 128
SG_GROUPS = 16
SG_WIDTH = 1024
MLA_HEADS = 8
MLA_Q_LORA = 512
MLA_KV_LORA = 256
MLA_QK_NOPE = 128
MLA_QK_ROPE = 64
MLA_QK_DIM = 192
MLA_V_DIM = 128
MLA_COLS_PAD = 896
ROPE_BASE = 10000.0
MOE_GROUPS = 8
MOE_EPG = 8
MOE_EXPERTS = 64
MOE_TOP_K = 2
MOE_D_FF = 384
MOE_BLOCK = 256
NORM_EPS = 1e-6
LOG2_E = 1.4426950408889634

LANES = 128
RW_CHUNK = 64
RW_SUB = 2
VMEM_LIMIT = 48 * 1024 * 1024


def _cparams(sem):
    return pltpu.CompilerParams(dimension_semantics=sem, vmem_limit_bytes=VMEM_LIMIT)


def _dot(a, b):
    return jnp.dot(a.astype(bf16), b.astype(bf16), preferred_element_type=f32)


def _dot_nt(a, b):
    return lax.dot_general(a.astype(bf16), b.astype(bf16), (((1,), (1,)), ((), ())),
                           preferred_element_type=f32)


def _dot_tn(a, b):
    return lax.dot_general(a.astype(bf16), b.astype(bf16), (((0,), (0,)), ((), ())),
                           preferred_element_type=f32)


def _mm_kernel(a_ref, w_ref, o_ref):
    o_ref[...] = _dot(a_ref[...], w_ref[...]).astype(o_ref.dtype)


def matmul(a, w, *, tm, tn, out_dtype=f32, layer=None, name="mm"):
    m, k = a.shape
    n = w.shape[-1]
    assert m % tm == 0 and n % tn == 0, (m, tm, n, tn)
    if layer is None:
        w_spec = pl.BlockSpec((k, tn), lambda j, i: (0, j))
    else:
        w_spec = pl.BlockSpec((None, k, tn), lambda j, i: (layer, 0, j))
    return pl.pallas_call(
        _mm_kernel,
        grid=(n // tn, m // tm),
        in_specs=[pl.BlockSpec((tm, k), lambda j, i: (i, 0)), w_spec],
        out_specs=pl.BlockSpec((tm, tn), lambda j, i: (i, j)),
        out_shape=jax.ShapeDtypeStruct((m, n), out_dtype),
        compiler_params=_cparams(("parallel", "parallel")),
        name=name,
    )(a, w)


RW_LORA_TM = 1024


def _softplus(z):
    return jnp.maximum(z, 0.0) + jnp.log(1.0 + jnp.exp(-jnp.abs(z)))


def _rw_lora_kernel(p_ref, mu_ref, w0_ref, a0_ref, ww_ref, wa_ref, wg_ref, lw_ref, a_ref, g_ref, prev_ref):
    @pl.when(pl.program_id(1) == 0)
    def _():
        prev_ref[...] = jnp.zeros_like(prev_ref)

    x = p_ref[...]
    rowi = lax.broadcasted_iota(jnp.int32, x.shape, 0)
    xprev = jnp.where(rowi == 0, prev_ref[...], pltpu.roll(x, 1, 0))
    prev_ref[...] = x[RW_LORA_TM - 1:RW_LORA_TM, :]
    xs = x + mu_ref[...] * (xprev - x)
    dec = _dot(jnp.tanh(xs[:, :LANES]), ww_ref[...])
    log_w = -_softplus(-(w0_ref[...] + dec)) - 0.5
    lw_ref[...] = -jnp.exp(log_w)
    a_ref[...] = jax.nn.sigmoid(a0_ref[...] + _dot(xs[:, LANES:2 * LANES], wa_ref[...]))
    g_ref[...] = _dot(jax.nn.sigmoid(xs[:, 2 * LANES:]), wg_ref[...]).astype(g_ref.dtype)


def rw_lora(p_rw, mu_lora, w0, a0, w_w2, w_a2, w_g2):
    b, s, _ = p_rw.shape
    cb = 3 * RW_WIDTH // RW_LORA_COLS
    vec = pl.BlockSpec((1, RW_WIDTH), lambda i, j: (0, 0))
    out = pl.BlockSpec((None, RW_LORA_TM, RW_WIDTH), lambda i, j: (i, j, 0))
    return pl.pallas_call(
        _rw_lora_kernel,
        grid=(b, s // RW_LORA_TM),
        in_specs=[pl.BlockSpec((None, RW_LORA_TM, RW_LORA_COLS), lambda i, j: (i, j, cb)),
                  pl.BlockSpec((1, RW_LORA_COLS), lambda i, j: (0, 0)), vec, vec,
                  pl.BlockSpec((LANES, RW_WIDTH), lambda i, j: (0, 0)),
                  pl.BlockSpec((LANES, RW_WIDTH), lambda i, j: (0, 0)),
                  pl.BlockSpec((2 * LANES, RW_WIDTH), lambda i, j: (0, 0))],
        out_specs=[out, out, out],
        out_shape=[jax.ShapeDtypeStruct((b, s, RW_WIDTH), f32), jax.ShapeDtypeStruct((b, s, RW_WIDTH), f32),
                   jax.ShapeDtypeStruct((b, s, RW_WIDTH), bf16)],
        scratch_shapes=[pltpu.VMEM((1, RW_LORA_COLS), f32)],
        compiler_params=_cparams(("parallel", "arbitrary")),
        name="rw_lora",
    )(p_rw, mu_lora, w0, a0, w_w2, w_a2, w_g2)


def _rwkv_kernel(p_ref, lw_ref, a_ref, g_ref, mu_ref, kk_ref, ka_ref, rk_ref, lng_ref, lnb_ref,
                 o_ref, state_ref, prev_ref):
    c = RW_CHUNK
    n = 2 * c
    nrow = RW_SUB * c
    pairs = range(RW_HEADS // 2)
    units = [(q, p) for q in range(RW_SUB) for p in pairs]

    @pl.when(pl.program_id(1) == 0)
    def _():
        state_ref[...] = jnp.zeros_like(state_ref)
        prev_ref[...] = jnp.zeros_like(prev_ref)

    row = lax.broadcasted_iota(jnp.int32, (n, n), 0)
    col = lax.broadcasted_iota(jnp.int32, (n, n), 1)
    same = (row >> 6) == (col >> 6)
    rpos = row & (c - 1)
    cpos = col & (c - 1)
    strict = same & (cpos < rpos)
    incl = same & (cpos <= rpos)
    eye = (row == col).astype(f32)
    blk16 = (row >> 4) == (col >> 4)
    blk32 = (row >> 5) == (col >> 5)
    low32 = blk32 & jnp.logical_not(blk16)
    low64 = same & jnp.logical_not(blk32)
    tr = lax.broadcasted_iota(jnp.int32, (c, c), 0)
    tc = lax.broadcasted_iota(jnp.int32, (c, c), 1)
    tril1 = (tc <= tr).astype(bf16)
    m0 = lax.broadcasted_iota(jnp.int32, (c, n), 1) < c
    m0f = lax.broadcasted_iota(jnp.int32, (nrow, n), 1) < c
    row0 = lax.broadcasted_iota(jnp.int32, (nrow, n), 0) == 0

    def expand(y):
        return jnp.concatenate([jnp.where(m0, y, 0.0), jnp.where(m0, 0.0, y)], axis=0)

    def collapse(y):
        return y[:c] + y[c:]

    def head_sum(y, mask):
        lo = jnp.sum(jnp.where(mask, y, 0.0), axis=-1, keepdims=True)
        hi = jnp.sum(jnp.where(mask, 0.0, y), axis=-1, keepdims=True)
        return jnp.where(mask, lo, hi)

    def shifted(base, p):
        sl = slice(base + p * LANES, base + (p + 1) * LANES)
        x = p_ref[:, sl]
        xprev = jnp.where(row0, prev_ref[:, sl], pltpu.roll(x, 1, 0))
        prev_ref[:, sl] = x[nrow - 1:nrow, :]
        return x + mu_ref[:, sl] * (xprev - x)

    sls = [slice(p * LANES, (p + 1) * LANES) for p in pairs]
    rf = [shifted(0, p) for p in pairs]
    k0f = [shifted(RW_WIDTH, p) for p in pairs]
    vf = [shifted(2 * RW_WIDTH, p) for p in pairs]
    af = [a_ref[:, s] for s in sls]
    kkrf = [k0f[p] * kk_ref[:, sls[p]] for p in pairs]
    kkf = [kkrf[p] * lax.rsqrt(jnp.maximum(head_sum(kkrf[p] * kkrf[p], m0f), 1e-24)) for p in pairs]
    kf = [k0f[p] * (1.0 + (af[p] - 1.0) * ka_ref[:, sls[p]]) for p in pairs]
    bonus = [head_sum(rf[p] * kf[p] * rk_ref[:, sls[p]], m0f) * vf[p] for p in pairs]

    def chunk(xs, q, p):
        return xs[p][q * c:(q + 1) * c, :]

    r = [chunk(rf, q, p) for q, p in units]
    v = [chunk(vf, q, p) for q, p in units]
    a = [chunk(af, q, p) for q, p in units]
    kk = [chunk(kkf, q, p) for q, p in units]
    k = [chunk(kf, q, p) for q, p in units]
    lw = [lw_ref[q * c:(q + 1) * c, sls[p]] for q, p in units]
    pairs = range(len(units))

    def cumsum(x):
        x1 = x.astype(bf16)
        res = x - x1.astype(f32)
        x2 = res.astype(bf16)
        x3 = (res - x2.astype(f32)).astype(bf16)
        return (jnp.dot(tril1, x1, preferred_element_type=f32) + jnp.dot(tril1, x2, preferred_element_type=f32)
                + jnp.dot(tril1, x3, preferred_element_type=f32))

    cum = [cumsum(lw[p]) for p in pairs]
    ctot = [cum[p][c - 1:c, :] for p in pairs]
    ginv = [jnp.exp(-cum[p]) for p in pairs]
    g2 = [jnp.exp(ctot[p] - cum[p]) for p in pairs]
    b = [kk[p] * a[p] for p in pairs]
    at = [-kk[p] * jnp.exp(cum[p] - lw[p]) for p in pairs]
    rt = [r[p] * jnp.exp(cum[p]) for p in pairs]
    bt = [b[p] * ginv[p] for p in pairs]
    kt = [k[p] * ginv[p] for p in pairs]
    pm = [_dot_nt(jnp.concatenate([expand(at[p]), expand(rt[p])], axis=0),
                  jnp.concatenate([expand(bt[p]), expand(kt[p])], axis=0)) for p in pairs]
    a_ab = [jnp.where(strict, pm[p][:n, :n], 0.0) for p in pairs]
    a_ak = [jnp.where(strict, pm[p][:n, n:], 0.0).astype(bf16) for p in pairs]
    m_rb = [jnp.where(incl, pm[p][n:, :n], 0.0).astype(bf16) for p in pairs]
    m_rk = [jnp.where(incl, pm[p][n:, n:], 0.0).astype(bf16) for p in pairs]
    ev = [expand(v[p]).astype(bf16) for p in pairs]
    akv = [_dot(a_ak[p], ev[p]) for p in pairs]
    o0 = [collapse(_dot(m_rk[p], ev[p])) for p in pairs]
    kv = [jnp.where(same, _dot_tn(v[p], k[p] * g2[p]), 0.0) for p in pairs]
    a0 = [jnp.where(blk16, a_ab[p], 0.0) for p in pairs]
    a2 = [_dot(a0[p], a0[p]) for p in pairs]
    p2 = [eye + a0[p] + a2[p] + _dot(a0[p], a2[p]) for p in pairs]
    a4 = [_dot(a2[p], a2[p]) for p in pairs]
    p3 = [p2[p] + _dot(p2[p], a4[p]) for p in pairs]
    a8 = [_dot(a4[p], a4[p]) for p in pairs]
    d = [p3[p] + _dot(p3[p], a8[p]) for p in pairs]
    x1 = [_dot(jnp.where(low32, a_ab[p], 0.0), d[p]) for p in pairs]
    d = [d[p] + _dot(d[p], x1[p]) for p in pairs]
    x2 = [_dot(jnp.where(low64, a_ab[p], 0.0), d[p]) for p in pairs]
    t = [(d[p] + _dot(d[p], x2[p])).astype(bf16) for p in pairs]
    tx = [_dot(t[p], jnp.concatenate([expand(at[p]), akv[p]], axis=1)) for p in pairs]
    at_hat = [collapse(tx[p][:, :n]) for p in pairs]
    u0 = [collapse(tx[p][:, n:]) for p in pairs]
    inv_n = 1.0 / RW_HEAD_DIM
    npair = RW_HEADS // 2
    state = [state_ref[h] for h in range(npair)]
    for q in range(RW_SUB):
        ids = [q * npair + h for h in range(npair)]
        so = [_dot_nt(jnp.concatenate([at_hat[i], rt[i]], axis=0), state[h]) for h, i in enumerate(ids)]
        u = [so[h][:c] + u0[i] for h, i in enumerate(ids)]
        su = [jnp.where(same, _dot_tn(u[h], b[i] * g2[i]), 0.0) for h, i in enumerate(ids)]
        mu_ = [collapse(_dot(m_rb[i], expand(u[h]))) for h, i in enumerate(ids)]
        state = [state[h] * jnp.exp(ctot[i]) + kv[i] + su[h] for h, i in enumerate(ids)]
        rows = slice(q * c, (q + 1) * c)
        for h, i in enumerate(ids):
            o = so[h][c:] + o0[i] + mu_[h]
            mean = head_sum(o, m0) * inv_n
            oc = o - mean
            var = head_sum(oc * oc, m0) * inv_n
            y = oc * lax.rsqrt(var + RW_GN_EPS) * lng_ref[:, sls[h]] + lnb_ref[:, sls[h]]
            y = y + bonus[h][rows, :]
            o_ref[rows, sls[h]] = (y * g_ref[rows, sls[h]].astype(f32)).astype(o_ref.dtype)
    for h in range(npair):
        state_ref[h] = state[h]


def rwkv_mix(p_rw, lw, a, g, mu, k_k, k_a, r_k, ln_g, ln_b):
    bsz, s, _ = p_rw.shape
    nrow = RW_SUB * RW_CHUNK
    spec = pl.BlockSpec((None, nrow, RW_WIDTH), lambda i, j: (i, j, 0))
    vec = pl.BlockSpec((1, RW_WIDTH), lambda i, j: (0, 0))
    return pl.pallas_call(
        _rwkv_kernel,
        grid=(bsz, s // nrow),
        in_specs=[pl.BlockSpec((None, nrow, 3 * RW_WIDTH), lambda i, j: (i, j, 0)), spec, spec, spec,
                  pl.BlockSpec((1, 3 * RW_WIDTH), lambda i, j: (0, 0)), vec, vec, vec, vec, vec],
        out_specs=spec,
        out_shape=jax.ShapeDtypeStruct((bsz, s, RW_WIDTH), bf16),
        scratch_shapes=[pltpu.VMEM((RW_HEADS // 2, LANES, LANES), f32), pltpu.VMEM((1, 3 * RW_WIDTH), f32)],
        compiler_params=_cparams(("parallel", "arbitrary")),
        name="rwkv_mix",
    )(p_rw, lw, a, g, mu, k_k, k_a, r_k, ln_g, ln_b)


def _sgu_kernel(p_ref, lng_ref, lnb_ref, w_ref, bias_ref, o_ref):
    x = p_ref[...].astype(f32)
    g = 0.5 * x * (1.0 + jnp.tanh(0.7978845608028654 * (x + 0.044715 * (x * x * x))))
    u = g[:, :SG_WIDTH]
    v = g[:, SG_WIDTH:]
    mu = jnp.mean(v, axis=-1, keepdims=True)
    vc = v - mu
    var = jnp.mean(vc * vc, axis=-1, keepdims=True)
    vn = (vc * lax.rsqrt(var + 1e-5) * lng_ref[...] + lnb_ref[...]).astype(bf16)
    first = lax.broadcasted_iota(jnp.int32, (SG_CHUNK, LANES), 1) < (LANES // 2)
    for q in range(SG_GROUPS // 2):
        sl = slice(q * LANES, (q + 1) * LANES)
        vq = vn[:, sl]
        lo = jnp.dot(w_ref[2 * q], vq, preferred_element_type=f32)
        hi = jnp.dot(w_ref[2 * q + 1], vq, preferred_element_type=f32)
        mixed = jnp.where(first, lo, hi) + bias_ref[:, sl]
        o_ref[:, sl] = (u[:, sl] * mixed).astype(o_ref.dtype)


def sgu(p_sg, ln_g, ln_b, w_tril, bias_full):
    t = p_sg.shape[0]
    return pl.pallas_call(
        _sgu_kernel,
        grid=(t // SG_CHUNK,),
        in_specs=[pl.BlockSpec((SG_CHUNK, 2 * SG_WIDTH), lambda i: (i, 0)),
                  pl.BlockSpec((1, SG_WIDTH), lambda i: (0, 0)),
                  pl.BlockSpec((1, SG_WIDTH), lambda i: (0, 0)),
                  pl.BlockSpec((SG_GROUPS, SG_CHUNK, SG_CHUNK), lambda i: (0, 0, 0)),
                  pl.BlockSpec((SG_CHUNK, SG_WIDTH), lambda i: (0, 0))],
        out_specs=pl.BlockSpec((SG_CHUNK, SG_WIDTH), lambda i: (i, 0)),
        out_shape=jax.ShapeDtypeStruct((t, SG_WIDTH), bf16),
        compiler_params=_cparams(("parallel",)),
        name="sgu",
    )(p_sg, ln_g, ln_b, w_tril, bias_full)


MLA_TM = 512
ATT_QK_PAD = 256


def _mla_prep_kernel(p_ref, qg_ref, kvg_ref, wq_ref, wkv_ref, cs_ref, sn_ref, q_ref, kv_ref, kr_ref):
    x = p_ref[...]

    def rms(y, g):
        return y * lax.rsqrt(jnp.mean(y * y, axis=-1, keepdims=True) + NORM_EPS) * g

    q = _dot(rms(x[:, :MLA_Q_LORA], qg_ref[...]), wq_ref[...])
    kv_ref[...] = _dot(rms(x[:, MLA_Q_LORA:MLA_Q_LORA + MLA_KV_LORA], kvg_ref[...]), wkv_ref[...]).astype(bf16)
    cs = cs_ref[...]
    sn = sn_ref[...]
    half = lax.broadcasted_iota(jnp.int32, cs.shape, 1) < (MLA_QK_ROPE // 2)

    def rope(y):
        swapped = jnp.where(half, pltpu.roll(y, LANES - MLA_QK_ROPE // 2, 1), pltpu.roll(y, MLA_QK_ROPE // 2, 1))
        return y * cs + swapped * sn

    kr_ref[...] = rope(x[:, MLA_Q_LORA + MLA_KV_LORA:]).astype(bf16)
    for h in range(MLA_HEADS):
        base = h * ATT_QK_PAD
        q_ref[:, base:base + LANES] = q[:, base:base + LANES].astype(bf16)
        q_ref[:, base + LANES:base + 2 * LANES] = rope(q[:, base + LANES:base + 2 * LANES]).astype(bf16)


def mla_prep(p_at, qg, kvg, wq, wkv, cs, sn):
    t = p_at.shape[0]
    full = lambda shape: pl.BlockSpec(shape, lambda i: (0, 0))
    return pl.pallas_call(
        _mla_prep_kernel,
        grid=(t // MLA_TM,),
        in_specs=[pl.BlockSpec((MLA_TM, MLA_COLS_PAD), lambda i: (i, 0)),
                  full((1, MLA_Q_LORA)), full((1, MLA_KV_LORA)),
                  full((MLA_Q_LORA, MLA_HEADS * ATT_QK_PAD)), full((MLA_KV_LORA, MLA_HEADS * 256)),
                  pl.BlockSpec((MLA_TM, LANES), lambda i: (i, 0)), pl.BlockSpec((MLA_TM, LANES), lambda i: (i, 0))],
        out_specs=[pl.BlockSpec((MLA_TM, MLA_HEADS * ATT_QK_PAD), lambda i: (i, 0)),
                   pl.BlockSpec((MLA_TM, MLA_HEADS * 256), lambda i: (i, 0)),
                   pl.BlockSpec((MLA_TM, LANES), lambda i: (i, 0))],
        out_shape=[jax.ShapeDtypeStruct((t, MLA_HEADS * ATT_QK_PAD), bf16),
                   jax.ShapeDtypeStruct((t, MLA_HEADS * 256), bf16),
                   jax.ShapeDtypeStruct((t, LANES), bf16)],
        compiler_params=_cparams(("parallel",)),
        name="mla_prep",
    )(p_at, qg, kvg, wq, wkv, cs, sn)


ATT_TQ = 1024
ATT_TK = 512


def _attn_kernel(q_ref, kn_ref, kr_ref, v_ref, o_ref):
    i = pl.program_id(2)
    hq = ATT_TQ // 2
    qs = [q_ref[:hq, :], q_ref[hq:, :]]

    def keys(j):
        rows = pl.ds(pl.multiple_of(j * ATT_TK, ATT_TK), ATT_TK)
        return jnp.concatenate([kn_ref[rows, :], kr_ref[rows, :]], axis=1), v_ref[rows, :]

    def update(carry, s, vj):
        m_old, l_old, acc = carry
        m_new = jnp.maximum(m_old, jnp.max(s, axis=-1, keepdims=True))
        p = jnp.exp2(s - m_new)
        alpha = jnp.exp2(m_old - m_new)
        l_new = alpha * l_old + jnp.sum(p, axis=-1, keepdims=True)
        acc = alpha * acc + jnp.dot(p.astype(bf16), vj, preferred_element_type=f32)
        return m_new, l_new, acc

    def body(j, carry):
        ka, va = keys(2 * j)
        kb, vb = keys(2 * j + 1)
        sa = [_dot_nt(qs[h], ka) for h in range(2)]
        sb = [_dot_nt(qs[h], kb) for h in range(2)]
        carry = tuple(update(carry[h], sa[h], va) for h in range(2))
        return tuple(update(carry[h], sb[h], vb) for h in range(2))

    init = tuple((jnp.full((hq, 1), -jnp.inf, f32), jnp.zeros((hq, 1), f32), jnp.zeros((hq, MLA_V_DIM), f32))
                 for _ in range(2))
    carry = lax.fori_loop(0, i, body, init)
    rowi = lax.broadcasted_iota(jnp.int32, (hq, ATT_TK), 0)
    coli = lax.broadcasted_iota(jnp.int32, (hq, ATT_TK), 1)
    diag = coli > rowi
    kj, vj = keys(2 * i)
    c0 = update(carry[0], jnp.where(diag, -jnp.inf, _dot_nt(qs[0], kj)), vj)
    c1 = update(carry[1], _dot_nt(qs[1], kj), vj)
    kj, vj = keys(2 * i + 1)
    c1 = update(c1, jnp.where(diag, -jnp.inf, _dot_nt(qs[1], kj)), vj)
    o_ref[:hq, :] = (c0[2] / c0[1]).astype(o_ref.dtype)
    o_ref[hq:, :] = (c1[2] / c1[1]).astype(o_ref.dtype)


def attention(q, kv, kr):
    b, s, _ = q.shape
    return pl.pallas_call(
        _attn_kernel,
        grid=(b, MLA_HEADS, s // ATT_TQ),
        in_specs=[pl.BlockSpec((None, ATT_TQ, ATT_QK_PAD), lambda bi, h, i: (bi, i, h)),
                  pl.BlockSpec((None, s, MLA_QK_NOPE), lambda bi, h, i: (bi, 0, 2 * h)),
                  pl.BlockSpec((None, s, LANES), lambda bi, h, i: (bi, 0, 0)),
                  pl.BlockSpec((None, s, MLA_V_DIM), lambda bi, h, i: (bi, 0, 2 * h + 1))],
        out_specs=pl.BlockSpec((None, ATT_TQ, MLA_V_DIM), lambda bi, h, i: (bi, i, h)),
        out_shape=jax.ShapeDtypeStruct((b, s, MLA_HEADS * MLA_V_DIM), bf16),
        compiler_params=_cparams(("parallel", "parallel", "parallel")),
        name="attention",
    )(q, kv, kr, kv)


OUT_TM = 512


def _merge_kernel(yr_ref, ys_ref, ya_ref, gate_ref, wr_ref, ws_ref, wa_ref, o_ref):
    d = D_MODEL
    acc = jax.nn.sigmoid(gate_ref[:, :d].astype(f32)) * jnp.dot(yr_ref[...], wr_ref[...], preferred_element_type=f32)
    acc += jax.nn.sigmoid(gate_ref[:, d:2 * d].astype(f32)) * jnp.dot(ys_ref[...], ws_ref[...],
                                                                      preferred_element_type=f32)
    acc += jax.nn.sigmoid(gate_ref[:, 2 * d:].astype(f32)) * jnp.dot(ya_ref[...], wa_ref[...],
                                                                     preferred_element_type=f32)
    o_ref[...] = acc.astype(o_ref.dtype)


def merge(y_rw, y_sg, y_at, p_gate, w_rw, w_sg, w_at):
    t = y_rw.shape[0]
    yspec = pl.BlockSpec((OUT_TM, RW_WIDTH), lambda i: (i, 0))
    wspec = pl.BlockSpec((RW_WIDTH, D_MODEL), lambda i: (0, 0), pipeline_mode=pl.Buffered(1))
    return pl.pallas_call(
        _merge_kernel,
        grid=(t // OUT_TM,),
        in_specs=[yspec, yspec, yspec, pl.BlockSpec((OUT_TM, 3 * D_MODEL), lambda i: (i, 0)), wspec, wspec, wspec],
        out_specs=pl.BlockSpec((OUT_TM, D_MODEL), lambda i: (i, 0)),
        out_shape=jax.ShapeDtypeStruct((t, D_MODEL), bf16),
        compiler_params=_cparams(("parallel",)),
        name="merge",
    )(y_rw, y_sg, y_at, p_gate, w_rw, w_sg, w_at)


def _out_kernel(mix_ref, wo_ref, x_ref, gate_ref, ng_ref, scale_ref, shift_ref, xo_ref, h_ref):
    xn = x_ref[...] + gate_ref[...] * jnp.dot(mix_ref[...], wo_ref[...], preferred_element_type=f32)
    xo_ref[...] = xn
    y = xn * lax.rsqrt(jnp.mean(xn * xn, axis=-1, keepdims=True) + NORM_EPS) * ng_ref[...]
    h_ref[...] = (y * (1.0 + scale_ref[...]) + shift_ref[...]).astype(h_ref.dtype)


def out_proj(mix, w_o, x, gate, norm_g, scale, shift, seq):
    t, d = x.shape
    per_b = pl.BlockSpec((None, 1, d), lambda i: (i * OUT_TM // seq, 0, 0))
    rows = pl.BlockSpec((OUT_TM, d), lambda i: (i, 0))
    return pl.pallas_call(
        _out_kernel,
        grid=(t // OUT_TM,),
        in_specs=[rows, pl.BlockSpec((d, d), lambda i: (0, 0), pipeline_mode=pl.Buffered(1)), rows, per_b,
                  pl.BlockSpec((1, d), lambda i: (0, 0)), per_b, per_b],
        out_specs=[rows, rows],
        out_shape=[jax.ShapeDtypeStruct((t, d), f32), jax.ShapeDtypeStruct((t, d), f32)],
        compiler_params=_cparams(("parallel",)),
        name="out_proj",
    )(mix, w_o, x, gate, norm_g, scale, shift)


def _moe_kernel(layer, run_ref, rexp_ref, nu_ref, nr_ref, tok_ref, h_ref, rw_ref, wg_hbm, wu_hbm, wd_hbm, o_ref,
                wg_f, wu_f, wd_f, wgu_s, wd_s, xbuf, gsem, wsem):
    s = pl.program_id(0)
    nblk = pl.num_programs(0) - 1
    cur = jnp.minimum(s, nblk - 1)
    prv = jnp.maximum(s - 1, 0)
    n_used = nu_ref[0]
    n_runs = nr_ref[0]
    compute = (s >= 1) & (s <= n_used)

    def weight_copies(j):
        e = rexp_ref[j]
        k = j % 2
        return [pltpu.make_async_copy(src.at[layer, e], dst.at[k], wsem.at[k])
                for src, dst in ((wg_hbm, wg_f), (wu_hbm, wu_f), (wd_hbm, wd_f))]

    def fetch(j):
        for cp in weight_copies(j):
            cp.start()

    @pl.when(s == 0)
    def _():
        fetch(0)

        @pl.when(n_runs > 1)
        def _():
            fetch(1)

    @pl.when(compute)
    def _():
        b = prv % 2
        pltpu.make_async_copy(h_ref.at[pl.ds(0, MOE_BLOCK), :], xbuf.at[b], gsem.at[b]).wait()

    @pl.when(s < n_used)
    def _():
        b = s % 2

        def issue(r, carry):
            pltpu.make_async_copy(h_ref.at[pl.ds(tok_ref[0, r], 1), :], xbuf.at[b, pl.ds(r, 1), :],
                                  gsem.at[b]).start(priority=1)
            return carry

        lax.fori_loop(0, MOE_BLOCK, issue, 0, unroll=8)

    @pl.when((s < n_used) & ((s == 0) | (run_ref[cur] != run_ref[prv])))
    def _():
        j = run_ref[cur]
        k = j % 2
        for cp in weight_copies(j):
            cp.wait()
        wgu_s[k, :, :MOE_D_FF] = wg_f[k].astype(bf16)
        wgu_s[k, :, MOE_D_FF:] = wu_f[k].astype(bf16)
        wd_s[k] = wd_f[k].astype(bf16)

        @pl.when(j + 2 < n_runs)
        def _():
            fetch(j + 2)

    @pl.when(compute)
    def _():
        k = run_ref[prv] % 2
        gu = jnp.dot(xbuf[prv % 2].astype(bf16), wgu_s[k], preferred_element_type=f32)
        g = gu[:, :MOE_D_FF]
        hmid = (g * jax.nn.sigmoid(g) * gu[:, MOE_D_FF:] * rw_ref[...]).astype(bf16)
        o_ref[...] = jnp.dot(hmid, wd_s[k], preferred_element_type=f32).astype(o_ref.dtype)

    @pl.when(s > n_used)
    def _():
        o_ref[...] = jnp.zeros_like(o_ref)


def moe_ffn(block_expert, n_used, h, row_token, row_weight, w_gate, w_up, w_down, layer):
    rows = row_token.shape[0]
    d = h.shape[1]
    nblk = rows // MOE_BLOCK
    changed = jnp.concatenate([jnp.zeros((1,), jnp.int32),
                               (block_expert[1:] != block_expert[:-1]).astype(jnp.int32)])
    run = jnp.cumsum(changed).astype(jnp.int32)
    run_expert = jnp.zeros((nblk,), jnp.int32).at[run].set(block_expert)
    n_runs = run[jnp.maximum(n_used[0] - 1, 0)].reshape(1) + 1

    def cur_block(s, *_):
        return (jnp.minimum(s, nblk - 1), 0, 0)

    def prev_rows(s, run_, rexp_, nu, nr):
        return (jnp.clip(s - 1, 0, nu[0] - 1), 0)

    any_spec = pl.BlockSpec(memory_space=pl.ANY)
    return pl.pallas_call(
        functools.partial(_moe_kernel, layer),
        grid_spec=pltpu.PrefetchScalarGridSpec(
            num_scalar_prefetch=4,
            grid=(nblk + 1,),
            in_specs=[pl.BlockSpec((None, 1, MOE_BLOCK), cur_block, memory_space=pltpu.SMEM),
                      any_spec,
                      pl.BlockSpec((MOE_BLOCK, 1), prev_rows),
                      any_spec, any_spec, any_spec],
            out_specs=pl.BlockSpec((MOE_BLOCK, d), lambda s, *_: (jnp.maximum(s - 1, 0), 0)),
            scratch_shapes=[pltpu.VMEM((2, d, MOE_D_FF), f32), pltpu.VMEM((2, d, MOE_D_FF), f32),
                            pltpu.VMEM((2, MOE_D_FF, d), f32),
                            pltpu.VMEM((2, d, 2 * MOE_D_FF), bf16), pltpu.VMEM((2, MOE_D_FF, d), bf16),
                            pltpu.VMEM((2, MOE_BLOCK, d), f32),
                            pltpu.SemaphoreType.DMA((2,)), pltpu.SemaphoreType.DMA((2,))],
        ),
        out_shape=jax.ShapeDtypeStruct((rows, d), f32),
        compiler_params=_cparams(("arbitrary",)),
        name="moe_ffn",
    )(run, run_expert, n_used, n_runs.astype(jnp.int32), row_token.reshape(nblk, 1, MOE_BLOCK), h, row_weight,
      w_gate, w_up, w_down)


def _rms(x, g):
    return x * lax.rsqrt(jnp.mean(x * x, axis=-1, keepdims=True) + NORM_EPS) * g


def _pad_cols(w, width):
    return jnp.pad(w, ((0, 0), (0, width - w.shape[1])))


def _pad_rows(w, height):
    return jnp.pad(w, ((0, height - w.shape[0]), (0, 0)))


def _moe(h, layer, g_w, g_b, e_w, e_b, w_gate, w_up, w_down):
    t, d = h.shape
    w_r = _pad_cols(jnp.concatenate([g_w, e_w], axis=1), LANES)
    logits = matmul(h, w_r, tm=1024, tn=LANES, name="router")
    group_logits = logits[:, :MOE_GROUPS] + g_b
    group = jnp.argmax(group_logits, axis=-1)
    group_w = jnp.take_along_axis(jax.nn.softmax(group_logits, axis=-1), group[:, None], axis=-1)
    exp_logits = (logits[:, MOE_GROUPS:MOE_GROUPS + MOE_EXPERTS] + e_b).reshape(t, MOE_GROUPS, MOE_EPG)
    in_group = jnp.take_along_axis(exp_logits, group[:, None, None], axis=1)[:, 0]
    top_logit, top_idx = lax.top_k(in_group, MOE_TOP_K)
    weights = (group_w * jax.nn.softmax(top_logit, axis=-1)).reshape(-1)
    expert_ids = (group[:, None] * MOE_EPG + top_idx).reshape(-1).astype(jnp.int32)
    n_assign = t * MOE_TOP_K
    n_blocks = -(-n_assign // MOE_BLOCK) + MOE_EXPERTS
    rows = n_blocks * MOE_BLOCK
    onehot = (expert_ids[:, None] == jnp.arange(MOE_EXPERTS, dtype=jnp.int32)[None, :]).astype(jnp.int32)
    csum = jnp.cumsum(onehot, axis=0)
    rank = jnp.take_along_axis(csum, expert_ids[:, None], axis=1)[:, 0] - 1
    counts = csum[-1]
    padded = (counts + MOE_BLOCK - 1) // MOE_BLOCK * MOE_BLOCK
    ends = jnp.cumsum(padded)
    starts = ends - padded
    dest = starts[expert_ids] + rank
    token_ids = jnp.arange(n_assign, dtype=jnp.int32) // MOE_TOP_K
    row_token = jnp.zeros((rows,), jnp.int32).at[dest].set(token_ids)
    row_weight = jnp.zeros((rows,), f32).at[dest].set(weights)
    block_start = jnp.arange(n_blocks, dtype=jnp.int32) * MOE_BLOCK
    block_expert = jnp.minimum(jnp.searchsorted(ends, block_start, side='right'),
                               MOE_EXPERTS - 1).astype(jnp.int32)
    n_used = (ends[-1] // MOE_BLOCK).astype(jnp.int32).reshape(1)
    yb = moe_ffn(block_expert, n_used, h, row_token, row_weight[:, None], w_gate, w_up, w_down, layer)
    dest2 = dest.reshape(t, MOE_TOP_K)
    return yb, dest2[:, 0], dest2[:, 1]


def _gather_rows(d0_ref, d1_ref, yb_ref, buf, sem):
    def issue(r, carry):
        pltpu.make_async_copy(yb_ref.at[pl.ds(d0_ref[0, r], 1), :], buf.at[0, pl.ds(r, 1), :], sem.at[0]).start()
        pltpu.make_async_copy(yb_ref.at[pl.ds(d1_ref[0, r], 1), :], buf.at[1, pl.ds(r, 1), :], sem.at[1]).start()
        return carry

    lax.fori_loop(0, OUT_TM, issue, 0, unroll=8)
    for k in range(MOE_TOP_K):
        pltpu.make_async_copy(yb_ref.at[pl.ds(0, OUT_TM), :], buf.at[k], sem.at[k]).wait()
    return buf[0] + buf[1]


def _combine_kernel(d0_ref, d1_ref, yb_ref, x_ref, gate_ref, ng_ref, scale_ref, shift_ref, xo_ref, h_ref,
                    buf, sem):
    xn = x_ref[...] + gate_ref[...] * _gather_rows(d0_ref, d1_ref, yb_ref, buf, sem)
    xo_ref[...] = xn
    y = xn * lax.rsqrt(jnp.mean(xn * xn, axis=-1, keepdims=True) + NORM_EPS) * ng_ref[...]
    h_ref[...] = (y * (1.0 + scale_ref[...]) + shift_ref[...]).astype(h_ref.dtype)


def _final_kernel(d0_ref, d1_ref, yb_ref, x_ref, gate_ref, ng_ref, o_ref, buf, sem):
    xn = x_ref[...] + gate_ref[...] * _gather_rows(d0_ref, d1_ref, yb_ref, buf, sem)
    o_ref[...] = xn * lax.rsqrt(jnp.mean(xn * xn, axis=-1, keepdims=True) + NORM_EPS) * ng_ref[...]


def combine(yb, dest0, dest1, x, gate, norm_g, scale, shift, seq):
    t, d = x.shape
    nb = t // OUT_TM
    per_b = pl.BlockSpec((None, 1, d), lambda i: (i * OUT_TM // seq, 0, 0))
    rows = pl.BlockSpec((OUT_TM, d), lambda i: (i, 0))
    vec = pl.BlockSpec((1, d), lambda i: (0, 0))
    idx = pl.BlockSpec((None, 1, OUT_TM), lambda i: (i, 0, 0), memory_space=pltpu.SMEM)
    any_spec = pl.BlockSpec(memory_space=pl.ANY)
    scratch = [pltpu.VMEM((MOE_TOP_K, OUT_TM, d), f32), pltpu.SemaphoreType.DMA((MOE_TOP_K,))]
    d0 = dest0.reshape(nb, 1, OUT_TM)
    d1 = dest1.reshape(nb, 1, OUT_TM)
    if scale is None:
        return pl.pallas_call(
            _final_kernel, grid=(nb,), in_specs=[idx, idx, any_spec, rows, per_b, vec], out_specs=rows,
            out_shape=jax.ShapeDtypeStruct((t, d), f32), scratch_shapes=scratch,
            compiler_params=_cparams(("arbitrary",)), name="final_norm")(d0, d1, yb, x, gate, norm_g)
    return pl.pallas_call(
        _combine_kernel, grid=(nb,), in_specs=[idx, idx, any_spec, rows, per_b, vec, per_b, per_b],
        out_specs=[rows, rows],
        out_shape=[jax.ShapeDtypeStruct((t, d), f32), jax.ShapeDtypeStruct((t, d), bf16)],
        scratch_shapes=scratch, compiler_params=_cparams(("arbitrary",)),
        name="combine")(d0, d1, yb, x, gate, norm_g, scale, shift)


def kernel(x, c, positions, ada_w, ada_b, norm1_g, norm2_g, final_g, w_in, rw_mu, rw_w0, rw_w2, rw_a0, rw_a2, rw_g2, rw_k_k, rw_k_a, rw_r_k, rw_ln_g, rw_ln_b, sg_ln_g, sg_ln_b, sg_w, sg_b, mla_q_norm_g, mla_w_uq, mla_kv_norm_g, mla_w_ukv, p_rwkv, p_sgu, p_mla, w_o, router_g_w, router_g_b, router_e_w, router_e_b, exp_w_gate, exp_w_up, exp_w_down):
    bsz, seq, d = x.shape
    t = bsz * seq
    half = MLA_QK_ROPE // 2
    inv_freq = ROPE_BASE ** (-jnp.arange(half, dtype=f32) / half)
    ang = (positions.astype(f32)[..., None] * inv_freq).reshape(t, half)
    cos, sin = jnp.cos(ang), jnp.sin(ang)
    zpad = jnp.zeros((t, LANES - MLA_QK_ROPE), f32)
    rope_cs = jnp.concatenate([cos, cos, zpad], axis=1)
    rope_sn = jnp.concatenate([-sin, sin, zpad], axis=1)
    c_act = jnp.pad(jax.nn.silu(c), ((0, 8 - bsz), (0, 0)))
    x = x.reshape(t, d)
    row = lambda vec: vec[None, :]

    def per_batch(vec):
        return vec[:, None, :]

    mods = []
    for l in range(DEPTH):
        mod = matmul(c_act, ada_w, tm=8, tn=1024, layer=l, name="ada")[:bsz] + ada_b[l]
        mods.append(jnp.split(mod, 6, axis=-1))

    shift1, scale1 = mods[0][0], mods[0][1]
    h = ((_rms(x, norm1_g[0]).reshape(bsz, seq, d) * (1.0 + scale1[:, None, :]) + shift1[:, None, :])
         .reshape(t, d).astype(bf16))
    for l in range(DEPTH):
        _, _, gate1, shift2, scale2, gate2 = mods[l]
        wl = w_in[l]
        w_rw = jnp.concatenate([wl[:, :3072], _pad_cols(wl[:, 3072:3168], LANES),
                                _pad_cols(wl[:, 3168:3264], LANES), wl[:, 3264:3520]], axis=1).astype(bf16)
        mul = rw_mu[l]
        mu = jnp.concatenate([mul[:3072], jnp.pad(mul[3072:3168], (0, 32)),
                              jnp.pad(mul[3168:3264], (0, 32)), mul[3264:3520]])
        w_sg = wl[:, 3520:5568].astype(bf16)
        w_at = _pad_cols(wl[:, 5568:6400], MLA_COLS_PAD).astype(bf16)
        w_gt = wl[:, 6400:].astype(bf16)
        p_rw = matmul(h, w_rw, tm=1024, tn=896, name="in_rw").reshape(bsz, seq, RW_COLS_PAD)
        p_sg = matmul(h, w_sg, tm=1024, tn=1024, out_dtype=bf16, name="in_sg")
        p_at = matmul(h, w_at, tm=1024, tn=MLA_COLS_PAD, name="in_at")
        p_gate = matmul(h, w_gt, tm=1024, tn=1024, out_dtype=bf16, name="in_gate")
        lw, a, g = rw_lora(p_rw, row(mu[3 * RW_WIDTH:]), row(rw_w0[l]), row(rw_a0[l]),
                           _pad_rows(rw_w2[l], LANES).astype(bf16), _pad_rows(rw_a2[l], LANES).astype(bf16),
                           rw_g2[l].astype(bf16))
        y_rw = rwkv_mix(p_rw, lw, a, g, row(mu[:3 * RW_WIDTH]), row(rw_k_k[l]), row(rw_k_a[l]),
                        row(rw_r_k[l].reshape(-1)), row(rw_ln_g[l]), row(rw_ln_b[l])).reshape(t, RW_WIDTH)
        bias_full = jnp.repeat(sg_b[l].T, RW_HEAD_DIM, axis=1)
        y_sg = sgu(p_sg, row(sg_ln_g[l]), row(sg_ln_b[l]), jnp.tril(sg_w[l]).astype(bf16), bias_full)
        wq = mla_w_uq[l].reshape(MLA_Q_LORA, MLA_HEADS, MLA_QK_DIM) * (MLA_QK_DIM ** -0.5 * LOG2_E)
        wq = jnp.pad(wq, ((0, 0), (0, 0), (0, ATT_QK_PAD - MLA_QK_DIM))).reshape(MLA_Q_LORA, -1).astype(bf16)
        q, kv, kr = mla_prep(p_at, row(mla_q_norm_g[l]), row(mla_kv_norm_g[l]), wq, mla_w_ukv[l].astype(bf16),
                             rope_cs, rope_sn)
        y_at = attention(q.reshape(bsz, seq, -1), kv.reshape(bsz, seq, -1),
                         kr.reshape(bsz, seq, LANES)).reshape(t, -1)
        mix = merge(y_rw, y_sg, y_at, p_gate, p_rwkv[l].astype(bf16), p_sgu[l].astype(bf16), p_mla[l].astype(bf16))
        x, h2 = out_proj(mix, w_o[l].astype(bf16), x, per_batch(gate1), row(norm2_g[l]), per_batch(scale2),
                         per_batch(shift2), seq)
        yb, dest0, dest1 = _moe(h2, l, router_g_w[l], router_g_b[l], router_e_w[l], router_e_b[l],
                                exp_w_gate, exp_w_up, exp_w_down)
        if l + 1 < DEPTH:
            x, h = combine(yb, dest0, dest1, x, per_batch(gate2), row(norm1_g[l + 1]), per_batch(mods[l + 1][1]),
                           per_batch(mods[l + 1][0]), seq)
    return combine(yb, dest0, dest1, x, per_batch(gate2), row(final_g), None, None, seq).reshape(bsz, seq, d)
```

```python
import functools

import jax
import jax.numpy as jnp
from jax import lax
from jax.experimental import pallas as pl
from jax.experimental.pallas import tpu as pltpu

f32 = jnp.float32
bf16 = jnp.bfloat16

D_MODEL = 2048
DEPTH = 4
RW_HEADS = 16
RW_HEAD_DIM = 64
RW_WIDTH = 1024
RW_LORA_COLS = 512
RW_COLS_PAD = 3 * RW_WIDTH + RW_LORA_COLS
RW_GN_EPS = 64e-5
SG_CHUNK = 128
SG_GROUPS = 16
SG_WIDTH = 1024
MLA_HEADS = 8
MLA_Q_LORA = 512
MLA_KV_LORA = 256
MLA_QK_NOPE = 128
MLA_QK_ROPE = 64
MLA_QK_DIM = 192
MLA_V_DIM = 128
MLA_COLS_PAD = 896
ROPE_BASE = 10000.0
MOE_GROUPS = 8
MOE_EPG = 8
MOE_EXPERTS = 64
MOE_TOP_K = 2
MOE_D_FF = 384
MOE_BLOCK = 256
NORM_EPS = 1e-6
LOG2_E = 1.4426950408889634

LANES = 128
RW_CHUNK = 64
RW_SUB = 4
VMEM_LIMIT = 48 * 1024 * 1024


def _cparams(sem):
    return pltpu.CompilerParams(dimension_semantics=sem, vmem_limit_bytes=VMEM_LIMIT)


def _dot(a, b):
    return jnp.dot(a.astype(bf16), b.astype(bf16), preferred_element_type=f32)


def _dot_nt(a, b):
    return lax.dot_general(a.astype(bf16), b.astype(bf16), (((1,), (1,)), ((), ())),
                           preferred_element_type=f32)


def _dot_tn(a, b):
    return lax.dot_general(a.astype(bf16), b.astype(bf16), (((0,), (0,)), ((), ())),
                           preferred_element_type=f32)


def _mm_kernel(a_ref, w_ref, o_ref):
    o_ref[...] = _dot(a_ref[...], w_ref[...]).astype(o_ref.dtype)


def matmul(a, w, *, tm, tn, out_dtype=f32, layer=None, name="mm"):
    m, k = a.shape
    n = w.shape[-1]
    assert m % tm == 0 and n % tn == 0, (m, tm, n, tn)
    if layer is None:
        w_spec = pl.BlockSpec((k, tn), lambda j, i: (0, j))
    else:
        w_spec = pl.BlockSpec((None, k, tn), lambda j, i: (layer, 0, j))
    return pl.pallas_call(
        _mm_kernel,
        grid=(n // tn, m // tm),
        in_specs=[pl.BlockSpec((tm, k), lambda j, i: (i, 0)), w_spec],
        out_specs=pl.BlockSpec((tm, tn), lambda j, i: (i, j)),
        out_shape=jax.ShapeDtypeStruct((m, n), out_dtype),
        compiler_params=_cparams(("parallel", "parallel")),
        name=name,
    )(a, w)


RW_LORA_TM = 1024


def _softplus(z):
    return jnp.maximum(z, 0.0) + jnp.log(1.0 + jnp.exp(-jnp.abs(z)))


def _rw_lora_kernel(p_ref, mu_ref, w0_ref, a0_ref, ww_ref, wa_ref, wg_ref, lw_ref, a_ref, g_ref, prev_ref):
    @pl.when(pl.program_id(1) == 0)
    def _():
        prev_ref[...] = jnp.zeros_like(prev_ref)

    x = p_ref[...]
    rowi = lax.broadcasted_iota(jnp.int32, x.shape, 0)
    xprev = jnp.where(rowi == 0, prev_ref[...], pltpu.roll(x, 1, 0))
    prev_ref[...] = x[RW_LORA_TM - 1:RW_LORA_TM, :]
    xs = x + mu_ref[...] * (xprev - x)
    dec = _dot(jnp.tanh(xs[:, :LANES]), ww_ref[...])
    log_w = -_softplus(-(w0_ref[...] + dec)) - 0.5
    lw_ref[...] = -jnp.exp(log_w)
    a_ref[...] = jax.nn.sigmoid(a0_ref[...] + _dot(xs[:, LANES:2 * LANES], wa_ref[...]))
    g_ref[...] = _dot(jax.nn.sigmoid(xs[:, 2 * LANES:]), wg_ref[...]).astype(g_ref.dtype)


def rw_lora(p_rw, mu_lora, w0, a0, w_w2, w_a2, w_g2):
    b, s, _ = p_rw.shape
    cb = 3 * RW_WIDTH // RW_LORA_COLS
    vec = pl.BlockSpec((1, RW_WIDTH), lambda i, j: (0, 0))
    out = pl.BlockSpec((None, RW_LORA_TM, RW_WIDTH), lambda i, j: (i, j, 0))
    return pl.pallas_call(
        _rw_lora_kernel,
        grid=(b, s // RW_LORA_TM),
        in_specs=[pl.BlockSpec((None, RW_LORA_TM, RW_LORA_COLS), lambda i, j: (i, j, cb)),
                  pl.BlockSpec((1, RW_LORA_COLS), lambda i, j: (0, 0)), vec, vec,
                  pl.BlockSpec((LANES, RW_WIDTH), lambda i, j: (0, 0)),
                  pl.BlockSpec((LANES, RW_WIDTH), lambda i, j: (0, 0)),
                  pl.BlockSpec((2 * LANES, RW_WIDTH), lambda i, j: (0, 0))],
        out_specs=[out, out, out],
        out_shape=[jax.ShapeDtypeStruct((b, s, RW_WIDTH), f32), jax.ShapeDtypeStruct((b, s, RW_WIDTH), f32),
                   jax.ShapeDtypeStruct((b, s, RW_WIDTH), bf16)],
        scratch_shapes=[pltpu.VMEM((1, RW_LORA_COLS), f32)],
        compiler_params=_cparams(("parallel", "arbitrary")),
        name="rw_lora",
    )(p_rw, mu_lora, w0, a0, w_w2, w_a2, w_g2)


def _rwkv_kernel(p_ref, lw_ref, a_ref, g_ref, mu_ref, kk_ref, ka_ref, rk_ref, lng_ref, lnb_ref,
                 o_ref, state_ref, prev_ref):
    c = RW_CHUNK
    n = 2 * c
    nrow = RW_SUB * c
    pairs = range(RW_HEADS // 2)
    units = [(q, p) for q in range(RW_SUB) for p in pairs]

    @pl.when(pl.program_id(1) == 0)
    def _():
        state_ref[...] = jnp.zeros_like(state_ref)
        prev_ref[...] = jnp.zeros_like(prev_ref)

    row = lax.broadcasted_iota(jnp.int32, (n, n), 0)
    col = lax.broadcasted_iota(jnp.int32, (n, n), 1)
    same = (row >> 6) == (col >> 6)
    rpos = row & (c - 1)
    cpos = col & (c - 1)
    strict = same & (cpos < rpos)
    incl = same & (cpos <= rpos)
    eye = (row == col).astype(f32)
    blk16 = (row >> 4) == (col >> 4)
    blk32 = (row >> 5) == (col >> 5)
    low32 = blk32 & jnp.logical_not(blk16)
    low64 = same & jnp.logical_not(blk32)
    tr = lax.broadcasted_iota(jnp.int32, (c, c), 0)
    tc = lax.broadcasted_iota(jnp.int32, (c, c), 1)
    tril1 = (tc <= tr).astype(bf16)
    m0 = lax.broadcasted_iota(jnp.int32, (c, n), 1) < c
    m0f = lax.broadcasted_iota(jnp.int32, (nrow, n), 1) < c
    row0 = lax.broadcasted_iota(jnp.int32, (nrow, n), 0) == 0

    def expand(y):
        return jnp.concatenate([jnp.where(m0, y, 0.0), jnp.where(m0, 0.0, y)], axis=0)

    def collapse(y):
        return y[:c] + y[c:]

    def head_sum(y, mask):
        lo = jnp.sum(jnp.where(mask, y, 0.0), axis=-1, keepdims=True)
        hi = jnp.sum(jnp.where(mask, 0.0, y), axis=-1, keepdims=True)
        return jnp.where(mask, lo, hi)

    def shifted(base, p):
        sl = slice(base + p * LANES, base + (p + 1) * LANES)
        x = p_ref[:, sl]
        xprev = jnp.where(row0, prev_ref[:, sl], pltpu.roll(x, 1, 0))
        prev_ref[:, sl] = x[nrow - 1:nrow, :]
        return x + mu_ref[:, sl] * (xprev - x)

    sls = [slice(p * LANES, (p + 1) * LANES) for p in pairs]
    rf = [shifted(0, p) for p in pairs]
    k0f = [shifted(RW_WIDTH, p) for p in pairs]
    vf = [shifted(2 * RW_WIDTH, p) for p in pairs]
    af = [a_ref[:, s] for s in sls]
    kkrf = [k0f[p] * kk_ref[:, sls[p]] for p in pairs]
    kkf = [kkrf[p] * lax.rsqrt(jnp.maximum(head_sum(kkrf[p] * kkrf[p], m0f), 1e-24)) for p in pairs]
    kf = [k0f[p] * (1.0 + (af[p] - 1.0) * ka_ref[:, sls[p]]) for p in pairs]
    bonus = [head_sum(rf[p] * kf[p] * rk_ref[:, sls[p]], m0f) * vf[p] for p in pairs]

    def chunk(xs, q, p):
        return xs[p][q * c:(q + 1) * c, :]

    r = [chunk(rf, q, p) for q, p in units]
    v = [chunk(vf, q, p) for q, p in units]
    a = [chunk(af, q, p) for q, p in units]
    kk = [chunk(kkf, q, p) for q, p in units]
    k = [chunk(kf, q, p) for q, p in units]
    lw = [lw_ref[q * c:(q + 1) * c, sls[p]] for q, p in units]
    pairs = range(len(units))

    def cumsum(x):
        x1 = x.astype(bf16)
        res = x - x1.astype(f32)
        x2 = res.astype(bf16)
        x3 = (res - x2.astype(f32)).astype(bf16)
        return (jnp.dot(tril1, x1, preferred_element_type=f32) + jnp.dot(tril1, x2, preferred_element_type=f32)
                + jnp.dot(tril1, x3, preferred_element_type=f32))

    cum = [cumsum(lw[p]) for p in pairs]
    ctot = [cum[p][c - 1:c, :] for p in pairs]
    ginv = [jnp.exp(-cum[p]) for p in pairs]
    g2 = [jnp.exp(ctot[p] - cum[p]) for p in pairs]
    b = [kk[p] * a[p] for p in pairs]
    at = [-kk[p] * jnp.exp(cum[p] - lw[p]) for p in pairs]
    rt = [r[p] * jnp.exp(cum[p]) for p in pairs]
    bt = [b[p] * ginv[p] for p in pairs]
    kt = [k[p] * ginv[p] for p in pairs]
    pm = [_dot_nt(jnp.concatenate([expand(at[p]), expand(rt[p])], axis=0),
                  jnp.concatenate([expand(bt[p]), expand(kt[p])], axis=0)) for p in pairs]
    a_ab = [jnp.where(strict, pm[p][:n, :n], 0.0) for p in pairs]
    a_ak = [jnp.where(strict, pm[p][:n, n:], 0.0).astype(bf16) for p in pairs]
    m_rb = [jnp.where(incl, pm[p][n:, :n], 0.0).astype(bf16) for p in pairs]
    m_rk = [jnp.where(incl, pm[p][n:, n:], 0.0).astype(bf16) for p in pairs]
    ev = [expand(v[p]).astype(bf16) for p in pairs]
    akv = [_dot(a_ak[p], ev[p]) for p in pairs]
    o0 = [collapse(_dot(m_rk[p], ev[p])) for p in pairs]
    kv = [jnp.where(same, _dot_tn(v[p], k[p] * g2[p]), 0.0) for p in pairs]
    a0 = [jnp.where(blk16, a_ab[p], 0.0) for p in pairs]
    a2 = [_dot(a0[p], a0[p]) for p in pairs]
    p2 = [eye + a0[p] + a2[p] + _dot(a0[p], a2[p]) for p in pairs]
    a4 = [_dot(a2[p], a2[p]) for p in pairs]
    p3 = [p2[p] + _dot(p2[p], a4[p]) for p in pairs]
    a8 = [_dot(a4[p], a4[p]) for p in pairs]
    d = [p3[p] + _dot(p3[p], a8[p]) for p in pairs]
    x1 = [_dot(jnp.where(low32, a_ab[p], 0.0), d[p]) for p in pairs]
    d = [d[p] + _dot(d[p], x1[p]) for p in pairs]
    x2 = [_dot(jnp.where(low64, a_ab[p], 0.0), d[p]) for p in pairs]
    t = [(d[p] + _dot(d[p], x2[p])).astype(bf16) for p in pairs]
    tx = [_dot(t[p], jnp.concatenate([expand(at[p]), akv[p]], axis=1)) for p in pairs]
    at_hat = [collapse(tx[p][:, :n]) for p in pairs]
    u0 = [collapse(tx[p][:, n:]) for p in pairs]
    inv_n = 1.0 / RW_HEAD_DIM
    npair = RW_HEADS // 2
    state = [state_ref[h] for h in range(npair)]
    for q in range(RW_SUB):
        ids = [q * npair + h for h in range(npair)]
        so = [_dot_nt(jnp.concatenate([at_hat[i], rt[i]], axis=0), state[h]) for h, i in enumerate(ids)]
        u = [so[h][:c] + u0[i] for h, i in enumerate(ids)]
        su = [jnp.where(same, _dot_tn(u[h], b[i] * g2[i]), 0.0) for h, i in enumerate(ids)]
        mu_ = [collapse(_dot(m_rb[i], expand(u[h]))) for h, i in enumerate(ids)]
        state = [state[h] * jnp.exp(ctot[i]) + kv[i] + su[h] for h, i in enumerate(ids)]
        rows = slice(q * c, (q + 1) * c)
        for h, i in enumerate(ids):
            o = so[h][c:] + o0[i] + mu_[h]
            mean = head_sum(o, m0) * inv_n
            oc = o - mean
            var = head_sum(oc * oc, m0) * inv_n
            y = oc * lax.rsqrt(var + RW_GN_EPS) * lng_ref[:, sls[h]] + lnb_ref[:, sls[h]]
            y = y + bonus[h][rows, :]
            o_ref[rows, sls[h]] = (y * g_ref[rows, sls[h]].astype(f32)).astype(o_ref.dtype)
    for h in range(npair):
        state_ref[h] = state[h]


def rwkv_mix(p_rw, lw, a, g, mu, k_k, k_a, r_k, ln_g, ln_b):
    bsz, s, _ = p_rw.shape
    nrow = RW_SUB * RW_CHUNK
    spec = pl.BlockSpec((None, nrow, RW_WIDTH), lambda i, j: (i, j, 0))
    vec = pl.BlockSpec((1, RW_WIDTH), lambda i, j: (0, 0))
    return pl.pallas_call(
        _rwkv_kernel,
        grid=(bsz, s // nrow),
        in_specs=[pl.BlockSpec((None, nrow, 3 * RW_WIDTH), lambda i, j: (i, j, 0)), spec, spec, spec,
                  pl.BlockSpec((1, 3 * RW_WIDTH), lambda i, j: (0, 0)), vec, vec, vec, vec, vec],
        out_specs=spec,
        out_shape=jax.ShapeDtypeStruct((bsz, s, RW_WIDTH), bf16),
        scratch_shapes=[pltpu.VMEM((RW_HEADS // 2, LANES, LANES), f32), pltpu.VMEM((1, 3 * RW_WIDTH), f32)],
        compiler_params=_cparams(("parallel", "arbitrary")),
        name="rwkv_mix",
    )(p_rw, lw, a, g, mu, k_k, k_a, r_k, ln_g, ln_b)


def _sgu_kernel(p_ref, lng_ref, lnb_ref, w_ref, bias_ref, o_ref):
    x = p_ref[...].astype(f32)
    g = 0.5 * x * (1.0 + jnp.tanh(0.7978845608028654 * (x + 0.044715 * (x * x * x))))
    u = g[:, :SG_WIDTH]
    v = g[:, SG_WIDTH:]
    mu = jnp.mean(v, axis=-1, keepdims=True)
    vc = v - mu
    var = jnp.mean(vc * vc, axis=-1, keepdims=True)
    vn = (vc * lax.rsqrt(var + 1e-5) * lng_ref[...] + lnb_ref[...]).astype(bf16)
    first = lax.broadcasted_iota(jnp.int32, (SG_CHUNK, LANES), 1) < (LANES // 2)
    for q in range(SG_GROUPS // 2):
        sl = slice(q * LANES, (q + 1) * LANES)
        vq = vn[:, sl]
        lo = jnp.dot(w_ref[2 * q], vq, preferred_element_type=f32)
        hi = jnp.dot(w_ref[2 * q + 1], vq, preferred_element_type=f32)
        mixed = jnp.where(first, lo, hi) + bias_ref[:, sl]
        o_ref[:, sl] = (u[:, sl] * mixed).astype(o_ref.dtype)


def sgu(p_sg, ln_g, ln_b, w_tril, bias_full):
    t = p_sg.shape[0]
    return pl.pallas_call(
        _sgu_kernel,
        grid=(t // SG_CHUNK,),
        in_specs=[pl.BlockSpec((SG_CHUNK, 2 * SG_WIDTH), lambda i: (i, 0)),
                  pl.BlockSpec((1, SG_WIDTH), lambda i: (0, 0)),
                  pl.BlockSpec((1, SG_WIDTH), lambda i: (0, 0)),
                  pl.BlockSpec((SG_GROUPS, SG_CHUNK, SG_CHUNK), lambda i: (0, 0, 0)),
                  pl.BlockSpec((SG_CHUNK, SG_WIDTH), lambda i: (0, 0))],
        out_specs=pl.BlockSpec((SG_CHUNK, SG_WIDTH), lambda i: (i, 0)),
        out_shape=jax.ShapeDtypeStruct((t, SG_WIDTH), bf16),
        compiler_params=_cparams(("parallel",)),
        name="sgu",
    )(p_sg, ln_g, ln_b, w_tril, bias_full)


MLA_TM = 512
ATT_QK_PAD = 256


def _mla_prep_kernel(p_ref, qg_ref, kvg_ref, wq_ref, wkv_ref, cs_ref, sn_ref, q_ref, kv_ref, kr_ref):
    x = p_ref[...]

    def rms(y, g):
        return y * lax.rsqrt(jnp.mean(y * y, axis=-1, keepdims=True) + NORM_EPS) * g

    q = _dot(rms(x[:, :MLA_Q_LORA], qg_ref[...]), wq_ref[...])
    kv_ref[...] = _dot(rms(x[:, MLA_Q_LORA:MLA_Q_LORA + MLA_KV_LORA], kvg_ref[...]), wkv_ref[...]).astype(bf16)
    cs = cs_ref[...]
    sn = sn_ref[...]
    half = lax.broadcasted_iota(jnp.int32, cs.shape, 1) < (MLA_QK_ROPE // 2)

    def rope(y):
        swapped = jnp.where(half, pltpu.roll(y, LANES - MLA_QK_ROPE // 2, 1), pltpu.roll(y, MLA_QK_ROPE // 2, 1))
        return y * cs + swapped * sn

    kr_ref[...] = rope(x[:, MLA_Q_LORA + MLA_KV_LORA:]).astype(bf16)
    for h in range(MLA_HEADS):
        base = h * ATT_QK_PAD
        q_ref[:, base:base + LANES] = q[:, base:base + LANES].astype(bf16)
        q_ref[:, base + LANES:base + 2 * LANES] = rope(q[:, base + LANES:base + 2 * LANES]).astype(bf16)


def mla_prep(p_at, qg, kvg, wq, wkv, cs, sn):
    t = p_at.shape[0]
    full = lambda shape: pl.BlockSpec(shape, lambda i: (0, 0))
    return pl.pallas_call(
        _mla_prep_kernel,
        grid=(t // MLA_TM,),
        in_specs=[pl.BlockSpec((MLA_TM, MLA_COLS_PAD), lambda i: (i, 0)),
                  full((1, MLA_Q_LORA)), full((1, MLA_KV_LORA)),
                  full((MLA_Q_LORA, MLA_HEADS * ATT_QK_PAD)), full((MLA_KV_LORA, MLA_HEADS * 256)),
                  pl.BlockSpec((MLA_TM, LANES), lambda i: (i, 0)), pl.BlockSpec((MLA_TM, LANES), lambda i: (i, 0))],
        out_specs=[pl.BlockSpec((MLA_TM, MLA_HEADS * ATT_QK_PAD), lambda i: (i, 0)),
                   pl.BlockSpec((MLA_TM, MLA_HEADS * 256), lambda i: (i, 0)),
                   pl.BlockSpec((MLA_TM, LANES), lambda i: (i, 0))],
        out_shape=[jax.ShapeDtypeStruct((t, MLA_HEADS * ATT_QK_PAD), bf16),
                   jax.ShapeDtypeStruct((t, MLA_HEADS * 256), bf16),
                   jax.ShapeDtypeStruct((t, LANES), bf16)],
        compiler_params=_cparams(("parallel",)),
        name="mla_prep",
    )(p_at, qg, kvg, wq, wkv, cs, sn)


ATT_TQ = 1024
ATT_TK = 512


def _attn_kernel(q_ref, kn_ref, kr_ref, v_ref, o_ref):
    i = pl.program_id(2)
    hq = ATT_TQ // 2
    qs = [q_ref[:hq, :], q_ref[hq:, :]]

    def keys(j):
        rows = pl.ds(pl.multiple_of(j * ATT_TK, ATT_TK), ATT_TK)
        return jnp.concatenate([kn_ref[rows, :], kr_ref[rows, :]], axis=1), v_ref[rows, :]

    def update(carry, s, vj):
        m_old, l_old, acc = carry
        m_new = jnp.maximum(m_old, jnp.max(s, axis=-1, keepdims=True))
        p = jnp.exp2(s - m_new)
        alpha = jnp.exp2(m_old - m_new)
        l_new = alpha * l_old + jnp.sum(p, axis=-1, keepdims=True)
        acc = alpha * acc + jnp.dot(p.astype(bf16), vj, preferred_element_type=f32)
        return m_new, l_new, acc

    def body(j, carry):
        ka, va = keys(2 * j)
        kb, vb = keys(2 * j + 1)
        sa = [_dot_nt(qs[h], ka) for h in range(2)]
        sb = [_dot_nt(qs[h], kb) for h in range(2)]
        carry = tuple(update(carry[h], sa[h], va) for h in range(2))
        return tuple(update(carry[h], sb[h], vb) for h in range(2))

    init = tuple((jnp.full((hq, 1), -jnp.inf, f32), jnp.zeros((hq, 1), f32), jnp.zeros((hq, MLA_V_DIM), f32))
                 for _ in range(2))
    carry = lax.fori_loop(0, i, body, init)
    rowi = lax.broadcasted_iota(jnp.int32, (hq, ATT_TK), 0)
    coli = lax.broadcasted_iota(jnp.int32, (hq, ATT_TK), 1)
    diag = coli > rowi
    kj, vj = keys(2 * i)
    c0 = update(carry[0], jnp.where(diag, -jnp.inf, _dot_nt(qs[0], kj)), vj)
    c1 = update(carry[1], _dot_nt(qs[1], kj), vj)
    kj, vj = keys(2 * i + 1)
    c1 = update(c1, jnp.where(diag, -jnp.inf, _dot_nt(qs[1], kj)), vj)
    o_ref[:hq, :] = (c0[2] / c0[1]).astype(o_ref.dtype)
    o_ref[hq:, :] = (c1[2] / c1[1]).astype(o_ref.dtype)


def attention(q, kv, kr):
    b, s, _ = q.shape
    return pl.pallas_call(
        _attn_kernel,
        grid=(b, MLA_HEADS, s // ATT_TQ),
        in_specs=[pl.BlockSpec((None, ATT_TQ, ATT_QK_PAD), lambda bi, h, i: (bi, i, h)),
                  pl.BlockSpec((None, s, MLA_QK_NOPE), lambda bi, h, i: (bi, 0, 2 * h)),
                  pl.BlockSpec((None, s, LANES), lambda bi, h, i: (bi, 0, 0)),
                  pl.BlockSpec((None, s, MLA_V_DIM), lambda bi, h, i: (bi, 0, 2 * h + 1))],
        out_specs=pl.BlockSpec((None, ATT_TQ, MLA_V_DIM), lambda bi, h, i: (bi, i, h)),
        out_shape=jax.ShapeDtypeStruct((b, s, MLA_HEADS * MLA_V_DIM), bf16),
        compiler_params=_cparams(("parallel", "parallel", "parallel")),
        name="attention",
    )(q, kv, kr, kv)


OUT_TM = 512


def _merge_kernel(yr_ref, ys_ref, ya_ref, gate_ref, wr_ref, ws_ref, wa_ref, o_ref):
    d = D_MODEL
    acc = jax.nn.sigmoid(gate_ref[:, :d].astype(f32)) * jnp.dot(yr_ref[...], wr_ref[...], preferred_element_type=f32)
    acc += jax.nn.sigmoid(gate_ref[:, d:2 * d].astype(f32)) * jnp.dot(ys_ref[...], ws_ref[...],
                                                                      preferred_element_type=f32)
    acc += jax.nn.sigmoid(gate_ref[:, 2 * d:].astype(f32)) * jnp.dot(ya_ref[...], wa_ref[...],
                                                                     preferred_element_type=f32)
    o_ref[...] = acc.astype(o_ref.dtype)


def merge(y_rw, y_sg, y_at, p_gate, w_rw, w_sg, w_at):
    t = y_rw.shape[0]
    yspec = pl.BlockSpec((OUT_TM, RW_WIDTH), lambda i: (i, 0))
    wspec = pl.BlockSpec((RW_WIDTH, D_MODEL), lambda i: (0, 0), pipeline_mode=pl.Buffered(1))
    return pl.pallas_call(
        _merge_kernel,
        grid=(t // OUT_TM,),
        in_specs=[yspec, yspec, yspec, pl.BlockSpec((OUT_TM, 3 * D_MODEL), lambda i: (i, 0)), wspec, wspec, wspec],
        out_specs=pl.BlockSpec((OUT_TM, D_MODEL), lambda i: (i, 0)),
        out_shape=jax.ShapeDtypeStruct((t, D_MODEL), bf16),
        compiler_params=_cparams(("parallel",)),
        name="merge",
    )(y_rw, y_sg, y_at, p_gate, w_rw, w_sg, w_at)


def _out_kernel(mix_ref, wo_ref, x_ref, gate_ref, ng_ref, scale_ref, shift_ref, xo_ref, h_ref):
    xn = x_ref[...] + gate_ref[...] * jnp.dot(mix_ref[...], wo_ref[...], preferred_element_type=f32)
    xo_ref[...] = xn
    y = xn * lax.rsqrt(jnp.mean(xn * xn, axis=-1, keepdims=True) + NORM_EPS) * ng_ref[...]
    h_ref[...] = (y * (1.0 + scale_ref[...]) + shift_ref[...]).astype(h_ref.dtype)


def out_proj(mix, w_o, x, gate, norm_g, scale, shift, seq):
    t, d = x.shape
    per_b = pl.BlockSpec((None, 1, d), lambda i: (i * OUT_TM // seq, 0, 0))
    rows = pl.BlockSpec((OUT_TM, d), lambda i: (i, 0))
    return pl.pallas_call(
        _out_kernel,
        grid=(t // OUT_TM,),
        in_specs=[rows, pl.BlockSpec((d, d), lambda i: (0, 0), pipeline_mode=pl.Buffered(1)), rows, per_b,
                  pl.BlockSpec((1, d), lambda i: (0, 0)), per_b, per_b],
        out_specs=[rows, rows],
        out_shape=[jax.ShapeDtypeStruct((t, d), f32), jax.ShapeDtypeStruct((t, d), f32)],
        compiler_params=_cparams(("parallel",)),
        name="out_proj",
    )(mix, w_o, x, gate, norm_g, scale, shift)


ROW_GROUP = 8


def _moe_kernel(layer, run_ref, rexp_ref, nrow_ref, nu_ref, nr_ref, tok_ref, h_ref, rw_ref, wg_hbm, wu_hbm, wd_hbm, o_ref,
                wg_f, wu_f, wd_f, wgu_s, wd_s, xbuf, gsem, wsem):
    s = pl.program_id(0)
    nblk = pl.num_programs(0) - 1
    cur = jnp.minimum(s, nblk - 1)
    prv = jnp.maximum(s - 1, 0)
    n_used = nu_ref[0]
    n_runs = nr_ref[0]
    compute = (s >= 1) & (s <= n_used)

    def weight_copies(j):
        e = rexp_ref[j]
        k = j % 2
        return [pltpu.make_async_copy(src.at[layer, e], dst.at[k], wsem.at[k])
                for src, dst in ((wg_hbm, wg_f), (wu_hbm, wu_f), (wd_hbm, wd_f))]

    def fetch(j):
        for cp in weight_copies(j):
            cp.start()

    @pl.when(s == 0)
    def _():
        fetch(0)

        @pl.when(n_runs > 1)
        def _():
            fetch(1)

    @pl.when(compute)
    def _():
        b = prv % 2

        def wait_group(i, carry):
            pltpu.make_async_copy(h_ref.at[pl.ds(0, ROW_GROUP), :], xbuf.at[b, pl.ds(0, ROW_GROUP), :],
                                  gsem.at[b]).wait()
            return carry

        lax.fori_loop(0, nrow_ref[prv] // ROW_GROUP, wait_group, 0)

    @pl.when(s < n_used)
    def _():
        b = s % 2

        def issue_group(i, carry):
            for r in range(ROW_GROUP):
                row = i * ROW_GROUP + r
                pltpu.make_async_copy(h_ref.at[pl.ds(tok_ref[0, row], 1), :], xbuf.at[b, pl.ds(row, 1), :],
                                      gsem.at[b]).start(priority=1)
            return carry

        lax.fori_loop(0, nrow_ref[cur] // ROW_GROUP, issue_group, 0)

    @pl.when((s < n_used) & ((s == 0) | (run_ref[cur] != run_ref[prv])))
    def _():
        j = run_ref[cur]
        k = j % 2
        for cp in weight_copies(j):
            cp.wait()
        wgu_s[k, :, :MOE_D_FF] = wg_f[k].astype(bf16)
        wgu_s[k, :, MOE_D_FF:] = wu_f[k].astype(bf16)
        wd_s[k] = wd_f[k].astype(bf16)

        @pl.when(j + 2 < n_runs)
        def _():
            fetch(j + 2)

    @pl.when(compute)
    def _():
        k = run_ref[prv] % 2
        gathered = lax.broadcasted_iota(jnp.int32, (MOE_BLOCK, 1), 0) < nrow_ref[prv]
        x = jnp.where(gathered, xbuf[prv % 2], 0.0).astype(bf16)
        gu = jnp.dot(x, wgu_s[k], preferred_element_type=f32)
        g = gu[:, :MOE_D_FF]
        hmid = (g * jax.nn.sigmoid(g) * gu[:, MOE_D_FF:] * rw_ref[...]).astype(bf16)
        o_ref[...] = jnp.dot(hmid, wd_s[k], preferred_element_type=f32).astype(o_ref.dtype)

    @pl.when(s > n_used)
    def _():
        o_ref[...] = jnp.zeros_like(o_ref)


def moe_ffn(block_expert, block_rows, n_used, h, row_token, row_weight, w_gate, w_up, w_down, layer):
    rows = row_token.shape[0]
    d = h.shape[1]
    nblk = rows // MOE_BLOCK
    changed = jnp.concatenate([jnp.zeros((1,), jnp.int32),
                               (block_expert[1:] != block_expert[:-1]).astype(jnp.int32)])
    run = jnp.cumsum(changed).astype(jnp.int32)
    run_expert = jnp.zeros((nblk,), jnp.int32).at[run].set(block_expert)
    n_runs = run[jnp.maximum(n_used[0] - 1, 0)].reshape(1) + 1

    def cur_block(s, *_):
        return (jnp.minimum(s, nblk - 1), 0, 0)

    def prev_rows(s, run_, rexp_, nrow_, nu, nr):
        return (jnp.clip(s - 1, 0, nu[0] - 1), 0)

    any_spec = pl.BlockSpec(memory_space=pl.ANY)
    return pl.pallas_call(
        functools.partial(_moe_kernel, layer),
        grid_spec=pltpu.PrefetchScalarGridSpec(
            num_scalar_prefetch=5,
            grid=(nblk + 1,),
            in_specs=[pl.BlockSpec((None, 1, MOE_BLOCK), cur_block, memory_space=pltpu.SMEM),
                      any_spec,
                      pl.BlockSpec((MOE_BLOCK, 1), prev_rows),
                      any_spec, any_spec, any_spec],
            out_specs=pl.BlockSpec((MOE_BLOCK, d), lambda s, *_: (jnp.maximum(s - 1, 0), 0)),
            scratch_shapes=[pltpu.VMEM((2, d, MOE_D_FF), f32), pltpu.VMEM((2, d, MOE_D_FF), f32),
                            pltpu.VMEM((2, MOE_D_FF, d), f32),
                            pltpu.VMEM((2, d, 2 * MOE_D_FF), bf16), pltpu.VMEM((2, MOE_D_FF, d), bf16),
                            pltpu.VMEM((2, MOE_BLOCK, d), f32),
                            pltpu.SemaphoreType.DMA((2,)), pltpu.SemaphoreType.DMA((2,))],
        ),
        out_shape=jax.ShapeDtypeStruct((rows, d), f32),
        compiler_params=_cparams(("arbitrary",)),
        name="moe_ffn",
    )(run, run_expert, block_rows, n_used, n_runs.astype(jnp.int32), row_token.reshape(nblk, 1, MOE_BLOCK), h,
      row_weight, w_gate, w_up, w_down)


def _rms(x, g):
    return x * lax.rsqrt(jnp.mean(x * x, axis=-1, keepdims=True) + NORM_EPS) * g


def _pad_cols(w, width):
    return jnp.pad(w, ((0, 0), (0, width - w.shape[1])))


def _pad_rows(w, height):
    return jnp.pad(w, ((0, height - w.shape[0]), (0, 0)))


def _moe(h, layer, g_w, g_b, e_w, e_b, w_gate, w_up, w_down):
    t, d = h.shape
    w_r = _pad_cols(jnp.concatenate([g_w, e_w], axis=1), LANES)
    logits = matmul(h, w_r, tm=1024, tn=LANES, name="router")
    group_logits = logits[:, :MOE_GROUPS] + g_b
    group = jnp.argmax(group_logits, axis=-1)
    group_w = jnp.take_along_axis(jax.nn.softmax(group_logits, axis=-1), group[:, None], axis=-1)
    exp_logits = (logits[:, MOE_GROUPS:MOE_GROUPS + MOE_EXPERTS] + e_b).reshape(t, MOE_GROUPS, MOE_EPG)
    in_group = jnp.take_along_axis(exp_logits, group[:, None, None], axis=1)[:, 0]
    top_logit, top_idx = lax.top_k(in_group, MOE_TOP_K)
    weights = (group_w * jax.nn.softmax(top_logit, axis=-1)).reshape(-1)
    expert_ids = (group[:, None] * MOE_EPG + top_idx).reshape(-1).astype(jnp.int32)
    n_assign = t * MOE_TOP_K
    n_blocks = -(-n_assign // MOE_BLOCK) + MOE_EXPERTS
    rows = n_blocks * MOE_BLOCK
    onehot = (expert_ids[:, None] == jnp.arange(MOE_EXPERTS, dtype=jnp.int32)[None, :]).astype(jnp.int32)
    csum = jnp.cumsum(onehot, axis=0)
    rank = jnp.take_along_axis(csum, expert_ids[:, None], axis=1)[:, 0] - 1
    counts = csum[-1]
    padded = (counts + MOE_BLOCK - 1) // MOE_BLOCK * MOE_BLOCK
    ends = jnp.cumsum(padded)
    starts = ends - padded
    dest = starts[expert_ids] + rank
    token_ids = jnp.arange(n_assign, dtype=jnp.int32) // MOE_TOP_K
    row_token = jnp.zeros((rows,), jnp.int32).at[dest].set(token_ids)
    row_weight = jnp.zeros((rows,), f32).at[dest].set(weights)
    block_start = jnp.arange(n_blocks, dtype=jnp.int32) * MOE_BLOCK
    block_expert = jnp.minimum(jnp.searchsorted(ends, block_start, side='right'),
                               MOE_EXPERTS - 1).astype(jnp.int32)
    n_used = (ends[-1] // MOE_BLOCK).astype(jnp.int32).reshape(1)
    block_valid = jnp.clip(starts[block_expert] + counts[block_expert] - block_start, 0, MOE_BLOCK)
    block_valid = jnp.where(block_start < ends[-1], block_valid, 0)
    block_rows = ((block_valid + ROW_GROUP - 1) // ROW_GROUP * ROW_GROUP).astype(jnp.int32)
    yb = moe_ffn(block_expert, block_rows, n_used, h, row_token, row_weight[:, None], w_gate, w_up, w_down, layer)
    dest2 = dest.reshape(t, MOE_TOP_K)
    return yb, dest2[:, 0], dest2[:, 1]


def _gather_rows(d0_ref, d1_ref, yb_ref, buf, sem):
    def issue(r, carry):
        pltpu.make_async_copy(yb_ref.at[pl.ds(d0_ref[0, r], 1), :], buf.at[0, pl.ds(r, 1), :], sem.at[0]).start()
        pltpu.make_async_copy(yb_ref.at[pl.ds(d1_ref[0, r], 1), :], buf.at[1, pl.ds(r, 1), :], sem.at[1]).start()
        return carry

    lax.fori_loop(0, OUT_TM, issue, 0, unroll=8)
    for k in range(MOE_TOP_K):
        pltpu.make_async_copy(yb_ref.at[pl.ds(0, OUT_TM), :], buf.at[k], sem.at[k]).wait()
    return buf[0] + buf[1]


def _combine_kernel(d0_ref, d1_ref, yb_ref, x_ref, gate_ref, ng_ref, scale_ref, shift_ref, xo_ref, h_ref,
                    buf, sem):
    xn = x_ref[...] + gate_ref[...] * _gather_rows(d0_ref, d1_ref, yb_ref, buf, sem)
    xo_ref[...] = xn
    y = xn * lax.rsqrt(jnp.mean(xn * xn, axis=-1, keepdims=True) + NORM_EPS) * ng_ref[...]
    h_ref[...] = (y * (1.0 + scale_ref[...]) + shift_ref[...]).astype(h_ref.dtype)


def _final_kernel(d0_ref, d1_ref, yb_ref, x_ref, gate_ref, ng_ref, o_ref, buf, sem):
    xn = x_ref[...] + gate_ref[...] * _gather_rows(d0_ref, d1_ref, yb_ref, buf, sem)
    o_ref[...] = xn * lax.rsqrt(jnp.mean(xn * xn, axis=-1, keepdims=True) + NORM_EPS) * ng_ref[...]


def combine(yb, dest0, dest1, x, gate, norm_g, scale, shift, seq):
    t, d = x.shape
    nb = t // OUT_TM
    per_b = pl.BlockSpec((None, 1, d), lambda i: (i * OUT_TM // seq, 0, 0))
    rows = pl.BlockSpec((OUT_TM, d), lambda i: (i, 0))
    vec = pl.BlockSpec((1, d), lambda i: (0, 0))
    idx = pl.BlockSpec((None, 1, OUT_TM), lambda i: (i, 0, 0), memory_space=pltpu.SMEM)
    any_spec = pl.BlockSpec(memory_space=pl.ANY)
    scratch = [pltpu.VMEM((MOE_TOP_K, OUT_TM, d), f32), pltpu.SemaphoreType.DMA((MOE_TOP_K,))]
    d0 = dest0.reshape(nb, 1, OUT_TM)
    d1 = dest1.reshape(nb, 1, OUT_TM)
    if scale is None:
        return pl.pallas_call(
            _final_kernel, grid=(nb,), in_specs=[idx, idx, any_spec, rows, per_b, vec], out_specs=rows,
            out_shape=jax.ShapeDtypeStruct((t, d), f32), scratch_shapes=scratch,
            compiler_params=_cparams(("arbitrary",)), name="final_norm")(d0, d1, yb, x, gate, norm_g)
    return pl.pallas_call(
        _combine_kernel, grid=(nb,), in_specs=[idx, idx, any_spec, rows, per_b, vec, per_b, per_b],
        out_specs=[rows, rows],
        out_shape=[jax.ShapeDtypeStruct((t, d), f32), jax.ShapeDtypeStruct((t, d), bf16)],
        scratch_shapes=scratch, compiler_params=_cparams(("arbitrary",)),
        name="combine")(d0, d1, yb, x, gate, norm_g, scale, shift)


def kernel(x, c, positions, ada_w, ada_b, norm1_g, norm2_g, final_g, w_in, rw_mu, rw_w0, rw_w2, rw_a0, rw_a2, rw_g2, rw_k_k, rw_k_a, rw_r_k, rw_ln_g, rw_ln_b, sg_ln_g, sg_ln_b, sg_w, sg_b, mla_q_norm_g, mla_w_uq, mla_kv_norm_g, mla_w_ukv, p_rwkv, p_sgu, p_mla, w_o, router_g_w, router_g_b, router_e_w, router_e_b, exp_w_gate, exp_w_up, exp_w_down):
    bsz, seq, d = x.shape
    t = bsz * seq
    half = MLA_QK_ROPE // 2
    inv_freq = ROPE_BASE ** (-jnp.arange(half, dtype=f32) / half)
    ang = (positions.astype(f32)[..., None] * inv_freq).reshape(t, half)
    cos, sin = jnp.cos(ang), jnp.sin(ang)
    zpad = jnp.zeros((t, LANES - MLA_QK_ROPE), f32)
    rope_cs = jnp.concatenate([cos, cos, zpad], axis=1)
    rope_sn = jnp.concatenate([-sin, sin, zpad], axis=1)
    c_act = jnp.pad(jax.nn.silu(c), ((0, 8 - bsz), (0, 0)))
    x = x.reshape(t, d)
    row = lambda vec: vec[None, :]

    def per_batch(vec):
        return vec[:, None, :]

    mods = []
    for l in range(DEPTH):
        mod = matmul(c_act, ada_w, tm=8, tn=1024, layer=l, name="ada")[:bsz] + ada_b[l]
        mods.append(jnp.split(mod, 6, axis=-1))

    shift1, scale1 = mods[0][0], mods[0][1]
    h = ((_rms(x, norm1_g[0]).reshape(bsz, seq, d) * (1.0 + scale1[:, None, :]) + shift1[:, None, :])
         .reshape(t, d).astype(bf16))
    for l in range(DEPTH):
        _, _, gate1, shift2, scale2, gate2 = mods[l]
        wl = w_in[l]
        w_rw = jnp.concatenate([wl[:, :3072], _pad_cols(wl[:, 3072:3168], LANES),
                                _pad_cols(wl[:, 3168:3264], LANES), wl[:, 3264:3520]], axis=1).astype(bf16)
        mul = rw_mu[l]
        mu = jnp.concatenate([mul[:3072], jnp.pad(mul[3072:3168], (0, 32)),
                              jnp.pad(mul[3168:3264], (0, 32)), mul[3264:3520]])
        w_sg = wl[:, 3520:5568].astype(bf16)
        w_at = _pad_cols(wl[:, 5568:6400], MLA_COLS_PAD).astype(bf16)
        w_gt = wl[:, 6400:].astype(bf16)
        p_rw = matmul(h, w_rw, tm=1024, tn=896, name="in_rw").reshape(bsz, seq, RW_COLS_PAD)
        p_sg = matmul(h, w_sg, tm=1024, tn=1024, out_dtype=bf16, name="in_sg")
        p_at = matmul(h, w_at, tm=1024, tn=MLA_COLS_PAD, name="in_at")
        p_gate = matmul(h, w_gt, tm=1024, tn=1024, out_dtype=bf16, name="in_gate")
        lw, a, g = rw_lora(p_rw, row(mu[3 * RW_WIDTH:]), row(rw_w0[l]), row(rw_a0[l]),
                           _pad_rows(rw_w2[l], LANES).astype(bf16), _pad_rows(rw_a2[l], LANES).astype(bf16),
                           rw_g2[l].astype(bf16))
        y_rw = rwkv_mix(p_rw, lw, a, g, row(mu[:3 * RW_WIDTH]), row(rw_k_k[l]), row(rw_k_a[l]),
                        row(rw_r_k[l].reshape(-1)), row(rw_ln_g[l]), row(rw_ln_b[l])).reshape(t, RW_WIDTH)
        bias_full = jnp.repeat(sg_b[l].T, RW_HEAD_DIM, axis=1)
        y_sg = sgu(p_sg, row(sg_ln_g[l]), row(sg_ln_b[l]), jnp.tril(sg_w[l]).astype(bf16), bias_full)
        wq = mla_w_uq[l].reshape(MLA_Q_LORA, MLA_HEADS, MLA_QK_DIM) * (MLA_QK_DIM ** -0.5 * LOG2_E)
        wq = jnp.pad(wq, ((0, 0), (0, 0), (0, ATT_QK_PAD - MLA_QK_DIM))).reshape(MLA_Q_LORA, -1).astype(bf16)
        q, kv, kr = mla_prep(p_at, row(mla_q_norm_g[l]), row(mla_kv_norm_g[l]), wq, mla_w_ukv[l].astype(bf16),
                             rope_cs, rope_sn)
        y_at = attention(q.reshape(bsz, seq, -1), kv.reshape(bsz, seq, -1),
                         kr.reshape(bsz, seq, LANES)).reshape(t, -1)
        mix = merge(y_rw, y_sg, y_at, p_gate, p_rwkv[l].astype(bf16), p_sgu[l].astype(bf16), p_mla[l].astype(bf16))
        x, h2 = out_proj(mix, w_o[l].astype(bf16), x, per_batch(gate1), row(norm2_g[l]), per_batch(scale2),
                         per_batch(shift2), seq)
        yb, dest0, dest1 = _moe(h2, l, router_g_w[l], router_g_b[l], router_e_w[l], router_e_b[l],
                                exp_w_gate, exp_w_up, exp_w_down)
        if l + 1 < DEPTH:
            x, h = combine(yb, dest0, dest1, x, per_batch(gate2), row(norm1_g[l + 1]), per_batch(mods[l + 1][1]),
                           per_batch(mods[l + 1][0]), seq)
    return combine(yb, dest0, dest1, x, per_batch(gate2), row(final_g), None, None, seq).reshape(bsz, seq, d)
```

```python
import functools

import jax
import jax.numpy as jnp
from jax import lax
from jax.experimental import pallas as pl
from jax.experimental.pallas import tpu as pltpu

f32 = jnp.float32
bf16 = jnp.bfloat16

D_MODEL = 2048
DEPTH = 4
RW_HEADS = 16
RW_HEAD_DIM = 64
RW_WIDTH = 1024
RW_LORA_COLS = 512
RW_COLS_PAD = 3 * RW_WIDTH + RW_LORA_COLS
RW_GN_EPS = 64e-5
SG_CHUNK = 128
SG_GROUPS = 16
SG_WIDTH = 1024
MLA_HEADS = 8
MLA_Q_LORA = 512
MLA_KV_LORA = 256
MLA_QK_NOPE = 128
MLA_QK_ROPE = 64
MLA_QK_DIM = 192
MLA_V_DIM = 128
MLA_COLS_PAD = 896
ROPE_BASE = 10000.0
MOE_GROUPS = 8
MOE_EPG = 8
MOE_EXPERTS = 64
MOE_TOP_K = 2
MOE_D_FF = 384
MOE_BLOCK = 256
NORM_EPS = 1e-6
LOG2_E = 1.4426950408889634

LANES = 128
RW_CHUNK = 64
RW_SUB = 4
VMEM_LIMIT = 48 * 1024 * 1024


def _cparams(sem):
    return pltpu.CompilerParams(dimension_semantics=sem, vmem_limit_bytes=VMEM_LIMIT)


def _dot(a, b):
    return jnp.dot(a.astype(bf16), b.astype(bf16), preferred_element_type=f32)


def _dot_nt(a, b):
    return lax.dot_general(a.astype(bf16), b.astype(bf16), (((1,), (1,)), ((), ())),
                           preferred_element_type=f32)


def _dot_tn(a, b):
    return lax.dot_general(a.astype(bf16), b.astype(bf16), (((0,), (0,)), ((), ())),
                           preferred_element_type=f32)


def _mm_kernel(a_ref, w_ref, o_ref):
    o_ref[...] = _dot(a_ref[...], w_ref[...]).astype(o_ref.dtype)


def matmul(a, w, *, tm, tn, out_dtype=f32, layer=None, name="mm"):
    m, k = a.shape
    n = w.shape[-1]
    assert m % tm == 0 and n % tn == 0, (m, tm, n, tn)
    if layer is None:
        w_spec = pl.BlockSpec((k, tn), lambda j, i: (0, j))
    else:
        w_spec = pl.BlockSpec((None, k, tn), lambda j, i: (layer, 0, j))
    return pl.pallas_call(
        _mm_kernel,
        grid=(n // tn, m // tm),
        in_specs=[pl.BlockSpec((tm, k), lambda j, i: (i, 0)), w_spec],
        out_specs=pl.BlockSpec((tm, tn), lambda j, i: (i, j)),
        out_shape=jax.ShapeDtypeStruct((m, n), out_dtype),
        compiler_params=_cparams(("parallel", "parallel")),
        name=name,
    )(a, w)


RW_LORA_TM = 1024


def _softplus(z):
    return jnp.maximum(z, 0.0) + jnp.log(1.0 + jnp.exp(-jnp.abs(z)))


def _rw_lora_kernel(p_ref, mu_ref, w0_ref, a0_ref, ww_ref, wa_ref, wg_ref, lw_ref, a_ref, g_ref, prev_ref):
    @pl.when(pl.program_id(1) == 0)
    def _():
        prev_ref[...] = jnp.zeros_like(prev_ref)

    x = p_ref[...]
    rowi = lax.broadcasted_iota(jnp.int32, x.shape, 0)
    xprev = jnp.where(rowi == 0, prev_ref[...], pltpu.roll(x, 1, 0))
    prev_ref[...] = x[RW_LORA_TM - 1:RW_LORA_TM, :]
    xs = x + mu_ref[...] * (xprev - x)
    dec = _dot(jnp.tanh(xs[:, :LANES]), ww_ref[...])
    log_w = -_softplus(-(w0_ref[...] + dec)) - 0.5
    lw_ref[...] = -jnp.exp(log_w)
    a_ref[...] = jax.nn.sigmoid(a0_ref[...] + _dot(xs[:, LANES:2 * LANES], wa_ref[...]))
    g_ref[...] = _dot(jax.nn.sigmoid(xs[:, 2 * LANES:]), wg_ref[...]).astype(g_ref.dtype)


def rw_lora(p_rw, mu_lora, w0, a0, w_w2, w_a2, w_g2):
    b, s, _ = p_rw.shape
    cb = 3 * RW_WIDTH // RW_LORA_COLS
    vec = pl.BlockSpec((1, RW_WIDTH), lambda i, j: (0, 0))
    out = pl.BlockSpec((None, RW_LORA_TM, RW_WIDTH), lambda i, j: (i, j, 0))
    return pl.pallas_call(
        _rw_lora_kernel,
        grid=(b, s // RW_LORA_TM),
        in_specs=[pl.BlockSpec((None, RW_LORA_TM, RW_LORA_COLS), lambda i, j: (i, j, cb)),
                  pl.BlockSpec((1, RW_LORA_COLS), lambda i, j: (0, 0)), vec, vec,
                  pl.BlockSpec((LANES, RW_WIDTH), lambda i, j: (0, 0)),
                  pl.BlockSpec((LANES, RW_WIDTH), lambda i, j: (0, 0)),
                  pl.BlockSpec((2 * LANES, RW_WIDTH), lambda i, j: (0, 0))],
        out_specs=[out, out, out],
        out_shape=[jax.ShapeDtypeStruct((b, s, RW_WIDTH), f32), jax.ShapeDtypeStruct((b, s, RW_WIDTH), f32),
                   jax.ShapeDtypeStruct((b, s, RW_WIDTH), bf16)],
        scratch_shapes=[pltpu.VMEM((1, RW_LORA_COLS), f32)],
        compiler_params=_cparams(("parallel", "arbitrary")),
        name="rw_lora",
    )(p_rw, mu_lora, w0, a0, w_w2, w_a2, w_g2)


def _rwkv_kernel(p_ref, lw_ref, a_ref, g_ref, mu_ref, kk_ref, ka_ref, rk_ref, lng_ref, lnb_ref,
                 o_ref, state_ref, prev_ref):
    c = RW_CHUNK
    n = 2 * c
    nrow = RW_SUB * c
    pairs = range(RW_HEADS // 2)
    units = [(q, p) for q in range(RW_SUB) for p in pairs]

    @pl.when(pl.program_id(1) == 0)
    def _():
        state_ref[...] = jnp.zeros_like(state_ref)
        prev_ref[...] = jnp.zeros_like(prev_ref)

    row = lax.broadcasted_iota(jnp.int32, (n, n), 0)
    col = lax.broadcasted_iota(jnp.int32, (n, n), 1)
    same = (row >> 6) == (col >> 6)
    rpos = row & (c - 1)
    cpos = col & (c - 1)
    strict = same & (cpos < rpos)
    incl = same & (cpos <= rpos)
    eye = (row == col).astype(f32)
    blk16 = (row >> 4) == (col >> 4)
    blk32 = (row >> 5) == (col >> 5)
    low32 = blk32 & jnp.logical_not(blk16)
    low64 = same & jnp.logical_not(blk32)
    tr = lax.broadcasted_iota(jnp.int32, (c, c), 0)
    tc = lax.broadcasted_iota(jnp.int32, (c, c), 1)
    tril1 = (tc <= tr).astype(bf16)
    m0 = lax.broadcasted_iota(jnp.int32, (c, n), 1) < c
    m0f = lax.broadcasted_iota(jnp.int32, (nrow, n), 1) < c
    row0 = lax.broadcasted_iota(jnp.int32, (nrow, n), 0) == 0

    def expand(y):
        return jnp.concatenate([jnp.where(m0, y, 0.0), jnp.where(m0, 0.0, y)], axis=0)

    def collapse(y):
        return y[:c] + y[c:]

    def head_sum(y, mask):
        lo = jnp.sum(jnp.where(mask, y, 0.0), axis=-1, keepdims=True)
        hi = jnp.sum(jnp.where(mask, 0.0, y), axis=-1, keepdims=True)
        return jnp.where(mask, lo, hi)

    def shifted(base, p):
        sl = slice(base + p * LANES, base + (p + 1) * LANES)
        x = p_ref[:, sl]
        xprev = jnp.where(row0, prev_ref[:, sl], pltpu.roll(x, 1, 0))
        prev_ref[:, sl] = x[nrow - 1:nrow, :]
        return x + mu_ref[:, sl] * (xprev - x)

    sls = [slice(p * LANES, (p + 1) * LANES) for p in pairs]
    rf = [shifted(0, p) for p in pairs]
    k0f = [shifted(RW_WIDTH, p) for p in pairs]
    vf = [shifted(2 * RW_WIDTH, p) for p in pairs]
    af = [a_ref[:, s] for s in sls]
    kkrf = [k0f[p] * kk_ref[:, sls[p]] for p in pairs]
    kkf = [kkrf[p] * lax.rsqrt(jnp.maximum(head_sum(kkrf[p] * kkrf[p], m0f), 1e-24)) for p in pairs]
    kf = [k0f[p] * (1.0 + (af[p] - 1.0) * ka_ref[:, sls[p]]) for p in pairs]
    bonus = [head_sum(rf[p] * kf[p] * rk_ref[:, sls[p]], m0f) * vf[p] for p in pairs]

    def chunk(xs, q, p):
        return xs[p][q * c:(q + 1) * c, :]

    r = [chunk(rf, q, p) for q, p in units]
    v = [chunk(vf, q, p) for q, p in units]
    a = [chunk(af, q, p) for q, p in units]
    kk = [chunk(kkf, q, p) for q, p in units]
    k = [chunk(kf, q, p) for q, p in units]
    lw = [lw_ref[q * c:(q + 1) * c, sls[p]] for q, p in units]
    pairs = range(len(units))

    def cumsum(x):
        x1 = x.astype(bf16)
        res = x - x1.astype(f32)
        x2 = res.astype(bf16)
        x3 = (res - x2.astype(f32)).astype(bf16)
        return (jnp.dot(tril1, x1, preferred_element_type=f32) + jnp.dot(tril1, x2, preferred_element_type=f32)
                + jnp.dot(tril1, x3, preferred_element_type=f32))

    cum = [cumsum(lw[p]) for p in pairs]
    ctot = [cum[p][c - 1:c, :] for p in pairs]
    ginv = [jnp.exp(-cum[p]) for p in pairs]
    g2 = [jnp.exp(ctot[p] - cum[p]) for p in pairs]
    b = [kk[p] * a[p] for p in pairs]
    at = [-kk[p] * jnp.exp(cum[p] - lw[p]) for p in pairs]
    rt = [r[p] * jnp.exp(cum[p]) for p in pairs]
    bt = [b[p] * ginv[p] for p in pairs]
    kt = [k[p] * ginv[p] for p in pairs]
    pm = [_dot_nt(jnp.concatenate([expand(at[p]), expand(rt[p])], axis=0),
                  jnp.concatenate([expand(bt[p]), expand(kt[p])], axis=0)) for p in pairs]
    a_ab = [jnp.where(strict, pm[p][:n, :n], 0.0) for p in pairs]
    a_ak = [jnp.where(strict, pm[p][:n, n:], 0.0).astype(bf16) for p in pairs]
    m_rb = [jnp.where(incl, pm[p][n:, :n], 0.0).astype(bf16) for p in pairs]
    m_rk = [jnp.where(incl, pm[p][n:, n:], 0.0).astype(bf16) for p in pairs]
    ev = [expand(v[p]).astype(bf16) for p in pairs]
    akv = [_dot(a_ak[p], ev[p]) for p in pairs]
    o0 = [collapse(_dot(m_rk[p], ev[p])) for p in pairs]
    kv = [jnp.where(same, _dot_tn(v[p], k[p] * g2[p]), 0.0) for p in pairs]
    a0 = [jnp.where(blk16, a_ab[p], 0.0) for p in pairs]
    a2 = [_dot(a0[p], a0[p]) for p in pairs]
    p2 = [eye + a0[p] + a2[p] + _dot(a0[p], a2[p]) for p in pairs]
    a4 = [_dot(a2[p], a2[p]) for p in pairs]
    p3 = [p2[p] + _dot(p2[p], a4[p]) for p in pairs]
    a8 = [_dot(a4[p], a4[p]) for p in pairs]
    d = [p3[p] + _dot(p3[p], a8[p]) for p in pairs]
    x1 = [_dot(jnp.where(low32, a_ab[p], 0.0), d[p]) for p in pairs]
    d = [d[p] + _dot(d[p], x1[p]) for p in pairs]
    x2 = [_dot(jnp.where(low64, a_ab[p], 0.0), d[p]) for p in pairs]
    t = [(d[p] + _dot(d[p], x2[p])).astype(bf16) for p in pairs]
    tx = [_dot(t[p], jnp.concatenate([expand(at[p]), akv[p]], axis=1)) for p in pairs]
    at_hat = [collapse(tx[p][:, :n]) for p in pairs]
    u0 = [collapse(tx[p][:, n:]) for p in pairs]
    inv_n = 1.0 / RW_HEAD_DIM
    npair = RW_HEADS // 2
    state = [state_ref[h] for h in range(npair)]
    for q in range(RW_SUB):
        ids = [q * npair + h for h in range(npair)]
        so = [_dot_nt(jnp.concatenate([at_hat[i], rt[i]], axis=0), state[h]) for h, i in enumerate(ids)]
        u = [so[h][:c] + u0[i] for h, i in enumerate(ids)]
        su = [jnp.where(same, _dot_tn(u[h], b[i] * g2[i]), 0.0) for h, i in enumerate(ids)]
        mu_ = [collapse(_dot(m_rb[i], expand(u[h]))) for h, i in enumerate(ids)]
        state = [state[h] * jnp.exp(ctot[i]) + kv[i] + su[h] for h, i in enumerate(ids)]
        rows = slice(q * c, (q + 1) * c)
        for h, i in enumerate(ids):
            o = so[h][c:] + o0[i] + mu_[h]
            mean = head_sum(o, m0) * inv_n
            oc = o - mean
            var = head_sum(oc * oc, m0) * inv_n
            y = oc * lax.rsqrt(var + RW_GN_EPS) * lng_ref[:, sls[h]] + lnb_ref[:, sls[h]]
            y = y + bonus[h][rows, :]
            o_ref[rows, sls[h]] = (y * g_ref[rows, sls[h]].astype(f32)).astype(o_ref.dtype)
    for h in range(npair):
        state_ref[h] = state[h]


def rwkv_mix(p_rw, lw, a, g, mu, k_k, k_a, r_k, ln_g, ln_b):
    bsz, s, _ = p_rw.shape
    nrow = RW_SUB * RW_CHUNK
    spec = pl.BlockSpec((None, nrow, RW_WIDTH), lambda i, j: (i, j, 0))
    vec = pl.BlockSpec((1, RW_WIDTH), lambda i, j: (0, 0))
    return pl.pallas_call(
        _rwkv_kernel,
        grid=(bsz, s // nrow),
        in_specs=[pl.BlockSpec((None, nrow, 3 * RW_WIDTH), lambda i, j: (i, j, 0)), spec, spec, spec,
                  pl.BlockSpec((1, 3 * RW_WIDTH), lambda i, j: (0, 0)), vec, vec, vec, vec, vec],
        out_specs=spec,
        out_shape=jax.ShapeDtypeStruct((bsz, s, RW_WIDTH), bf16),
        scratch_shapes=[pltpu.VMEM((RW_HEADS // 2, LANES, LANES), f32), pltpu.VMEM((1, 3 * RW_WIDTH), f32)],
        compiler_params=_cparams(("parallel", "arbitrary")),
        name="rwkv_mix",
    )(p_rw, lw, a, g, mu, k_k, k_a, r_k, ln_g, ln_b)


def _sgu_kernel(p_ref, lng_ref, lnb_ref, w_ref, bias_ref, o_ref):
    x = p_ref[...].astype(f32)
    g = 0.5 * x * (1.0 + jnp.tanh(0.7978845608028654 * (x + 0.044715 * (x * x * x))))
    u = g[:, :SG_WIDTH]
    v = g[:, SG_WIDTH:]
    mu = jnp.mean(v, axis=-1, keepdims=True)
    vc = v - mu
    var = jnp.mean(vc * vc, axis=-1, keepdims=True)
    vn = (vc * lax.rsqrt(var + 1e-5) * lng_ref[...] + lnb_ref[...]).astype(bf16)
    first = lax.broadcasted_iota(jnp.int32, (SG_CHUNK, LANES), 1) < (LANES // 2)
    for q in range(SG_GROUPS // 2):
        sl = slice(q * LANES, (q + 1) * LANES)
        vq = vn[:, sl]
        lo = jnp.dot(w_ref[2 * q], vq, preferred_element_type=f32)
        hi = jnp.dot(w_ref[2 * q + 1], vq, preferred_element_type=f32)
        mixed = jnp.where(first, lo, hi) + bias_ref[:, sl]
        o_ref[:, sl] = (u[:, sl] * mixed).astype(o_ref.dtype)


def sgu(p_sg, ln_g, ln_b, w_tril, bias_full):
    t = p_sg.shape[0]
    return pl.pallas_call(
        _sgu_kernel,
        grid=(t // SG_CHUNK,),
        in_specs=[pl.BlockSpec((SG_CHUNK, 2 * SG_WIDTH), lambda i: (i, 0)),
                  pl.BlockSpec((1, SG_WIDTH), lambda i: (0, 0)),
                  pl.BlockSpec((1, SG_WIDTH), lambda i: (0, 0)),
                  pl.BlockSpec((SG_GROUPS, SG_CHUNK, SG_CHUNK), lambda i: (0, 0, 0)),
                  pl.BlockSpec((SG_CHUNK, SG_WIDTH), lambda i: (0, 0))],
        out_specs=pl.BlockSpec((SG_CHUNK, SG_WIDTH), lambda i: (i, 0)),
        out_shape=jax.ShapeDtypeStruct((t, SG_WIDTH), bf16),
        compiler_params=_cparams(("parallel",)),
        name="sgu",
    )(p_sg, ln_g, ln_b, w_tril, bias_full)


MLA_TM = 512
ATT_QK_PAD = 256


def _mla_prep_kernel(p_ref, qg_ref, kvg_ref, wq_ref, wkv_ref, cs_ref, sn_ref, q_ref, kv_ref, kr_ref):
    x = p_ref[...]

    def rms(y, g):
        return y * lax.rsqrt(jnp.mean(y * y, axis=-1, keepdims=True) + NORM_EPS) * g

    q = _dot(rms(x[:, :MLA_Q_LORA], qg_ref[...]), wq_ref[...])
    kv_ref[...] = _dot(rms(x[:, MLA_Q_LORA:MLA_Q_LORA + MLA_KV_LORA], kvg_ref[...]), wkv_ref[...]).astype(bf16)
    cs = cs_ref[...]
    sn = sn_ref[...]
    half = lax.broadcasted_iota(jnp.int32, cs.shape, 1) < (MLA_QK_ROPE // 2)

    def rope(y):
        swapped = jnp.where(half, pltpu.roll(y, LANES - MLA_QK_ROPE // 2, 1), pltpu.roll(y, MLA_QK_ROPE // 2, 1))
        return y * cs + swapped * sn

    kr_ref[...] = rope(x[:, MLA_Q_LORA + MLA_KV_LORA:]).astype(bf16)
    for h in range(MLA_HEADS):
        base = h * ATT_QK_PAD
        q_ref[:, base:base + LANES] = q[:, base:base + LANES].astype(bf16)
        q_ref[:, base + LANES:base + 2 * LANES] = rope(q[:, base + LANES:base + 2 * LANES]).astype(bf16)


def mla_prep(p_at, qg, kvg, wq, wkv, cs, sn):
    t = p_at.shape[0]
    full = lambda shape: pl.BlockSpec(shape, lambda i: (0, 0))
    return pl.pallas_call(
        _mla_prep_kernel,
        grid=(t // MLA_TM,),
        in_specs=[pl.BlockSpec((MLA_TM, MLA_COLS_PAD), lambda i: (i, 0)),
                  full((1, MLA_Q_LORA)), full((1, MLA_KV_LORA)),
                  full((MLA_Q_LORA, MLA_HEADS * ATT_QK_PAD)), full((MLA_KV_LORA, MLA_HEADS * 256)),
                  pl.BlockSpec((MLA_TM, LANES), lambda i: (i, 0)), pl.BlockSpec((MLA_TM, LANES), lambda i: (i, 0))],
        out_specs=[pl.BlockSpec((MLA_TM, MLA_HEADS * ATT_QK_PAD), lambda i: (i, 0)),
                   pl.BlockSpec((MLA_TM, MLA_HEADS * 256), lambda i: (i, 0)),
                   pl.BlockSpec((MLA_TM, LANES), lambda i: (i, 0))],
        out_shape=[jax.ShapeDtypeStruct((t, MLA_HEADS * ATT_QK_PAD), bf16),
                   jax.ShapeDtypeStruct((t, MLA_HEADS * 256), bf16),
                   jax.ShapeDtypeStruct((t, LANES), bf16)],
        compiler_params=_cparams(("parallel",)),
        name="mla_prep",
    )(p_at, qg, kvg, wq, wkv, cs, sn)


ATT_TQ = 1024
ATT_TK = 512


def _attn_kernel(q_ref, kn_ref, kr_ref, v_ref, o_ref):
    i = pl.program_id(2)
    hq = ATT_TQ // 2
    qs = [q_ref[:hq, :], q_ref[hq:, :]]

    def keys(j):
        rows = pl.ds(pl.multiple_of(j * ATT_TK, ATT_TK), ATT_TK)
        return jnp.concatenate([kn_ref[rows, :], kr_ref[rows, :]], axis=1), v_ref[rows, :]

    def update(carry, s, vj):
        m_old, l_old, acc = carry
        m_new = jnp.maximum(m_old, jnp.max(s, axis=-1, keepdims=True))
        p = jnp.exp2(s - m_new)
        alpha = jnp.exp2(m_old - m_new)
        l_new = alpha * l_old + jnp.sum(p, axis=-1, keepdims=True)
        acc = alpha * acc + jnp.dot(p.astype(bf16), vj, preferred_element_type=f32)
        return m_new, l_new, acc

    def body(j, carry):
        ka, va = keys(2 * j)
        kb, vb = keys(2 * j + 1)
        sa = [_dot_nt(qs[h], ka) for h in range(2)]
        sb = [_dot_nt(qs[h], kb) for h in range(2)]
        carry = tuple(update(carry[h], sa[h], va) for h in range(2))
        return tuple(update(carry[h], sb[h], vb) for h in range(2))

    init = tuple((jnp.full((hq, 1), -jnp.inf, f32), jnp.zeros((hq, 1), f32), jnp.zeros((hq, MLA_V_DIM), f32))
                 for _ in range(2))
    carry = lax.fori_loop(0, i, body, init)
    rowi = lax.broadcasted_iota(jnp.int32, (hq, ATT_TK), 0)
    coli = lax.broadcasted_iota(jnp.int32, (hq, ATT_TK), 1)
    diag = coli > rowi
    kj, vj = keys(2 * i)
    c0 = update(carry[0], jnp.where(diag, -jnp.inf, _dot_nt(qs[0], kj)), vj)
    c1 = update(carry[1], _dot_nt(qs[1], kj), vj)
    kj, vj = keys(2 * i + 1)
    c1 = update(c1, jnp.where(diag, -jnp.inf, _dot_nt(qs[1], kj)), vj)
    o_ref[:hq, :] = (c0[2] / c0[1]).astype(o_ref.dtype)
    o_ref[hq:, :] = (c1[2] / c1[1]).astype(o_ref.dtype)


def attention(q, kv, kr):
    b, s, _ = q.shape
    return pl.pallas_call(
        _attn_kernel,
        grid=(b, MLA_HEADS, s // ATT_TQ),
        in_specs=[pl.BlockSpec((None, ATT_TQ, ATT_QK_PAD), lambda bi, h, i: (bi, i, h)),
                  pl.BlockSpec((None, s, MLA_QK_NOPE), lambda bi, h, i: (bi, 0, 2 * h)),
                  pl.BlockSpec((None, s, LANES), lambda bi, h, i: (bi, 0, 0)),
                  pl.BlockSpec((None, s, MLA_V_DIM), lambda bi, h, i: (bi, 0, 2 * h + 1))],
        out_specs=pl.BlockSpec((None, ATT_TQ, MLA_V_DIM), lambda bi, h, i: (bi, i, h)),
        out_shape=jax.ShapeDtypeStruct((b, s, MLA_HEADS * MLA_V_DIM), bf16),
        compiler_params=_cparams(("parallel", "parallel", "parallel")),
        name="attention",
    )(q, kv, kr, kv)


OUT_TM = 512


def _merge_kernel(yr_ref, ys_ref, ya_ref, gate_ref, wr_ref, ws_ref, wa_ref, o_ref):
    d = D_MODEL
    acc = jax.nn.sigmoid(gate_ref[:, :d].astype(f32)) * jnp.dot(yr_ref[...], wr_ref[...], preferred_element_type=f32)
    acc += jax.nn.sigmoid(gate_ref[:, d:2 * d].astype(f32)) * jnp.dot(ys_ref[...], ws_ref[...],
                                                                      preferred_element_type=f32)
    acc += jax.nn.sigmoid(gate_ref[:, 2 * d:].astype(f32)) * jnp.dot(ya_ref[...], wa_ref[...],
                                                                     preferred_element_type=f32)
    o_ref[...] = acc.astype(o_ref.dtype)


def merge(y_rw, y_sg, y_at, p_gate, w_rw, w_sg, w_at):
    t = y_rw.shape[0]
    yspec = pl.BlockSpec((OUT_TM, RW_WIDTH), lambda i: (i, 0))
    wspec = pl.BlockSpec((RW_WIDTH, D_MODEL), lambda i: (0, 0), pipeline_mode=pl.Buffered(1))
    return pl.pallas_call(
        _merge_kernel,
        grid=(t // OUT_TM,),
        in_specs=[yspec, yspec, yspec, pl.BlockSpec((OUT_TM, 3 * D_MODEL), lambda i: (i, 0)), wspec, wspec, wspec],
        out_specs=pl.BlockSpec((OUT_TM, D_MODEL), lambda i: (i, 0)),
        out_shape=jax.ShapeDtypeStruct((t, D_MODEL), bf16),
        compiler_params=_cparams(("parallel",)),
        name="merge",
    )(y_rw, y_sg, y_at, p_gate, w_rw, w_sg, w_at)


def _out_kernel(mix_ref, wo_ref, x_ref, gate_ref, ng_ref, scale_ref, shift_ref, xo_ref, h_ref):
    xn = x_ref[...] + gate_ref[...] * jnp.dot(mix_ref[...], wo_ref[...], preferred_element_type=f32)
    xo_ref[...] = xn
    y = xn * lax.rsqrt(jnp.mean(xn * xn, axis=-1, keepdims=True) + NORM_EPS) * ng_ref[...]
    h_ref[...] = (y * (1.0 + scale_ref[...]) + shift_ref[...]).astype(h_ref.dtype)


def out_proj(mix, w_o, x, gate, norm_g, scale, shift, seq):
    t, d = x.shape
    per_b = pl.BlockSpec((None, 1, d), lambda i: (i * OUT_TM // seq, 0, 0))
    rows = pl.BlockSpec((OUT_TM, d), lambda i: (i, 0))
    return pl.pallas_call(
        _out_kernel,
        grid=(t // OUT_TM,),
        in_specs=[rows, pl.BlockSpec((d, d), lambda i: (0, 0), pipeline_mode=pl.Buffered(1)), rows, per_b,
                  pl.BlockSpec((1, d), lambda i: (0, 0)), per_b, per_b],
        out_specs=[rows, rows],
        out_shape=[jax.ShapeDtypeStruct((t, d), f32), jax.ShapeDtypeStruct((t, d), f32)],
        compiler_params=_cparams(("parallel",)),
        name="out_proj",
    )(mix, w_o, x, gate, norm_g, scale, shift)


ROW_GROUP = 8


def _moe_kernel(layer, run_ref, rexp_ref, nrow_ref, nu_ref, nr_ref, tok_ref, h_ref, wg_hbm, wu_hbm, wd_hbm, o_ref,
                wg_f, wu_f, wd_f, wgu_s, wd_s, xbuf, gsem, wsem):
    s = pl.program_id(0)
    nblk = pl.num_programs(0) - 1
    cur = jnp.minimum(s, nblk - 1)
    prv = jnp.maximum(s - 1, 0)
    n_used = nu_ref[0]
    n_runs = nr_ref[0]
    compute = (s >= 1) & (s <= n_used)

    def weight_copies(j):
        e = rexp_ref[j]
        k = j % 2
        return [pltpu.make_async_copy(src.at[layer, e], dst.at[k], wsem.at[k])
                for src, dst in ((wg_hbm, wg_f), (wu_hbm, wu_f), (wd_hbm, wd_f))]

    def fetch(j):
        for cp in weight_copies(j):
            cp.start()

    @pl.when(s == 0)
    def _():
        fetch(0)

        @pl.when(n_runs > 1)
        def _():
            fetch(1)

    @pl.when(compute)
    def _():
        b = prv % 2

        def wait_group(i, carry):
            pltpu.make_async_copy(h_ref.at[pl.ds(0, ROW_GROUP), :], xbuf.at[b, pl.ds(0, ROW_GROUP), :],
                                  gsem.at[b]).wait()
            return carry

        lax.fori_loop(0, nrow_ref[prv] // ROW_GROUP, wait_group, 0)

    @pl.when(s < n_used)
    def _():
        b = s % 2

        def issue_group(i, carry):
            for r in range(ROW_GROUP):
                row = i * ROW_GROUP + r
                pltpu.make_async_copy(h_ref.at[pl.ds(tok_ref[0, row], 1), :], xbuf.at[b, pl.ds(row, 1), :],
                                      gsem.at[b]).start(priority=1)
            return carry

        lax.fori_loop(0, nrow_ref[cur] // ROW_GROUP, issue_group, 0)

    @pl.when((s < n_used) & ((s == 0) | (run_ref[cur] != run_ref[prv])))
    def _():
        j = run_ref[cur]
        k = j % 2
        for cp in weight_copies(j):
            cp.wait()
        wgu_s[k, :, :MOE_D_FF] = wg_f[k].astype(bf16)
        wgu_s[k, :, MOE_D_FF:] = wu_f[k].astype(bf16)
        wd_s[k] = wd_f[k].astype(bf16)

        @pl.when(j + 2 < n_runs)
        def _():
            fetch(j + 2)

    @pl.when(compute)
    def _():
        k = run_ref[prv] % 2
        gathered = lax.broadcasted_iota(jnp.int32, (MOE_BLOCK, 1), 0) < nrow_ref[prv]
        x = jnp.where(gathered, xbuf[prv % 2], 0.0).astype(bf16)
        gu = jnp.dot(x, wgu_s[k], preferred_element_type=f32)
        g = gu[:, :MOE_D_FF]
        hmid = (g * jax.nn.sigmoid(g) * gu[:, MOE_D_FF:]).astype(bf16)
        o_ref[...] = jnp.dot(hmid, wd_s[k], preferred_element_type=f32).astype(o_ref.dtype)

    @pl.when(s > n_used)
    def _():
        o_ref[...] = jnp.zeros_like(o_ref)


def moe_ffn(block_expert, run_expert, block_rows, n_used, h, row_token, w_gate, w_up, w_down, layer):
    rows = row_token.shape[0]
    d = h.shape[1]
    nblk = rows // MOE_BLOCK
    changed = jnp.concatenate([jnp.zeros((1,), jnp.int32),
                               (block_expert[1:] != block_expert[:-1]).astype(jnp.int32)])
    run = jnp.cumsum(changed).astype(jnp.int32)
    n_runs = run[jnp.maximum(n_used[0] - 1, 0)].reshape(1) + 1

    def cur_block(s, *_):
        return (jnp.minimum(s, nblk - 1), 0, 0)

    any_spec = pl.BlockSpec(memory_space=pl.ANY)
    return pl.pallas_call(
        functools.partial(_moe_kernel, layer),
        grid_spec=pltpu.PrefetchScalarGridSpec(
            num_scalar_prefetch=5,
            grid=(nblk + 1,),
            in_specs=[pl.BlockSpec((None, 1, MOE_BLOCK), cur_block, memory_space=pltpu.SMEM),
                      any_spec,
                      any_spec, any_spec, any_spec],
            out_specs=pl.BlockSpec((MOE_BLOCK, d), lambda s, *_: (jnp.maximum(s - 1, 0), 0)),
            scratch_shapes=[pltpu.VMEM((2, d, MOE_D_FF), f32), pltpu.VMEM((2, d, MOE_D_FF), f32),
                            pltpu.VMEM((2, MOE_D_FF, d), f32),
                            pltpu.VMEM((2, d, 2 * MOE_D_FF), bf16), pltpu.VMEM((2, MOE_D_FF, d), bf16),
                            pltpu.VMEM((2, MOE_BLOCK, d), f32),
                            pltpu.SemaphoreType.DMA((2,)), pltpu.SemaphoreType.DMA((2,))],
        ),
        out_shape=jax.ShapeDtypeStruct((rows, d), f32),
        compiler_params=_cparams(("arbitrary",)),
        name="moe_ffn",
    )(run, run_expert, block_rows, n_used, n_runs.astype(jnp.int32), row_token.reshape(nblk, 1, MOE_BLOCK), h,
      w_gate, w_up, w_down)


def _rms(x, g):
    return x * lax.rsqrt(jnp.mean(x * x, axis=-1, keepdims=True) + NORM_EPS) * g


def _pad_cols(w, width):
    return jnp.pad(w, ((0, 0), (0, width - w.shape[1])))


def _pad_rows(w, height):
    return jnp.pad(w, ((0, height - w.shape[0]), (0, 0)))


def _moe(h, layer, g_w, g_b, e_w, e_b, w_gate, w_up, w_down):
    t, d = h.shape
    w_r = _pad_cols(jnp.concatenate([g_w, e_w], axis=1), LANES)
    logits = matmul(h, w_r, tm=1024, tn=LANES, name="router")
    group_logits = logits[:, :MOE_GROUPS] + g_b
    group = jnp.argmax(group_logits, axis=-1)
    group_w = jnp.take_along_axis(jax.nn.softmax(group_logits, axis=-1), group[:, None], axis=-1)
    exp_logits = (logits[:, MOE_GROUPS:MOE_GROUPS + MOE_EXPERTS] + e_b).reshape(t, MOE_GROUPS, MOE_EPG)
    in_group = jnp.take_along_axis(exp_logits, group[:, None, None], axis=1)[:, 0]
    top_logit, top_idx = lax.top_k(in_group, MOE_TOP_K)
    weights = (group_w * jax.nn.softmax(top_logit, axis=-1)).reshape(-1)
    expert_ids = (group[:, None] * MOE_EPG + top_idx).reshape(-1).astype(jnp.int32)
    n_assign = t * MOE_TOP_K
    n_blocks = -(-n_assign // MOE_BLOCK) + MOE_EXPERTS
    rows = n_blocks * MOE_BLOCK
    onehot = (expert_ids[:, None] == jnp.arange(MOE_EXPERTS, dtype=jnp.int32)[None, :]).astype(jnp.int32)
    csum = jnp.cumsum(onehot, axis=0)
    rank = jnp.take_along_axis(csum, expert_ids[:, None], axis=1)[:, 0] - 1
    counts = csum[-1]
    padded = (counts + MOE_BLOCK - 1) // MOE_BLOCK * MOE_BLOCK
    ends = jnp.cumsum(padded)
    starts = ends - padded
    dest = starts[expert_ids] + rank
    token_ids = jnp.arange(n_assign, dtype=jnp.int32) // MOE_TOP_K
    row_token = jnp.zeros((rows,), jnp.int32).at[dest].set(token_ids)
    block_start = jnp.arange(n_blocks, dtype=jnp.int32) * MOE_BLOCK
    block_expert = jnp.minimum(jnp.searchsorted(ends, block_start, side='right'),
                               MOE_EXPERTS - 1).astype(jnp.int32)
    n_used = (ends[-1] // MOE_BLOCK).astype(jnp.int32).reshape(1)
    block_valid = jnp.clip(starts[block_expert] + counts[block_expert] - block_start, 0, MOE_BLOCK)
    block_valid = jnp.where(block_start < ends[-1], block_valid, 0)
    block_rows = ((block_valid + ROW_GROUP - 1) // ROW_GROUP * ROW_GROUP).astype(jnp.int32)
    experts = jnp.arange(MOE_EXPERTS, dtype=jnp.int32)
    run_expert = jnp.sort(jnp.where(counts > 0, experts, MOE_EXPERTS)).astype(jnp.int32)
    yb = moe_ffn(block_expert, run_expert, block_rows, n_used, h, row_token, w_gate, w_up, w_down, layer)
    return yb, dest.reshape(t, MOE_TOP_K), weights.reshape(t, MOE_TOP_K)


def _gather_rows(d0_ref, d1_ref, w_ref, yb_ref, buf, sem):
    def issue(r, carry):
        pltpu.make_async_copy(yb_ref.at[pl.ds(d0_ref[0, r], 1), :], buf.at[0, pl.ds(r, 1), :],
                              sem.at[0]).start(priority=0)
        pltpu.make_async_copy(yb_ref.at[pl.ds(d1_ref[0, r], 1), :], buf.at[1, pl.ds(r, 1), :],
                              sem.at[1]).start(priority=1)
        return carry

    lax.fori_loop(0, OUT_TM, issue, 0, unroll=8)
    for k in range(MOE_TOP_K):
        pltpu.make_async_copy(yb_ref.at[pl.ds(0, OUT_TM), :], buf.at[k], sem.at[k]).wait()
    w = w_ref[...]
    return buf[0] * w[:, 0:1] + buf[1] * w[:, 1:2]


def _combine_kernel(d0_ref, d1_ref, w_ref, yb_ref, x_ref, gate_ref, ng_ref, scale_ref, shift_ref, xo_ref, h_ref,
                    buf, sem):
    xn = x_ref[...] + gate_ref[...] * _gather_rows(d0_ref, d1_ref, w_ref, yb_ref, buf, sem)
    xo_ref[...] = xn
    y = xn * lax.rsqrt(jnp.mean(xn * xn, axis=-1, keepdims=True) + NORM_EPS) * ng_ref[...]
    h_ref[...] = (y * (1.0 + scale_ref[...]) + shift_ref[...]).astype(h_ref.dtype)


def _final_kernel(d0_ref, d1_ref, w_ref, yb_ref, x_ref, gate_ref, ng_ref, o_ref, buf, sem):
    xn = x_ref[...] + gate_ref[...] * _gather_rows(d0_ref, d1_ref, w_ref, yb_ref, buf, sem)
    o_ref[...] = xn * lax.rsqrt(jnp.mean(xn * xn, axis=-1, keepdims=True) + NORM_EPS) * ng_ref[...]


def combine(yb, dest, weights, x, gate, norm_g, scale, shift, seq):
    t, d = x.shape
    nb = t // OUT_TM
    per_b = pl.BlockSpec((None, 1, d), lambda i: (i * OUT_TM // seq, 0, 0))
    rows = pl.BlockSpec((OUT_TM, d), lambda i: (i, 0))
    vec = pl.BlockSpec((1, d), lambda i: (0, 0))
    idx = pl.BlockSpec((None, 1, OUT_TM), lambda i: (i, 0, 0), memory_space=pltpu.SMEM)
    wts = pl.BlockSpec((OUT_TM, MOE_TOP_K), lambda i: (i, 0))
    any_spec = pl.BlockSpec(memory_space=pl.ANY)
    scratch = [pltpu.VMEM((MOE_TOP_K, OUT_TM, d), f32), pltpu.SemaphoreType.DMA((MOE_TOP_K,))]
    d0 = dest[:, 0].reshape(nb, 1, OUT_TM)
    d1 = dest[:, 1].reshape(nb, 1, OUT_TM)
    if scale is None:
        return pl.pallas_call(
            _final_kernel, grid=(nb,), in_specs=[idx, idx, wts, any_spec, rows, per_b, vec], out_specs=rows,
            out_shape=jax.ShapeDtypeStruct((t, d), f32), scratch_shapes=scratch,
            compiler_params=_cparams(("arbitrary",)), name="final_norm")(d0, d1, weights, yb, x, gate, norm_g)
    return pl.pallas_call(
        _combine_kernel, grid=(nb,), in_specs=[idx, idx, wts, any_spec, rows, per_b, vec, per_b, per_b],
        out_specs=[rows, rows],
        out_shape=[jax.ShapeDtypeStruct((t, d), f32), jax.ShapeDtypeStruct((t, d), bf16)],
        scratch_shapes=scratch, compiler_params=_cparams(("arbitrary",)),
        name="combine")(d0, d1, weights, yb, x, gate, norm_g, scale, shift)


def kernel(x, c, positions, ada_w, ada_b, norm1_g, norm2_g, final_g, w_in, rw_mu, rw_w0, rw_w2, rw_a0, rw_a2, rw_g2, rw_k_k, rw_k_a, rw_r_k, rw_ln_g, rw_ln_b, sg_ln_g, sg_ln_b, sg_w, sg_b, mla_q_norm_g, mla_w_uq, mla_kv_norm_g, mla_w_ukv, p_rwkv, p_sgu, p_mla, w_o, router_g_w, router_g_b, router_e_w, router_e_b, exp_w_gate, exp_w_up, exp_w_down):
    bsz, seq, d = x.shape
    t = bsz * seq
    half = MLA_QK_ROPE // 2
    inv_freq = ROPE_BASE ** (-jnp.arange(half, dtype=f32) / half)
    ang = (positions.astype(f32)[..., None] * inv_freq).reshape(t, half)
    cos, sin = jnp.cos(ang), jnp.sin(ang)
    zpad = jnp.zeros((t, LANES - MLA_QK_ROPE), f32)
    rope_cs = jnp.concatenate([cos, cos, zpad], axis=1)
    rope_sn = jnp.concatenate([-sin, sin, zpad], axis=1)
    c_act = jnp.pad(jax.nn.silu(c), ((0, 8 - bsz), (0, 0)))
    x = x.reshape(t, d)
    row = lambda vec: vec[None, :]

    def per_batch(vec):
        return vec[:, None, :]

    mods = []
    for l in range(DEPTH):
        mod = matmul(c_act, ada_w, tm=8, tn=1024, layer=l, name="ada")[:bsz] + ada_b[l]
        mods.append(jnp.split(mod, 6, axis=-1))

    shift1, scale1 = mods[0][0], mods[0][1]
    h = ((_rms(x, norm1_g[0]).reshape(bsz, seq, d) * (1.0 + scale1[:, None, :]) + shift1[:, None, :])
         .reshape(t, d).astype(bf16))
    for l in range(DEPTH):
        _, _, gate1, shift2, scale2, gate2 = mods[l]
        wl = w_in[l]
        w_rw = jnp.concatenate([wl[:, :3072], _pad_cols(wl[:, 3072:3168], LANES),
                                _pad_cols(wl[:, 3168:3264], LANES), wl[:, 3264:3520]], axis=1).astype(bf16)
        mul = rw_mu[l]
        mu = jnp.concatenate([mul[:3072], jnp.pad(mul[3072:3168], (0, 32)),
                              jnp.pad(mul[3168:3264], (0, 32)), mul[3264:3520]])
        w_sg = wl[:, 3520:5568].astype(bf16)
        w_at = _pad_cols(wl[:, 5568:6400], MLA_COLS_PAD).astype(bf16)
        w_gt = wl[:, 6400:].astype(bf16)
        p_rw = matmul(h, w_rw, tm=1024, tn=896, name="in_rw").reshape(bsz, seq, RW_COLS_PAD)
        p_sg = matmul(h, w_sg, tm=1024, tn=1024, out_dtype=bf16, name="in_sg")
        p_at = matmul(h, w_at, tm=1024, tn=MLA_COLS_PAD, name="in_at")
        p_gate = matmul(h, w_gt, tm=1024, tn=1024, out_dtype=bf16, name="in_gate")
        lw, a, g = rw_lora(p_rw, row(mu[3 * RW_WIDTH:]), row(rw_w0[l]), row(rw_a0[l]),
                           _pad_rows(rw_w2[l], LANES).astype(bf16), _pad_rows(rw_a2[l], LANES).astype(bf16),
                           rw_g2[l].astype(bf16))
        y_rw = rwkv_mix(p_rw, lw, a, g, row(mu[:3 * RW_WIDTH]), row(rw_k_k[l]), row(rw_k_a[l]),
                        row(rw_r_k[l].reshape(-1)), row(rw_ln_g[l]), row(rw_ln_b[l])).reshape(t, RW_WIDTH)
        bias_full = jnp.repeat(sg_b[l].T, RW_HEAD_DIM, axis=1)
        y_sg = sgu(p_sg, row(sg_ln_g[l]), row(sg_ln_b[l]), jnp.tril(sg_w[l]).astype(bf16), bias_full)
        wq = mla_w_uq[l].reshape(MLA_Q_LORA, MLA_HEADS, MLA_QK_DIM) * (MLA_QK_DIM ** -0.5 * LOG2_E)
        wq = jnp.pad(wq, ((0, 0), (0, 0), (0, ATT_QK_PAD - MLA_QK_DIM))).reshape(MLA_Q_LORA, -1).astype(bf16)
        q, kv, kr = mla_prep(p_at, row(mla_q_norm_g[l]), row(mla_kv_norm_g[l]), wq, mla_w_ukv[l].astype(bf16),
                             rope_cs, rope_sn)
        y_at = attention(q.reshape(bsz, seq, -1), kv.reshape(bsz, seq, -1),
                         kr.reshape(bsz, seq, LANES)).reshape(t, -1)
        mix = merge(y_rw, y_sg, y_at, p_gate, p_rwkv[l].astype(bf16), p_sgu[l].astype(bf16), p_mla[l].astype(bf16))
        x, h2 = out_proj(mix, w_o[l].astype(bf16), x, per_batch(gate1), row(norm2_g[l]), per_batch(scale2),
                         per_batch(shift2), seq)
        yb, dest, wts = _moe(h2, l, router_g_w[l], router_g_b[l], router_e_w[l], router_e_b[l],
                             exp_w_gate, exp_w_up, exp_w_down)
        if l + 1 < DEPTH:
            x, h = combine(yb, dest, wts, x, per_batch(gate2), row(norm1_g[l + 1]), per_batch(mods[l + 1][1]),
                           per_batch(mods[l + 1][0]), seq)
    return combine(yb, dest, wts, x, per_batch(gate2), row(final_g), None, None, seq).reshape(bsz, seq, d)
```

```python
import functools

import jax
import jax.numpy as jnp
from jax import lax
from jax.experimental import pallas as pl
from jax.experimental.pallas import tpu as pltpu

f32 = jnp.float32
bf16 = jnp.bfloat16

D_MODEL = 2048
DEPTH = 4
RW_HEADS = 16
RW_HEAD_DIM = 64
RW_WIDTH = 1024
RW_LORA_COLS = 512
RW_COLS_PAD = 3 * RW_WIDTH + RW_LORA_COLS
RW_GN_EPS = 64e-5
SG_CHUNK = 128
SG_GROUPS = 16
SG_WIDTH = 1024
MLA_HEADS = 8
MLA_Q_LORA = 512
MLA_KV_LORA = 256
MLA_QK_NOPE = 128
MLA_QK_ROPE = 64
MLA_QK_DIM = 192
MLA_V_DIM = 128
MLA_COLS_PAD = 896
ROPE_BASE = 10000.0
MOE_GROUPS = 8
MOE_EPG = 8
MOE_EXPERTS = 64
MOE_TOP_K = 2
MOE_D_FF = 384
MOE_BLOCK = 256
NORM_EPS = 1e-6
LOG2_E = 1.4426950408889634

LANES = 128
RW_CHUNK = 64
RW_SUB = 4
VMEM_LIMIT = 48 * 1024 * 1024


def _cparams(sem):
    return pltpu.CompilerParams(dimension_semantics=sem, vmem_limit_bytes=VMEM_LIMIT)


def _dot(a, b):
    return jnp.dot(a.astype(bf16), b.astype(bf16), preferred_element_type=f32)


def _dot_nt(a, b):
    return lax.dot_general(a.astype(bf16), b.astype(bf16), (((1,), (1,)), ((), ())),
                           preferred_element_type=f32)


def _dot_tn(a, b):
    return lax.dot_general(a.astype(bf16), b.astype(bf16), (((0,), (0,)), ((), ())),
                           preferred_element_type=f32)


def _mm_kernel(a_ref, w_ref, o_ref):
    o_ref[...] = _dot(a_ref[...], w_ref[...]).astype(o_ref.dtype)


def matmul(a, w, *, tm, tn, out_dtype=f32, layer=None, name="mm"):
    m, k = a.shape
    n = w.shape[-1]
    assert m % tm == 0 and n % tn == 0, (m, tm, n, tn)
    if layer is None:
        w_spec = pl.BlockSpec((k, tn), lambda j, i: (0, j))
    else:
        w_spec = pl.BlockSpec((None, k, tn), lambda j, i: (layer, 0, j))
    return pl.pallas_call(
        _mm_kernel,
        grid=(n // tn, m // tm),
        in_specs=[pl.BlockSpec((tm, k), lambda j, i: (i, 0)), w_spec],
        out_specs=pl.BlockSpec((tm, tn), lambda j, i: (i, j)),
        out_shape=jax.ShapeDtypeStruct((m, n), out_dtype),
        compiler_params=_cparams(("parallel", "parallel")),
        name=name,
    )(a, w)


RW_LORA_TM = 1024


def _softplus(z):
    return jnp.maximum(z, 0.0) + jnp.log(1.0 + jnp.exp(-jnp.abs(z)))


def _rw_lora_kernel(p_ref, mu_ref, w0_ref, a0_ref, ww_ref, wa_ref, wg_ref, lw_ref, a_ref, g_ref, prev_ref):
    @pl.when(pl.program_id(1) == 0)
    def _():
        prev_ref[...] = jnp.zeros_like(prev_ref)

    x = p_ref[...]
    rowi = lax.broadcasted_iota(jnp.int32, x.shape, 0)
    xprev = jnp.where(rowi == 0, prev_ref[...], pltpu.roll(x, 1, 0))
    prev_ref[...] = x[RW_LORA_TM - 1:RW_LORA_TM, :]
    xs = x + mu_ref[...] * (xprev - x)
    dec = _dot(jnp.tanh(xs[:, :LANES]), ww_ref[...])
    log_w = -_softplus(-(w0_ref[...] + dec)) - 0.5
    lw_ref[...] = -jnp.exp(log_w)
    a_ref[...] = jax.nn.sigmoid(a0_ref[...] + _dot(xs[:, LANES:2 * LANES], wa_ref[...]))
    g_ref[...] = _dot(jax.nn.sigmoid(xs[:, 2 * LANES:]), wg_ref[...]).astype(g_ref.dtype)


def rw_lora(p_rw, mu_lora, w0, a0, w_w2, w_a2, w_g2):
    b, s, _ = p_rw.shape
    cb = 3 * RW_WIDTH // RW_LORA_COLS
    vec = pl.BlockSpec((1, RW_WIDTH), lambda i, j: (0, 0))
    out = pl.BlockSpec((None, RW_LORA_TM, RW_WIDTH), lambda i, j: (i, j, 0))
    return pl.pallas_call(
        _rw_lora_kernel,
        grid=(b, s // RW_LORA_TM),
        in_specs=[pl.BlockSpec((None, RW_LORA_TM, RW_LORA_COLS), lambda i, j: (i, j, cb)),
                  pl.BlockSpec((1, RW_LORA_COLS), lambda i, j: (0, 0)), vec, vec,
                  pl.BlockSpec((LANES, RW_WIDTH), lambda i, j: (0, 0)),
                  pl.BlockSpec((LANES, RW_WIDTH), lambda i, j: (0, 0)),
                  pl.BlockSpec((2 * LANES, RW_WIDTH), lambda i, j: (0, 0))],
        out_specs=[out, out, out],
        out_shape=[jax.ShapeDtypeStruct((b, s, RW_WIDTH), f32), jax.ShapeDtypeStruct((b, s, RW_WIDTH), f32),
                   jax.ShapeDtypeStruct((b, s, RW_WIDTH), bf16)],
        scratch_shapes=[pltpu.VMEM((1, RW_LORA_COLS), f32)],
        compiler_params=_cparams(("parallel", "arbitrary")),
        name="rw_lora",
    )(p_rw, mu_lora, w0, a0, w_w2, w_a2, w_g2)


def _rwkv_kernel(p_ref, lw_ref, a_ref, g_ref, mu_ref, kk_ref, ka_ref, rk_ref, lng_ref, lnb_ref,
                 o_ref, state_ref, prev_ref):
    c = RW_CHUNK
    n = 2 * c
    nrow = RW_SUB * c
    pairs = range(RW_HEADS // 2)
    units = [(q, p) for q in range(RW_SUB) for p in pairs]

    @pl.when(pl.program_id(1) == 0)
    def _():
        state_ref[...] = jnp.zeros_like(state_ref)
        prev_ref[...] = jnp.zeros_like(prev_ref)

    row = lax.broadcasted_iota(jnp.int32, (n, n), 0)
    col = lax.broadcasted_iota(jnp.int32, (n, n), 1)
    same = (row >> 6) == (col >> 6)
    rpos = row & (c - 1)
    cpos = col & (c - 1)
    strict = same & (cpos < rpos)
    incl = same & (cpos <= rpos)
    eye = (row == col).astype(f32)
    blk16 = (row >> 4) == (col >> 4)
    blk32 = (row >> 5) == (col >> 5)
    low32 = blk32 & jnp.logical_not(blk16)
    low64 = same & jnp.logical_not(blk32)
    tr = lax.broadcasted_iota(jnp.int32, (c, c), 0)
    tc = lax.broadcasted_iota(jnp.int32, (c, c), 1)
    tril1 = (tc <= tr).astype(bf16)
    m0 = lax.broadcasted_iota(jnp.int32, (c, n), 1) < c
    m0f = lax.broadcasted_iota(jnp.int32, (nrow, n), 1) < c
    row0 = lax.broadcasted_iota(jnp.int32, (nrow, n), 0) == 0

    def expand(y):
        return jnp.concatenate([jnp.where(m0, y, 0.0), jnp.where(m0, 0.0, y)], axis=0)

    def collapse(y):
        return y[:c] + y[c:]

    def head_sum(y, mask):
        lo = jnp.sum(jnp.where(mask, y, 0.0), axis=-1, keepdims=True)
        hi = jnp.sum(jnp.where(mask, 0.0, y), axis=-1, keepdims=True)
        return jnp.where(mask, lo, hi)

    def shifted(base, p):
        sl = slice(base + p * LANES, base + (p + 1) * LANES)
        x = p_ref[:, sl]
        xprev = jnp.where(row0, prev_ref[:, sl], pltpu.roll(x, 1, 0))
        prev_ref[:, sl] = x[nrow - 1:nrow, :]
        return x + mu_ref[:, sl] * (xprev - x)

    sls = [slice(p * LANES, (p + 1) * LANES) for p in pairs]
    rf = [shifted(0, p) for p in pairs]
    k0f = [shifted(RW_WIDTH, p) for p in pairs]
    vf = [shifted(2 * RW_WIDTH, p) for p in pairs]
    af = [a_ref[:, s] for s in sls]
    kkrf = [k0f[p] * kk_ref[:, sls[p]] for p in pairs]
    kkf = [kkrf[p] * lax.rsqrt(jnp.maximum(head_sum(kkrf[p] * kkrf[p], m0f), 1e-24)) for p in pairs]
    kf = [k0f[p] * (1.0 + (af[p] - 1.0) * ka_ref[:, sls[p]]) for p in pairs]
    bonus = [head_sum(rf[p] * kf[p] * rk_ref[:, sls[p]], m0f) * vf[p] for p in pairs]

    def chunk(xs, q, p):
        return xs[p][q * c:(q + 1) * c, :]

    r = [chunk(rf, q, p) for q, p in units]
    v = [chunk(vf, q, p) for q, p in units]
    a = [chunk(af, q, p) for q, p in units]
    kk = [chunk(kkf, q, p) for q, p in units]
    k = [chunk(kf, q, p) for q, p in units]
    lw = [lw_ref[q * c:(q + 1) * c, sls[p]] for q, p in units]
    pairs = range(len(units))

    def cumsum(x):
        x1 = x.astype(bf16)
        res = x - x1.astype(f32)
        x2 = res.astype(bf16)
        x3 = (res - x2.astype(f32)).astype(bf16)
        return (jnp.dot(tril1, x1, preferred_element_type=f32) + jnp.dot(tril1, x2, preferred_element_type=f32)
                + jnp.dot(tril1, x3, preferred_element_type=f32))

    cum = [cumsum(lw[p]) for p in pairs]
    ctot = [cum[p][c - 1:c, :] for p in pairs]
    ginv = [jnp.exp(-cum[p]) for p in pairs]
    g2 = [jnp.exp(ctot[p] - cum[p]) for p in pairs]
    b = [kk[p] * a[p] for p in pairs]
    at = [-kk[p] * jnp.exp(cum[p] - lw[p]) for p in pairs]
    rt = [r[p] * jnp.exp(cum[p]) for p in pairs]
    bt = [b[p] * ginv[p] for p in pairs]
    kt = [k[p] * ginv[p] for p in pairs]
    pm = [_dot_nt(jnp.concatenate([expand(at[p]), expand(rt[p])], axis=0),
                  jnp.concatenate([expand(bt[p]), expand(kt[p])], axis=0)) for p in pairs]
    a_ab = [jnp.where(strict, pm[p][:n, :n], 0.0) for p in pairs]
    a_ak = [jnp.where(strict, pm[p][:n, n:], 0.0).astype(bf16) for p in pairs]
    m_rb = [jnp.where(incl, pm[p][n:, :n], 0.0).astype(bf16) for p in pairs]
    m_rk = [jnp.where(incl, pm[p][n:, n:], 0.0).astype(bf16) for p in pairs]
    ev = [expand(v[p]).astype(bf16) for p in pairs]
    akv = [_dot(a_ak[p], ev[p]) for p in pairs]
    o0 = [collapse(_dot(m_rk[p], ev[p])) for p in pairs]
    kv = [jnp.where(same, _dot_tn(v[p], k[p] * g2[p]), 0.0) for p in pairs]
    a0 = [jnp.where(blk16, a_ab[p], 0.0) for p in pairs]
    a2 = [_dot(a0[p], a0[p]) for p in pairs]
    p2 = [eye + a0[p] + a2[p] + _dot(a0[p], a2[p]) for p in pairs]
    a4 = [_dot(a2[p], a2[p]) for p in pairs]
    p3 = [p2[p] + _dot(p2[p], a4[p]) for p in pairs]
    a8 = [_dot(a4[p], a4[p]) for p in pairs]
    d = [p3[p] + _dot(p3[p], a8[p]) for p in pairs]
    x1 = [_dot(jnp.where(low32, a_ab[p], 0.0), d[p]) for p in pairs]
    d = [d[p] + _dot(d[p], x1[p]) for p in pairs]
    x2 = [_dot(jnp.where(low64, a_ab[p], 0.0), d[p]) for p in pairs]
    t = [(d[p] + _dot(d[p], x2[p])).astype(bf16) for p in pairs]
    tx = [_dot(t[p], jnp.concatenate([expand(at[p]), akv[p]], axis=1)) for p in pairs]
    at_hat = [collapse(tx[p][:, :n]) for p in pairs]
    u0 = [collapse(tx[p][:, n:]) for p in pairs]
    inv_n = 1.0 / RW_HEAD_DIM
    npair = RW_HEADS // 2
    state = [state_ref[h] for h in range(npair)]
    for q in range(RW_SUB):
        ids = [q * npair + h for h in range(npair)]
        so = [_dot_nt(jnp.concatenate([at_hat[i], rt[i]], axis=0), state[h]) for h, i in enumerate(ids)]
        u = [so[h][:c] + u0[i] for h, i in enumerate(ids)]
        su = [jnp.where(same, _dot_tn(u[h], b[i] * g2[i]), 0.0) for h, i in enumerate(ids)]
        mu_ = [collapse(_dot(m_rb[i], expand(u[h]))) for h, i in enumerate(ids)]
        state = [state[h] * jnp.exp(ctot[i]) + kv[i] + su[h] for h, i in enumerate(ids)]
        rows = slice(q * c, (q + 1) * c)
        for h, i in enumerate(ids):
            o = so[h][c:] + o0[i] + mu_[h]
            mean = head_sum(o, m0) * inv_n
            oc = o - mean
            var = head_sum(oc * oc, m0) * inv_n
            y = oc * lax.rsqrt(var + RW_GN_EPS) * lng_ref[:, sls[h]] + lnb_ref[:, sls[h]]
            y = y + bonus[h][rows, :]
            o_ref[rows, sls[h]] = (y * g_ref[rows, sls[h]].astype(f32)).astype(o_ref.dtype)
    for h in range(npair):
        state_ref[h] = state[h]


def rwkv_mix(p_rw, lw, a, g, mu, k_k, k_a, r_k, ln_g, ln_b):
    bsz, s, _ = p_rw.shape
    nrow = RW_SUB * RW_CHUNK
    spec = pl.BlockSpec((None, nrow, RW_WIDTH), lambda i, j: (i, j, 0))
    vec = pl.BlockSpec((1, RW_WIDTH), lambda i, j: (0, 0))
    return pl.pallas_call(
        _rwkv_kernel,
        grid=(bsz, s // nrow),
        in_specs=[pl.BlockSpec((None, nrow, 3 * RW_WIDTH), lambda i, j: (i, j, 0)), spec, spec, spec,
                  pl.BlockSpec((1, 3 * RW_WIDTH), lambda i, j: (0, 0)), vec, vec, vec, vec, vec],
        out_specs=spec,
        out_shape=jax.ShapeDtypeStruct((bsz, s, RW_WIDTH), bf16),
        scratch_shapes=[pltpu.VMEM((RW_HEADS // 2, LANES, LANES), f32), pltpu.VMEM((1, 3 * RW_WIDTH), f32)],
        compiler_params=_cparams(("parallel", "arbitrary")),
        name="rwkv_mix",
    )(p_rw, lw, a, g, mu, k_k, k_a, r_k, ln_g, ln_b)


def _sgu_kernel(p_ref, lng_ref, lnb_ref, w_ref, bias_ref, o_ref):
    x = p_ref[...].astype(f32)
    g = 0.5 * x * (1.0 + jnp.tanh(0.7978845608028654 * (x + 0.044715 * (x * x * x))))
    u = g[:, :SG_WIDTH]
    v = g[:, SG_WIDTH:]
    mu = jnp.mean(v, axis=-1, keepdims=True)
    vc = v - mu
    var = jnp.mean(vc * vc, axis=-1, keepdims=True)
    vn = (vc * lax.rsqrt(var + 1e-5) * lng_ref[...] + lnb_ref[...]).astype(bf16)
    first = lax.broadcasted_iota(jnp.int32, (SG_CHUNK, LANES), 1) < (LANES // 2)
    for q in range(SG_GROUPS // 2):
        sl = slice(q * LANES, (q + 1) * LANES)
        vq = vn[:, sl]
        lo = jnp.dot(w_ref[2 * q], vq, preferred_element_type=f32)
        hi = jnp.dot(w_ref[2 * q + 1], vq, preferred_element_type=f32)
        mixed = jnp.where(first, lo, hi) + bias_ref[:, sl]
        o_ref[:, sl] = (u[:, sl] * mixed).astype(o_ref.dtype)


def sgu(p_sg, ln_g, ln_b, w_tril, bias_full):
    t = p_sg.shape[0]
    return pl.pallas_call(
        _sgu_kernel,
        grid=(t // SG_CHUNK,),
        in_specs=[pl.BlockSpec((SG_CHUNK, 2 * SG_WIDTH), lambda i: (i, 0)),
                  pl.BlockSpec((1, SG_WIDTH), lambda i: (0, 0)),
                  pl.BlockSpec((1, SG_WIDTH), lambda i: (0, 0)),
                  pl.BlockSpec((SG_GROUPS, SG_CHUNK, SG_CHUNK), lambda i: (0, 0, 0)),
                  pl.BlockSpec((SG_CHUNK, SG_WIDTH), lambda i: (0, 0))],
        out_specs=pl.BlockSpec((SG_CHUNK, SG_WIDTH), lambda i: (i, 0)),
        out_shape=jax.ShapeDtypeStruct((t, SG_WIDTH), bf16),
        compiler_params=_cparams(("parallel",)),
        name="sgu",
    )(p_sg, ln_g, ln_b, w_tril, bias_full)


MLA_TM = 512
ATT_QK_PAD = 256


def _mla_prep_kernel(p_ref, qg_ref, kvg_ref, wq_ref, wkv_ref, cs_ref, sn_ref, q_ref, kv_ref, kr_ref):
    x = p_ref[...]

    def rms(y, g):
        return y * lax.rsqrt(jnp.mean(y * y, axis=-1, keepdims=True) + NORM_EPS) * g

    q = _dot(rms(x[:, :MLA_Q_LORA], qg_ref[...]), wq_ref[...])
    kv_ref[...] = _dot(rms(x[:, MLA_Q_LORA:MLA_Q_LORA + MLA_KV_LORA], kvg_ref[...]), wkv_ref[...]).astype(bf16)
    cs = cs_ref[...]
    sn = sn_ref[...]
    half = lax.broadcasted_iota(jnp.int32, cs.shape, 1) < (MLA_QK_ROPE // 2)

    def rope(y):
        swapped = jnp.where(half, pltpu.roll(y, LANES - MLA_QK_ROPE // 2, 1), pltpu.roll(y, MLA_QK_ROPE // 2, 1))
        return y * cs + swapped * sn

    kr_ref[...] = rope(x[:, MLA_Q_LORA + MLA_KV_LORA:]).astype(bf16)
    for h in range(MLA_HEADS):
        base = h * ATT_QK_PAD
        q_ref[:, base:base + LANES] = q[:, base:base + LANES].astype(bf16)
        q_ref[:, base + LANES:base + 2 * LANES] = rope(q[:, base + LANES:base + 2 * LANES]).astype(bf16)


def mla_prep(p_at, qg, kvg, wq, wkv, cs, sn):
    t = p_at.shape[0]
    full = lambda shape: pl.BlockSpec(shape, lambda i: (0, 0))
    return pl.pallas_call(
        _mla_prep_kernel,
        grid=(t // MLA_TM,),
        in_specs=[pl.BlockSpec((MLA_TM, MLA_COLS_PAD), lambda i: (i, 0)),
                  full((1, MLA_Q_LORA)), full((1, MLA_KV_LORA)),
                  full((MLA_Q_LORA, MLA_HEADS * ATT_QK_PAD)), full((MLA_KV_LORA, MLA_HEADS * 256)),
                  pl.BlockSpec((MLA_TM, LANES), lambda i: (i, 0)), pl.BlockSpec((MLA_TM, LANES), lambda i: (i, 0))],
        out_specs=[pl.BlockSpec((MLA_TM, MLA_HEADS * ATT_QK_PAD), lambda i: (i, 0)),
                   pl.BlockSpec((MLA_TM, MLA_HEADS * 256), lambda i: (i, 0)),
                   pl.BlockSpec((MLA_TM, LANES), lambda i: (i, 0))],
        out_shape=[jax.ShapeDtypeStruct((t, MLA_HEADS * ATT_QK_PAD), bf16),
                   jax.ShapeDtypeStruct((t, MLA_HEADS * 256), bf16),
                   jax.ShapeDtypeStruct((t, LANES), bf16)],
        compiler_params=_cparams(("parallel",)),
        name="mla_prep",
    )(p_at, qg, kvg, wq, wkv, cs, sn)


ATT_TQ = 1024
ATT_TK = 512


def _attn_kernel(q_ref, kn_ref, kr_ref, v_ref, o_ref):
    i = pl.program_id(2)
    hq = ATT_TQ // 2
    qs = [q_ref[:hq, :], q_ref[hq:, :]]

    def keys(j):
        rows = pl.ds(pl.multiple_of(j * ATT_TK, ATT_TK), ATT_TK)
        return jnp.concatenate([kn_ref[rows, :], kr_ref[rows, :]], axis=1), v_ref[rows, :]

    def update(carry, s, vj):
        m_old, l_old, acc = carry
        m_new = jnp.maximum(m_old, jnp.max(s, axis=-1, keepdims=True))
        p = jnp.exp2(s - m_new)
        alpha = jnp.exp2(m_old - m_new)
        l_new = alpha * l_old + jnp.sum(p, axis=-1, keepdims=True)
        acc = alpha * acc + jnp.dot(p.astype(bf16), vj, preferred_element_type=f32)
        return m_new, l_new, acc

    def body(j, carry):
        ka, va = keys(2 * j)
        kb, vb = keys(2 * j + 1)
        sa = [_dot_nt(qs[h], ka) for h in range(2)]
        sb = [_dot_nt(qs[h], kb) for h in range(2)]
        carry = tuple(update(carry[h], sa[h], va) for h in range(2))
        return tuple(update(carry[h], sb[h], vb) for h in range(2))

    init = tuple((jnp.full((hq, 1), -jnp.inf, f32), jnp.zeros((hq, 1), f32), jnp.zeros((hq, MLA_V_DIM), f32))
                 for _ in range(2))
    carry = lax.fori_loop(0, i, body, init)
    rowi = lax.broadcasted_iota(jnp.int32, (hq, ATT_TK), 0)
    coli = lax.broadcasted_iota(jnp.int32, (hq, ATT_TK), 1)
    diag = coli > rowi
    kj, vj = keys(2 * i)
    c0 = update(carry[0], jnp.where(diag, -jnp.inf, _dot_nt(qs[0], kj)), vj)
    c1 = update(carry[1], _dot_nt(qs[1], kj), vj)
    kj, vj = keys(2 * i + 1)
    c1 = update(c1, jnp.where(diag, -jnp.inf, _dot_nt(qs[1], kj)), vj)
    o_ref[:hq, :] = (c0[2] / c0[1]).astype(o_ref.dtype)
    o_ref[hq:, :] = (c1[2] / c1[1]).astype(o_ref.dtype)


def attention(q, kv, kr):
    b, s, _ = q.shape
    return pl.pallas_call(
        _attn_kernel,
        grid=(b, MLA_HEADS, s // ATT_TQ),
        in_specs=[pl.BlockSpec((None, ATT_TQ, ATT_QK_PAD), lambda bi, h, i: (bi, i, h)),
                  pl.BlockSpec((None, s, MLA_QK_NOPE), lambda bi, h, i: (bi, 0, 2 * h)),
                  pl.BlockSpec((None, s, LANES), lambda bi, h, i: (bi, 0, 0)),
                  pl.BlockSpec((None, s, MLA_V_DIM), lambda bi, h, i: (bi, 0, 2 * h + 1))],
        out_specs=pl.BlockSpec((None, ATT_TQ, MLA_V_DIM), lambda bi, h, i: (bi, i, h)),
        out_shape=jax.ShapeDtypeStruct((b, s, MLA_HEADS * MLA_V_DIM), bf16),
        compiler_params=_cparams(("parallel", "parallel", "parallel")),
        name="attention",
    )(q, kv, kr, kv)


OUT_TM = 512


def _merge_kernel(yr_ref, ys_ref, ya_ref, gate_ref, wr_ref, ws_ref, wa_ref, o_ref):
    d = D_MODEL
    acc = jax.nn.sigmoid(gate_ref[:, :d].astype(f32)) * jnp.dot(yr_ref[...], wr_ref[...], preferred_element_type=f32)
    acc += jax.nn.sigmoid(gate_ref[:, d:2 * d].astype(f32)) * jnp.dot(ys_ref[...], ws_ref[...],
                                                                      preferred_element_type=f32)
    acc += jax.nn.sigmoid(gate_ref[:, 2 * d:].astype(f32)) * jnp.dot(ya_ref[...], wa_ref[...],
                                                                     preferred_element_type=f32)
    o_ref[...] = acc.astype(o_ref.dtype)


def merge(y_rw, y_sg, y_at, p_gate, w_rw, w_sg, w_at):
    t = y_rw.shape[0]
    yspec = pl.BlockSpec((OUT_TM, RW_WIDTH), lambda i: (i, 0))
    wspec = pl.BlockSpec((RW_WIDTH, D_MODEL), lambda i: (0, 0), pipeline_mode=pl.Buffered(1))
    return pl.pallas_call(
        _merge_kernel,
        grid=(t // OUT_TM,),
        in_specs=[yspec, yspec, yspec, pl.BlockSpec((OUT_TM, 3 * D_MODEL), lambda i: (i, 0)), wspec, wspec, wspec],
        out_specs=pl.BlockSpec((OUT_TM, D_MODEL), lambda i: (i, 0)),
        out_shape=jax.ShapeDtypeStruct((t, D_MODEL), bf16),
        compiler_params=_cparams(("parallel",)),
        name="merge",
    )(y_rw, y_sg, y_at, p_gate, w_rw, w_sg, w_at)


def _out_kernel(mix_ref, wo_ref, x_ref, gate_ref, ng_ref, scale_ref, shift_ref, wr_ref, xo_ref, h_ref, lg_ref):
    xn = x_ref[...] + gate_ref[...] * jnp.dot(mix_ref[...], wo_ref[...], preferred_element_type=f32)
    xo_ref[...] = xn
    y = xn * lax.rsqrt(jnp.mean(xn * xn, axis=-1, keepdims=True) + NORM_EPS) * ng_ref[...]
    h = y * (1.0 + scale_ref[...]) + shift_ref[...]
    h_ref[...] = h
    lg_ref[...] = _dot(h, wr_ref[...])


def out_proj(mix, w_o, x, gate, norm_g, scale, shift, w_router, seq):
    t, d = x.shape
    per_b = pl.BlockSpec((None, 1, d), lambda i: (i * OUT_TM // seq, 0, 0))
    rows = pl.BlockSpec((OUT_TM, d), lambda i: (i, 0))
    return pl.pallas_call(
        _out_kernel,
        grid=(t // OUT_TM,),
        in_specs=[rows, pl.BlockSpec((d, d), lambda i: (0, 0), pipeline_mode=pl.Buffered(1)), rows, per_b,
                  pl.BlockSpec((1, d), lambda i: (0, 0)), per_b, per_b,
                  pl.BlockSpec((d, LANES), lambda i: (0, 0))],
        out_specs=[rows, rows, pl.BlockSpec((OUT_TM, LANES), lambda i: (i, 0))],
        out_shape=[jax.ShapeDtypeStruct((t, d), f32), jax.ShapeDtypeStruct((t, d), f32),
                   jax.ShapeDtypeStruct((t, LANES), f32)],
        compiler_params=_cparams(("parallel",)),
        name="out_proj",
    )(mix, w_o, x, gate, norm_g, scale, shift, w_router)


ROW_GROUP = 8


def _moe_kernel(layer, run_ref, rexp_ref, nrow_ref, nu_ref, nr_ref, tok_ref, h_ref, wg_hbm, wu_hbm, wd_hbm, o_ref,
                wg_f, wu_f, wd_f, wgu_s, wd_s, xbuf, gsem, wsem):
    s = pl.program_id(0)
    nblk = pl.num_programs(0) - 1
    cur = jnp.minimum(s, nblk - 1)
    prv = jnp.maximum(s - 1, 0)
    n_used = nu_ref[0]
    n_runs = nr_ref[0]
    compute = (s >= 1) & (s <= n_used)

    def weight_copies(j):
        e = rexp_ref[j]
        k = j % 2
        return [pltpu.make_async_copy(src.at[layer, e], dst.at[k], wsem.at[k])
                for src, dst in ((wg_hbm, wg_f), (wu_hbm, wu_f), (wd_hbm, wd_f))]

    def fetch(j):
        for cp in weight_copies(j):
            cp.start()

    @pl.when(s == 0)
    def _():
        fetch(0)

        @pl.when(n_runs > 1)
        def _():
            fetch(1)

    @pl.when(compute)
    def _():
        b = prv % 2

        def wait_group(i, carry):
            pltpu.make_async_copy(h_ref.at[pl.ds(0, ROW_GROUP), :], xbuf.at[b, pl.ds(0, ROW_GROUP), :],
                                  gsem.at[b]).wait()
            return carry

        lax.fori_loop(0, nrow_ref[prv] // ROW_GROUP, wait_group, 0)

    @pl.when(s < n_used)
    def _():
        b = s % 2

        def issue_group(i, carry):
            for r in range(ROW_GROUP):
                row = i * ROW_GROUP + r
                pltpu.make_async_copy(h_ref.at[pl.ds(tok_ref[0, row], 1), :], xbuf.at[b, pl.ds(row, 1), :],
                                      gsem.at[b]).start(priority=1)
            return carry

        lax.fori_loop(0, nrow_ref[cur] // ROW_GROUP, issue_group, 0)

    @pl.when((s < n_used) & ((s == 0) | (run_ref[cur] != run_ref[prv])))
    def _():
        j = run_ref[cur]
        k = j % 2
        for cp in weight_copies(j):
            cp.wait()
        wgu_s[k, :, :MOE_D_FF] = wg_f[k].astype(bf16)
        wgu_s[k, :, MOE_D_FF:] = wu_f[k].astype(bf16)
        wd_s[k] = wd_f[k].astype(bf16)

        @pl.when(j + 2 < n_runs)
        def _():
            fetch(j + 2)

    @pl.when(compute)
    def _():
        k = run_ref[prv] % 2
        gathered = lax.broadcasted_iota(jnp.int32, (MOE_BLOCK, 1), 0) < nrow_ref[prv]
        x = jnp.where(gathered, xbuf[prv % 2], 0.0).astype(bf16)
        gu = jnp.dot(x, wgu_s[k], preferred_element_type=f32)
        g = gu[:, :MOE_D_FF]
        hmid = (g * jax.nn.sigmoid(g) * gu[:, MOE_D_FF:]).astype(bf16)
        o_ref[...] = jnp.dot(hmid, wd_s[k], preferred_element_type=f32).astype(o_ref.dtype)

    @pl.when(s > n_used)
    def _():
        o_ref[...] = jnp.zeros_like(o_ref)


def moe_ffn(block_expert, run_expert, block_rows, n_used, h, row_token, w_gate, w_up, w_down, layer):
    rows = row_token.shape[0]
    d = h.shape[1]
    nblk = rows // MOE_BLOCK
    changed = jnp.concatenate([jnp.zeros((1,), jnp.int32),
                               (block_expert[1:] != block_expert[:-1]).astype(jnp.int32)])
    run = jnp.cumsum(changed).astype(jnp.int32)
    n_runs = run[jnp.maximum(n_used[0] - 1, 0)].reshape(1) + 1

    def cur_block(s, *_):
        return (jnp.minimum(s, nblk - 1), 0, 0)

    any_spec = pl.BlockSpec(memory_space=pl.ANY)
    return pl.pallas_call(
        functools.partial(_moe_kernel, layer),
        grid_spec=pltpu.PrefetchScalarGridSpec(
            num_scalar_prefetch=5,
            grid=(nblk + 1,),
            in_specs=[pl.BlockSpec((None, 1, MOE_BLOCK), cur_block, memory_space=pltpu.SMEM),
                      any_spec,
                      any_spec, any_spec, any_spec],
            out_specs=pl.BlockSpec((MOE_BLOCK, d), lambda s, *_: (jnp.maximum(s - 1, 0), 0)),
            scratch_shapes=[pltpu.VMEM((2, d, MOE_D_FF), f32), pltpu.VMEM((2, d, MOE_D_FF), f32),
                            pltpu.VMEM((2, MOE_D_FF, d), f32),
                            pltpu.VMEM((2, d, 2 * MOE_D_FF), bf16), pltpu.VMEM((2, MOE_D_FF, d), bf16),
                            pltpu.VMEM((2, MOE_BLOCK, d), f32),
                            pltpu.SemaphoreType.DMA((2,)), pltpu.SemaphoreType.DMA((2,))],
        ),
        out_shape=jax.ShapeDtypeStruct((rows, d), f32),
        compiler_params=_cparams(("arbitrary",)),
        name="moe_ffn",
    )(run, run_expert, block_rows, n_used, n_runs.astype(jnp.int32), row_token.reshape(nblk, 1, MOE_BLOCK), h,
      w_gate, w_up, w_down)


def _rms(x, g):
    return x * lax.rsqrt(jnp.mean(x * x, axis=-1, keepdims=True) + NORM_EPS) * g


def _pad_cols(w, width):
    return jnp.pad(w, ((0, 0), (0, width - w.shape[1])))


def _pad_rows(w, height):
    return jnp.pad(w, ((0, height - w.shape[0]), (0, 0)))


def _moe(h, logits, layer, g_b, e_b, w_gate, w_up, w_down):
    t, d = h.shape
    group_logits = logits[:, :MOE_GROUPS] + g_b
    group = jnp.argmax(group_logits, axis=-1)
    group_w = jnp.take_along_axis(jax.nn.softmax(group_logits, axis=-1), group[:, None], axis=-1)
    exp_logits = (logits[:, MOE_GROUPS:MOE_GROUPS + MOE_EXPERTS] + e_b).reshape(t, MOE_GROUPS, MOE_EPG)
    in_group = jnp.take_along_axis(exp_logits, group[:, None, None], axis=1)[:, 0]
    lane = jnp.arange(MOE_EPG, dtype=jnp.int32)[None, :]
    first = jnp.argmax(in_group, axis=-1).astype(jnp.int32)
    rest = jnp.where(lane == first[:, None], -jnp.inf, in_group)
    second = jnp.argmax(rest, axis=-1).astype(jnp.int32)
    top_idx = jnp.stack([first, second], axis=1)
    top_logit = jnp.stack([jnp.max(in_group, axis=-1), jnp.max(rest, axis=-1)], axis=1)
    weights = (group_w * jax.nn.softmax(top_logit, axis=-1)).reshape(-1)
    expert_ids = (group[:, None] * MOE_EPG + top_idx).reshape(-1).astype(jnp.int32)
    n_assign = t * MOE_TOP_K
    n_blocks = -(-n_assign // MOE_BLOCK) + MOE_EXPERTS
    rows = n_blocks * MOE_BLOCK
    onehot = (expert_ids[:, None] == jnp.arange(MOE_EXPERTS, dtype=jnp.int32)[None, :]).astype(jnp.int32)
    csum = jnp.cumsum(onehot, axis=0)
    rank = jnp.take_along_axis(csum, expert_ids[:, None], axis=1)[:, 0] - 1
    counts = csum[-1]
    padded = (counts + MOE_BLOCK - 1) // MOE_BLOCK * MOE_BLOCK
    ends = jnp.cumsum(padded)
    starts = ends - padded
    dest = starts[expert_ids] + rank
    token_ids = jnp.arange(n_assign, dtype=jnp.int32) // MOE_TOP_K
    row_token = jnp.zeros((rows,), jnp.int32).at[dest].set(token_ids)
    block_start = jnp.arange(n_blocks, dtype=jnp.int32) * MOE_BLOCK
    block_expert = jnp.minimum(jnp.searchsorted(ends, block_start, side='right'),
                               MOE_EXPERTS - 1).astype(jnp.int32)
    n_used = (ends[-1] // MOE_BLOCK).astype(jnp.int32).reshape(1)
    block_valid = jnp.clip(starts[block_expert] + counts[block_expert] - block_start, 0, MOE_BLOCK)
    block_valid = jnp.where(block_start < ends[-1], block_valid, 0)
    block_rows = ((block_valid + ROW_GROUP - 1) // ROW_GROUP * ROW_GROUP).astype(jnp.int32)
    experts = jnp.arange(MOE_EXPERTS, dtype=jnp.int32)
    run_expert = jnp.sort(jnp.where(counts > 0, experts, MOE_EXPERTS)).astype(jnp.int32)
    yb = moe_ffn(block_expert, run_expert, block_rows, n_used, h, row_token, w_gate, w_up, w_down, layer)
    return yb, dest.reshape(t, MOE_TOP_K), weights.reshape(t, MOE_TOP_K)


def _gather_rows(d0_ref, d1_ref, w_ref, yb_ref, buf, sem):
    def issue(r, carry):
        pltpu.make_async_copy(yb_ref.at[pl.ds(d0_ref[0, r], 1), :], buf.at[0, pl.ds(r, 1), :],
                              sem.at[0]).start(priority=0)
        pltpu.make_async_copy(yb_ref.at[pl.ds(d1_ref[0, r], 1), :], buf.at[1, pl.ds(r, 1), :],
                              sem.at[1]).start(priority=1)
        return carry

    lax.fori_loop(0, OUT_TM, issue, 0, unroll=8)
    for k in range(MOE_TOP_K):
        pltpu.make_async_copy(yb_ref.at[pl.ds(0, OUT_TM), :], buf.at[k], sem.at[k]).wait()
    w = w_ref[...]
    return buf[0] * w[:, 0:1] + buf[1] * w[:, 1:2]


def _combine_kernel(d0_ref, d1_ref, w_ref, yb_ref, x_ref, gate_ref, ng_ref, scale_ref, shift_ref, xo_ref, h_ref,
                    buf, sem):
    xn = x_ref[...] + gate_ref[...] * _gather_rows(d0_ref, d1_ref, w_ref, yb_ref, buf, sem)
    xo_ref[...] = xn
    y = xn * lax.rsqrt(jnp.mean(xn * xn, axis=-1, keepdims=True) + NORM_EPS) * ng_ref[...]
    h_ref[...] = (y * (1.0 + scale_ref[...]) + shift_ref[...]).astype(h_ref.dtype)


def _final_kernel(d0_ref, d1_ref, w_ref, yb_ref, x_ref, gate_ref, ng_ref, o_ref, buf, sem):
    xn = x_ref[...] + gate_ref[...] * _gather_rows(d0_ref, d1_ref, w_ref, yb_ref, buf, sem)
    o_ref[...] = xn * lax.rsqrt(jnp.mean(xn * xn, axis=-1, keepdims=True) + NORM_EPS) * ng_ref[...]


def combine(yb, dest, weights, x, gate, norm_g, scale, shift, seq):
    t, d = x.shape
    nb = t // OUT_TM
    per_b = pl.BlockSpec((None, 1, d), lambda i: (i * OUT_TM // seq, 0, 0))
    rows = pl.BlockSpec((OUT_TM, d), lambda i: (i, 0))
    vec = pl.BlockSpec((1, d), lambda i: (0, 0))
    idx = pl.BlockSpec((None, 1, OUT_TM), lambda i: (i, 0, 0), memory_space=pltpu.SMEM)
    wts = pl.BlockSpec((OUT_TM, MOE_TOP_K), lambda i: (i, 0))
    any_spec = pl.BlockSpec(memory_space=pl.ANY)
    scratch = [pltpu.VMEM((MOE_TOP_K, OUT_TM, d), f32), pltpu.SemaphoreType.DMA((MOE_TOP_K,))]
    d0 = dest[:, 0].reshape(nb, 1, OUT_TM)
    d1 = dest[:, 1].reshape(nb, 1, OUT_TM)
    if scale is None:
        return pl.pallas_call(
            _final_kernel, grid=(nb,), in_specs=[idx, idx, wts, any_spec, rows, per_b, vec], out_specs=rows,
            out_shape=jax.ShapeDtypeStruct((t, d), f32), scratch_shapes=scratch,
            compiler_params=_cparams(("arbitrary",)), name="final_norm")(d0, d1, weights, yb, x, gate, norm_g)
    return pl.pallas_call(
        _combine_kernel, grid=(nb,), in_specs=[idx, idx, wts, any_spec, rows, per_b, vec, per_b, per_b],
        out_specs=[rows, rows],
        out_shape=[jax.ShapeDtypeStruct((t, d), f32), jax.ShapeDtypeStruct((t, d), bf16)],
        scratch_shapes=scratch, compiler_params=_cparams(("arbitrary",)),
        name="combine")(d0, d1, weights, yb, x, gate, norm_g, scale, shift)


def kernel(x, c, positions, ada_w, ada_b, norm1_g, norm2_g, final_g, w_in, rw_mu, rw_w0, rw_w2, rw_a0, rw_a2, rw_g2, rw_k_k, rw_k_a, rw_r_k, rw_ln_g, rw_ln_b, sg_ln_g, sg_ln_b, sg_w, sg_b, mla_q_norm_g, mla_w_uq, mla_kv_norm_g, mla_w_ukv, p_rwkv, p_sgu, p_mla, w_o, router_g_w, router_g_b, router_e_w, router_e_b, exp_w_gate, exp_w_up, exp_w_down):
    bsz, seq, d = x.shape
    t = bsz * seq
    half = MLA_QK_ROPE // 2
    inv_freq = ROPE_BASE ** (-jnp.arange(half, dtype=f32) / half)
    ang = (positions.astype(f32)[..., None] * inv_freq).reshape(t, half)
    cos, sin = jnp.cos(ang), jnp.sin(ang)
    zpad = jnp.zeros((t, LANES - MLA_QK_ROPE), f32)
    rope_cs = jnp.concatenate([cos, cos, zpad], axis=1)
    rope_sn = jnp.concatenate([-sin, sin, zpad], axis=1)
    c_act = jnp.pad(jax.nn.silu(c), ((0, 8 - bsz), (0, 0)))
    x = x.reshape(t, d)
    row = lambda vec: vec[None, :]

    def per_batch(vec):
        return vec[:, None, :]

    mods = []
    for l in range(DEPTH):
        mod = matmul(c_act, ada_w, tm=8, tn=1024, layer=l, name="ada")[:bsz] + ada_b[l]
        mods.append(jnp.split(mod, 6, axis=-1))

    shift1, scale1 = mods[0][0], mods[0][1]
    h = ((_rms(x, norm1_g[0]).reshape(bsz, seq, d) * (1.0 + scale1[:, None, :]) + shift1[:, None, :])
         .reshape(t, d).astype(bf16))
    for l in range(DEPTH):
        _, _, gate1, shift2, scale2, gate2 = mods[l]
        wl = w_in[l]
        w_rw = jnp.concatenate([wl[:, :3072], _pad_cols(wl[:, 3072:3168], LANES),
                                _pad_cols(wl[:, 3168:3264], LANES), wl[:, 3264:3520]], axis=1).astype(bf16)
        mul = rw_mu[l]
        mu = jnp.concatenate([mul[:3072], jnp.pad(mul[3072:3168], (0, 32)),
                              jnp.pad(mul[3168:3264], (0, 32)), mul[3264:3520]])
        w_sg = wl[:, 3520:5568].astype(bf16)
        w_at = _pad_cols(wl[:, 5568:6400], MLA_COLS_PAD).astype(bf16)
        w_gt = wl[:, 6400:].astype(bf16)
        p_rw = matmul(h, w_rw, tm=1024, tn=896, name="in_rw").reshape(bsz, seq, RW_COLS_PAD)
        p_sg = matmul(h, w_sg, tm=1024, tn=1024, out_dtype=bf16, name="in_sg")
        p_at = matmul(h, w_at, tm=1024, tn=MLA_COLS_PAD, name="in_at")
        p_gate = matmul(h, w_gt, tm=1024, tn=1024, out_dtype=bf16, name="in_gate")
        lw, a, g = rw_lora(p_rw, row(mu[3 * RW_WIDTH:]), row(rw_w0[l]), row(rw_a0[l]),
                           _pad_rows(rw_w2[l], LANES).astype(bf16), _pad_rows(rw_a2[l], LANES).astype(bf16),
                           rw_g2[l].astype(bf16))
        y_rw = rwkv_mix(p_rw, lw, a, g, row(mu[:3 * RW_WIDTH]), row(rw_k_k[l]), row(rw_k_a[l]),
                        row(rw_r_k[l].reshape(-1)), row(rw_ln_g[l]), row(rw_ln_b[l])).reshape(t, RW_WIDTH)
        bias_full = jnp.repeat(sg_b[l].T, RW_HEAD_DIM, axis=1)
        y_sg = sgu(p_sg, row(sg_ln_g[l]), row(sg_ln_b[l]), jnp.tril(sg_w[l]).astype(bf16), bias_full)
        wq = mla_w_uq[l].reshape(MLA_Q_LORA, MLA_HEADS, MLA_QK_DIM) * (MLA_QK_DIM ** -0.5 * LOG2_E)
        wq = jnp.pad(wq, ((0, 0), (0, 0), (0, ATT_QK_PAD - MLA_QK_DIM))).reshape(MLA_Q_LORA, -1).astype(bf16)
        q, kv, kr = mla_prep(p_at, row(mla_q_norm_g[l]), row(mla_kv_norm_g[l]), wq, mla_w_ukv[l].astype(bf16),
                             rope_cs, rope_sn)
        y_at = attention(q.reshape(bsz, seq, -1), kv.reshape(bsz, seq, -1),
                         kr.reshape(bsz, seq, LANES)).reshape(t, -1)
        mix = merge(y_rw, y_sg, y_at, p_gate, p_rwkv[l].astype(bf16), p_sgu[l].astype(bf16), p_mla[l].astype(bf16))
        w_router = _pad_cols(jnp.concatenate([router_g_w[l], router_e_w[l]], axis=1), LANES).astype(bf16)
        x, h2, logits = out_proj(mix, w_o[l].astype(bf16), x, per_batch(gate1), row(norm2_g[l]), per_batch(scale2),
                                 per_batch(shift2), w_router, seq)
        yb, dest, wts = _moe(h2, logits, l, router_g_b[l], router_e_b[l], exp_w_gate, exp_w_up, exp_w_down)
        if l + 1 < DEPTH:
            x, h = combine(yb, dest, wts, x, per_batch(gate2), row(norm1_g[l + 1]), per_batch(mods[l + 1][1]),
                           per_batch(mods[l + 1][0]), seq)
    return combine(yb, dest, wts, x, per_batch(gate2), row(final_g), None, None, seq).reshape(bsz, seq, d)
```

```python
import functools

import jax
import jax.numpy as jnp
from jax import lax
from jax.experimental import pallas as pl
from jax.experimental.pallas import tpu as pltpu

f32 = jnp.float32
bf16 = jnp.bfloat16

D_MODEL = 2048
DEPTH = 4
RW_HEADS = 16
RW_HEAD_DIM = 64
RW_WIDTH = 1024
RW_LORA_COLS = 512
RW_COLS_PAD = 3 * RW_WIDTH + RW_LORA_COLS
RW_GN_EPS = 64e-5
SG_CHUNK = 128
SG_GROUPS = 16
SG_WIDTH = 1024
MLA_HEADS = 8
MLA_Q_LORA = 512
MLA_KV_LORA = 256
MLA_QK_NOPE = 128
MLA_QK_ROPE = 64
MLA_QK_DIM = 192
MLA_V_DIM = 128
MLA_COLS_PAD = 896
ROPE_BASE = 10000.0
MOE_GROUPS = 8
MOE_EPG = 8
MOE_EXPERTS = 64
MOE_TOP_K = 2
MOE_D_FF = 384
MOE_BLOCK = 256
NORM_EPS = 1e-6
LOG2_E = 1.4426950408889634

LANES = 128
RW_CHUNK = 64
RW_SUB = 4
VMEM_LIMIT = 48 * 1024 * 1024


def _cparams(sem):
    return pltpu.CompilerParams(dimension_semantics=sem, vmem_limit_bytes=VMEM_LIMIT)


def _dot(a, b):
    return jnp.dot(a.astype(bf16), b.astype(bf16), preferred_element_type=f32)


def _dot_nt(a, b):
    return lax.dot_general(a.astype(bf16), b.astype(bf16), (((1,), (1,)), ((), ())),
                           preferred_element_type=f32)


def _dot_tn(a, b):
    return lax.dot_general(a.astype(bf16), b.astype(bf16), (((0,), (0,)), ((), ())),
                           preferred_element_type=f32)


def _mm_kernel(a_ref, w_ref, o_ref):
    o_ref[...] = _dot(a_ref[...], w_ref[...]).astype(o_ref.dtype)


def matmul(a, w, *, tm, tn, out_dtype=f32, layer=None, name="mm"):
    m, k = a.shape
    n = w.shape[-1]
    assert m % tm == 0 and n % tn == 0, (m, tm, n, tn)
    if layer is None:
        w_spec = pl.BlockSpec((k, tn), lambda j, i: (0, j))
    else:
        w_spec = pl.BlockSpec((None, k, tn), lambda j, i: (layer, 0, j))
    return pl.pallas_call(
        _mm_kernel,
        grid=(n // tn, m // tm),
        in_specs=[pl.BlockSpec((tm, k), lambda j, i: (i, 0)), w_spec],
        out_specs=pl.BlockSpec((tm, tn), lambda j, i: (i, j)),
        out_shape=jax.ShapeDtypeStruct((m, n), out_dtype),
        compiler_params=_cparams(("parallel", "parallel")),
        name=name,
    )(a, w)


RW_LORA_TM = 1024


def _softplus(z):
    return jnp.maximum(z, 0.0) + jnp.log(1.0 + jnp.exp(-jnp.abs(z)))


def _rw_lora_kernel(p_ref, mu_ref, w0_ref, a0_ref, ww_ref, wa_ref, wg_ref, lw_ref, a_ref, g_ref, prev_ref):
    @pl.when(pl.program_id(1) == 0)
    def _():
        prev_ref[...] = jnp.zeros_like(prev_ref)

    x = p_ref[...]
    rowi = lax.broadcasted_iota(jnp.int32, x.shape, 0)
    xprev = jnp.where(rowi == 0, prev_ref[...], pltpu.roll(x, 1, 0))
    prev_ref[...] = x[RW_LORA_TM - 1:RW_LORA_TM, :]
    xs = x + mu_ref[...] * (xprev - x)
    dec = _dot(jnp.tanh(xs[:, :LANES]), ww_ref[...])
    log_w = -_softplus(-(w0_ref[...] + dec)) - 0.5
    lw_ref[...] = -jnp.exp(log_w)
    a_ref[...] = jax.nn.sigmoid(a0_ref[...] + _dot(xs[:, LANES:2 * LANES], wa_ref[...]))
    g_ref[...] = _dot(jax.nn.sigmoid(xs[:, 2 * LANES:]), wg_ref[...]).astype(g_ref.dtype)


def rw_lora(p_rw, mu_lora, w0, a0, w_w2, w_a2, w_g2):
    b, s, _ = p_rw.shape
    cb = 3 * RW_WIDTH // RW_LORA_COLS
    vec = pl.BlockSpec((1, RW_WIDTH), lambda i, j: (0, 0))
    out = pl.BlockSpec((None, RW_LORA_TM, RW_WIDTH), lambda i, j: (i, j, 0))
    return pl.pallas_call(
        _rw_lora_kernel,
        grid=(b, s // RW_LORA_TM),
        in_specs=[pl.BlockSpec((None, RW_LORA_TM, RW_LORA_COLS), lambda i, j: (i, j, cb)),
                  pl.BlockSpec((1, RW_LORA_COLS), lambda i, j: (0, 0)), vec, vec,
                  pl.BlockSpec((LANES, RW_WIDTH), lambda i, j: (0, 0)),
                  pl.BlockSpec((LANES, RW_WIDTH), lambda i, j: (0, 0)),
                  pl.BlockSpec((2 * LANES, RW_WIDTH), lambda i, j: (0, 0))],
        out_specs=[out, out, out],
        out_shape=[jax.ShapeDtypeStruct((b, s, RW_WIDTH), f32), jax.ShapeDtypeStruct((b, s, RW_WIDTH), f32),
                   jax.ShapeDtypeStruct((b, s, RW_WIDTH), bf16)],
        scratch_shapes=[pltpu.VMEM((1, RW_LORA_COLS), f32)],
        compiler_params=_cparams(("parallel", "arbitrary")),
        name="rw_lora",
    )(p_rw, mu_lora, w0, a0, w_w2, w_a2, w_g2)


def _rwkv_kernel(p_ref, lw_ref, a_ref, g_ref, mu_ref, kk_ref, ka_ref, rk_ref, lng_ref, lnb_ref,
                 o_ref, state_ref, prev_ref):
    c = RW_CHUNK
    n = 2 * c
    nrow = RW_SUB * c
    pairs = range(RW_HEADS // 2)
    units = [(q, p) for q in range(RW_SUB) for p in pairs]

    @pl.when(pl.program_id(1) == 0)
    def _():
        state_ref[...] = jnp.zeros_like(state_ref)
        prev_ref[...] = jnp.zeros_like(prev_ref)

    row = lax.broadcasted_iota(jnp.int32, (n, n), 0)
    col = lax.broadcasted_iota(jnp.int32, (n, n), 1)
    same = (row >> 6) == (col >> 6)
    rpos = row & (c - 1)
    cpos = col & (c - 1)
    strict = same & (cpos < rpos)
    incl = same & (cpos <= rpos)
    eye = (row == col).astype(f32)
    blk16 = (row >> 4) == (col >> 4)
    blk32 = (row >> 5) == (col >> 5)
    low32 = blk32 & jnp.logical_not(blk16)
    low64 = same & jnp.logical_not(blk32)
    tr = lax.broadcasted_iota(jnp.int32, (c, c), 0)
    tc = lax.broadcasted_iota(jnp.int32, (c, c), 1)
    tril1 = (tc <= tr).astype(bf16)
    m0 = lax.broadcasted_iota(jnp.int32, (c, n), 1) < c
    m0f = lax.broadcasted_iota(jnp.int32, (nrow, n), 1) < c
    row0 = lax.broadcasted_iota(jnp.int32, (nrow, n), 0) == 0

    def expand(y):
        return jnp.concatenate([jnp.where(m0, y, 0.0), jnp.where(m0, 0.0, y)], axis=0)

    def collapse(y):
        return y[:c] + y[c:]

    def head_sum(y, mask):
        lo = jnp.sum(jnp.where(mask, y, 0.0), axis=-1, keepdims=True)
        hi = jnp.sum(jnp.where(mask, 0.0, y), axis=-1, keepdims=True)
        return jnp.where(mask, lo, hi)

    def shifted(base, p):
        sl = slice(base + p * LANES, base + (p + 1) * LANES)
        x = p_ref[:, sl]
        xprev = jnp.where(row0, prev_ref[:, sl], pltpu.roll(x, 1, 0))
        prev_ref[:, sl] = x[nrow - 1:nrow, :]
        return x + mu_ref[:, sl] * (xprev - x)

    sls = [slice(p * LANES, (p + 1) * LANES) for p in pairs]
    rf = [shifted(0, p) for p in pairs]
    k0f = [shifted(RW_WIDTH, p) for p in pairs]
    vf = [shifted(2 * RW_WIDTH, p) for p in pairs]
    af = [a_ref[:, s] for s in sls]
    kkrf = [k0f[p] * kk_ref[:, sls[p]] for p in pairs]
    kkf = [kkrf[p] * lax.rsqrt(jnp.maximum(head_sum(kkrf[p] * kkrf[p], m0f), 1e-24)) for p in pairs]
    kf = [k0f[p] * (1.0 + (af[p] - 1.0) * ka_ref[:, sls[p]]) for p in pairs]
    bonus = [head_sum(rf[p] * kf[p] * rk_ref[:, sls[p]], m0f) * vf[p] for p in pairs]

    def chunk(xs, q, p):
        return xs[p][q * c:(q + 1) * c, :]

    r = [chunk(rf, q, p) for q, p in units]
    v = [chunk(vf, q, p) for q, p in units]
    a = [chunk(af, q, p) for q, p in units]
    kk = [chunk(kkf, q, p) for q, p in units]
    k = [chunk(kf, q, p) for q, p in units]
    lw = [lw_ref[q * c:(q + 1) * c, sls[p]] for q, p in units]
    pairs = range(len(units))

    def cumsum(x):
        x1 = x.astype(bf16)
        res = x - x1.astype(f32)
        x2 = res.astype(bf16)
        x3 = (res - x2.astype(f32)).astype(bf16)
        return (jnp.dot(tril1, x1, preferred_element_type=f32) + jnp.dot(tril1, x2, preferred_element_type=f32)
                + jnp.dot(tril1, x3, preferred_element_type=f32))

    cum = [cumsum(lw[p]) for p in pairs]
    ctot = [cum[p][c - 1:c, :] for p in pairs]
    ginv = [jnp.exp(-cum[p]) for p in pairs]
    g2 = [jnp.exp(ctot[p] - cum[p]) for p in pairs]
    b = [kk[p] * a[p] for p in pairs]
    at = [-kk[p] * jnp.exp(cum[p] - lw[p]) for p in pairs]
    rt = [r[p] * jnp.exp(cum[p]) for p in pairs]
    bt = [b[p] * ginv[p] for p in pairs]
    kt = [k[p] * ginv[p] for p in pairs]
    pm = [_dot_nt(jnp.concatenate([expand(at[p]), expand(rt[p])], axis=0),
                  jnp.concatenate([expand(bt[p]), expand(kt[p])], axis=0)) for p in pairs]
    a_ab = [jnp.where(strict, pm[p][:n, :n], 0.0) for p in pairs]
    a_ak = [jnp.where(strict, pm[p][:n, n:], 0.0).astype(bf16) for p in pairs]
    m_rb = [jnp.where(incl, pm[p][n:, :n], 0.0).astype(bf16) for p in pairs]
    m_rk = [jnp.where(incl, pm[p][n:, n:], 0.0).astype(bf16) for p in pairs]
    ev = [expand(v[p]).astype(bf16) for p in pairs]
    akv = [_dot(a_ak[p], ev[p]) for p in pairs]
    o0 = [collapse(_dot(m_rk[p], ev[p])) for p in pairs]
    kv = [jnp.where(same, _dot_tn(v[p], k[p] * g2[p]), 0.0) for p in pairs]
    a0 = [jnp.where(blk16, a_ab[p], 0.0) for p in pairs]
    a2 = [_dot(a0[p], a0[p]) for p in pairs]
    p2 = [eye + a0[p] + a2[p] + _dot(a0[p], a2[p]) for p in pairs]
    a4 = [_dot(a2[p], a2[p]) for p in pairs]
    p3 = [p2[p] + _dot(p2[p], a4[p]) for p in pairs]
    a8 = [_dot(a4[p], a4[p]) for p in pairs]
    d = [p3[p] + _dot(p3[p], a8[p]) for p in pairs]
    x1 = [_dot(jnp.where(low32, a_ab[p], 0.0), d[p]) for p in pairs]
    d = [d[p] + _dot(d[p], x1[p]) for p in pairs]
    x2 = [_dot(jnp.where(low64, a_ab[p], 0.0), d[p]) for p in pairs]
    t = [(d[p] + _dot(d[p], x2[p])).astype(bf16) for p in pairs]
    tx = [_dot(t[p], jnp.concatenate([expand(at[p]), akv[p]], axis=1)) for p in pairs]
    at_hat = [collapse(tx[p][:, :n]) for p in pairs]
    u0 = [collapse(tx[p][:, n:]) for p in pairs]
    inv_n = 1.0 / RW_HEAD_DIM
    npair = RW_HEADS // 2
    state = [state_ref[h] for h in range(npair)]
    for q in range(RW_SUB):
        ids = [q * npair + h for h in range(npair)]
        so = [_dot_nt(jnp.concatenate([at_hat[i], rt[i]], axis=0), state[h]) for h, i in enumerate(ids)]
        u = [so[h][:c] + u0[i] for h, i in enumerate(ids)]
        su = [jnp.where(same, _dot_tn(u[h], b[i] * g2[i]), 0.0) for h, i in enumerate(ids)]
        mu_ = [collapse(_dot(m_rb[i], expand(u[h]))) for h, i in enumerate(ids)]
        state = [state[h] * jnp.exp(ctot[i]) + kv[i] + su[h] for h, i in enumerate(ids)]
        rows = slice(q * c, (q + 1) * c)
        for h, i in enumerate(ids):
            o = so[h][c:] + o0[i] + mu_[h]
            mean = head_sum(o, m0) * inv_n
            oc = o - mean
            var = head_sum(oc * oc, m0) * inv_n
            y = oc * lax.rsqrt(var + RW_GN_EPS) * lng_ref[:, sls[h]] + lnb_ref[:, sls[h]]
            y = y + bonus[h][rows, :]
            o_ref[rows, sls[h]] = (y * g_ref[rows, sls[h]].astype(f32)).astype(o_ref.dtype)
    for h in range(npair):
        state_ref[h] = state[h]


def rwkv_mix(p_rw, lw, a, g, mu, k_k, k_a, r_k, ln_g, ln_b):
    bsz, s, _ = p_rw.shape
    nrow = RW_SUB * RW_CHUNK
    spec = pl.BlockSpec((None, nrow, RW_WIDTH), lambda i, j: (i, j, 0))
    vec = pl.BlockSpec((1, RW_WIDTH), lambda i, j: (0, 0))
    return pl.pallas_call(
        _rwkv_kernel,
        grid=(bsz, s // nrow),
        in_specs=[pl.BlockSpec((None, nrow, 3 * RW_WIDTH), lambda i, j: (i, j, 0)), spec, spec, spec,
                  pl.BlockSpec((1, 3 * RW_WIDTH), lambda i, j: (0, 0)), vec, vec, vec, vec, vec],
        out_specs=spec,
        out_shape=jax.ShapeDtypeStruct((bsz, s, RW_WIDTH), bf16),
        scratch_shapes=[pltpu.VMEM((RW_HEADS // 2, LANES, LANES), f32), pltpu.VMEM((1, 3 * RW_WIDTH), f32)],
        compiler_params=_cparams(("parallel", "arbitrary")),
        name="rwkv_mix",
    )(p_rw, lw, a, g, mu, k_k, k_a, r_k, ln_g, ln_b)


def _sgu_kernel(p_ref, lng_ref, lnb_ref, w_ref, bias_ref, o_ref):
    x = p_ref[...].astype(f32)
    g = 0.5 * x * (1.0 + jnp.tanh(0.7978845608028654 * (x + 0.044715 * (x * x * x))))
    u = g[:, :SG_WIDTH]
    v = g[:, SG_WIDTH:]
    mu = jnp.mean(v, axis=-1, keepdims=True)
    vc = v - mu
    var = jnp.mean(vc * vc, axis=-1, keepdims=True)
    vn = (vc * lax.rsqrt(var + 1e-5) * lng_ref[...] + lnb_ref[...]).astype(bf16)
    first = lax.broadcasted_iota(jnp.int32, (SG_CHUNK, LANES), 1) < (LANES // 2)
    for q in range(SG_GROUPS // 2):
        sl = slice(q * LANES, (q + 1) * LANES)
        vq = vn[:, sl]
        lo = jnp.dot(w_ref[2 * q], vq, preferred_element_type=f32)
        hi = jnp.dot(w_ref[2 * q + 1], vq, preferred_element_type=f32)
        mixed = jnp.where(first, lo, hi) + bias_ref[:, sl]
        o_ref[:, sl] = (u[:, sl] * mixed).astype(o_ref.dtype)


def sgu(p_sg, ln_g, ln_b, w_tril, bias_full):
    t = p_sg.shape[0]
    return pl.pallas_call(
        _sgu_kernel,
        grid=(t // SG_CHUNK,),
        in_specs=[pl.BlockSpec((SG_CHUNK, 2 * SG_WIDTH), lambda i: (i, 0)),
                  pl.BlockSpec((1, SG_WIDTH), lambda i: (0, 0)),
                  pl.BlockSpec((1, SG_WIDTH), lambda i: (0, 0)),
                  pl.BlockSpec((SG_GROUPS, SG_CHUNK, SG_CHUNK), lambda i: (0, 0, 0)),
                  pl.BlockSpec((SG_CHUNK, SG_WIDTH), lambda i: (0, 0))],
        out_specs=pl.BlockSpec((SG_CHUNK, SG_WIDTH), lambda i: (i, 0)),
        out_shape=jax.ShapeDtypeStruct((t, SG_WIDTH), bf16),
        compiler_params=_cparams(("parallel",)),
        name="sgu",
    )(p_sg, ln_g, ln_b, w_tril, bias_full)


MLA_TM = 512
ATT_QK_PAD = 256


def _mla_prep_kernel(p_ref, qg_ref, kvg_ref, wq_ref, wkv_ref, cs_ref, sn_ref, q_ref, kv_ref, kr_ref):
    x = p_ref[...]

    def rms(y, g):
        return y * lax.rsqrt(jnp.mean(y * y, axis=-1, keepdims=True) + NORM_EPS) * g

    q = _dot(rms(x[:, :MLA_Q_LORA], qg_ref[...]), wq_ref[...])
    kv_ref[...] = _dot(rms(x[:, MLA_Q_LORA:MLA_Q_LORA + MLA_KV_LORA], kvg_ref[...]), wkv_ref[...]).astype(bf16)
    cs = cs_ref[...]
    sn = sn_ref[...]
    half = lax.broadcasted_iota(jnp.int32, cs.shape, 1) < (MLA_QK_ROPE // 2)

    def rope(y):
        swapped = jnp.where(half, pltpu.roll(y, LANES - MLA_QK_ROPE // 2, 1), pltpu.roll(y, MLA_QK_ROPE // 2, 1))
        return y * cs + swapped * sn

    kr_ref[...] = rope(x[:, MLA_Q_LORA + MLA_KV_LORA:]).astype(bf16)
    for h in range(MLA_HEADS):
        base = h * ATT_QK_PAD
        q_ref[:, base:base + LANES] = q[:, base:base + LANES].astype(bf16)
        q_ref[:, base + LANES:base + 2 * LANES] = rope(q[:, base + LANES:base + 2 * LANES]).astype(bf16)


def mla_prep(p_at, qg, kvg, wq, wkv, cs, sn):
    t = p_at.shape[0]
    full = lambda shape: pl.BlockSpec(shape, lambda i: (0, 0))
    return pl.pallas_call(
        _mla_prep_kernel,
        grid=(t // MLA_TM,),
        in_specs=[pl.BlockSpec((MLA_TM, MLA_COLS_PAD), lambda i: (i, 0)),
                  full((1, MLA_Q_LORA)), full((1, MLA_KV_LORA)),
                  full((MLA_Q_LORA, MLA_HEADS * ATT_QK_PAD)), full((MLA_KV_LORA, MLA_HEADS * 256)),
                  pl.BlockSpec((MLA_TM, LANES), lambda i: (i, 0)), pl.BlockSpec((MLA_TM, LANES), lambda i: (i, 0))],
        out_specs=[pl.BlockSpec((MLA_TM, MLA_HEADS * ATT_QK_PAD), lambda i: (i, 0)),
                   pl.BlockSpec((MLA_TM, MLA_HEADS * 256), lambda i: (i, 0)),
                   pl.BlockSpec((MLA_TM, LANES), lambda i: (i, 0))],
        out_shape=[jax.ShapeDtypeStruct((t, MLA_HEADS * ATT_QK_PAD), bf16),
                   jax.ShapeDtypeStruct((t, MLA_HEADS * 256), bf16),
                   jax.ShapeDtypeStruct((t, LANES), bf16)],
        compiler_params=_cparams(("parallel",)),
        name="mla_prep",
    )(p_at, qg, kvg, wq, wkv, cs, sn)


ATT_TQ = 1024
ATT_TK = 512


def _attn_kernel(q_ref, kn_ref, kr_ref, v_ref, o_ref):
    i = pl.program_id(2)
    hq = ATT_TQ // 2
    qs = [q_ref[:hq, :], q_ref[hq:, :]]

    def keys(j):
        rows = pl.ds(pl.multiple_of(j * ATT_TK, ATT_TK), ATT_TK)
        return jnp.concatenate([kn_ref[rows, :], kr_ref[rows, :]], axis=1), v_ref[rows, :]

    def update(carry, s, vj):
        m_old, l_old, acc = carry
        m_new = jnp.maximum(m_old, jnp.max(s, axis=-1, keepdims=True))
        p = jnp.exp2(s - m_new)
        alpha = jnp.exp2(m_old - m_new)
        l_new = alpha * l_old + jnp.sum(p, axis=-1, keepdims=True)
        acc = alpha * acc + jnp.dot(p.astype(bf16), vj, preferred_element_type=f32)
        return m_new, l_new, acc

    def body(j, carry):
        ka, va = keys(2 * j)
        kb, vb = keys(2 * j + 1)
        sa = [_dot_nt(qs[h], ka) for h in range(2)]
        sb = [_dot_nt(qs[h], kb) for h in range(2)]
        carry = tuple(update(carry[h], sa[h], va) for h in range(2))
        return tuple(update(carry[h], sb[h], vb) for h in range(2))

    init = tuple((jnp.full((hq, 1), -jnp.inf, f32), jnp.zeros((hq, 1), f32), jnp.zeros((hq, MLA_V_DIM), f32))
                 for _ in range(2))
    carry = lax.fori_loop(0, i, body, init)
    rowi = lax.broadcasted_iota(jnp.int32, (hq, ATT_TK), 0)
    coli = lax.broadcasted_iota(jnp.int32, (hq, ATT_TK), 1)
    diag = coli > rowi
    kj, vj = keys(2 * i)
    c0 = update(carry[0], jnp.where(diag, -jnp.inf, _dot_nt(qs[0], kj)), vj)
    c1 = update(carry[1], _dot_nt(qs[1], kj), vj)
    kj, vj = keys(2 * i + 1)
    c1 = update(c1, jnp.where(diag, -jnp.inf, _dot_nt(qs[1], kj)), vj)
    o_ref[:hq, :] = (c0[2] / c0[1]).astype(o_ref.dtype)
    o_ref[hq:, :] = (c1[2] / c1[1]).astype(o_ref.dtype)


def attention(q, kv, kr):
    b, s, _ = q.shape
    return pl.pallas_call(
        _attn_kernel,
        grid=(b, MLA_HEADS, s // ATT_TQ),
        in_specs=[pl.BlockSpec((None, ATT_TQ, ATT_QK_PAD), lambda bi, h, i: (bi, i, h)),
                  pl.BlockSpec((None, s, MLA_QK_NOPE), lambda bi, h, i: (bi, 0, 2 * h)),
                  pl.BlockSpec((None, s, LANES), lambda bi, h, i: (bi, 0, 0)),
                  pl.BlockSpec((None, s, MLA_V_DIM), lambda bi, h, i: (bi, 0, 2 * h + 1))],
        out_specs=pl.BlockSpec((None, ATT_TQ, MLA_V_DIM), lambda bi, h, i: (bi, i, h)),
        out_shape=jax.ShapeDtypeStruct((b, s, MLA_HEADS * MLA_V_DIM), bf16),
        compiler_params=_cparams(("parallel", "parallel", "parallel")),
        name="attention",
    )(q, kv, kr, kv)


OUT_TM = 512


def _merge_kernel(yr_ref, ys_ref, ya_ref, gate_ref, wr_ref, ws_ref, wa_ref, o_ref):
    d = D_MODEL
    acc = jax.nn.sigmoid(gate_ref[:, :d].astype(f32)) * jnp.dot(yr_ref[...], wr_ref[...], preferred_element_type=f32)
    acc += jax.nn.sigmoid(gate_ref[:, d:2 * d].astype(f32)) * jnp.dot(ys_ref[...], ws_ref[...],
                                                                      preferred_element_type=f32)
    acc += jax.nn.sigmoid(gate_ref[:, 2 * d:].astype(f32)) * jnp.dot(ya_ref[...], wa_ref[...],
                                                                     preferred_element_type=f32)
    o_ref[...] = acc.astype(o_ref.dtype)


def merge(y_rw, y_sg, y_at, p_gate, w_rw, w_sg, w_at):
    t = y_rw.shape[0]
    yspec = pl.BlockSpec((OUT_TM, RW_WIDTH), lambda i: (i, 0))
    wspec = pl.BlockSpec((RW_WIDTH, D_MODEL), lambda i: (0, 0), pipeline_mode=pl.Buffered(1))
    return pl.pallas_call(
        _merge_kernel,
        grid=(t // OUT_TM,),
        in_specs=[yspec, yspec, yspec, pl.BlockSpec((OUT_TM, 3 * D_MODEL), lambda i: (i, 0)), wspec, wspec, wspec],
        out_specs=pl.BlockSpec((OUT_TM, D_MODEL), lambda i: (i, 0)),
        out_shape=jax.ShapeDtypeStruct((t, D_MODEL), bf16),
        compiler_params=_cparams(("parallel",)),
        name="merge",
    )(y_rw, y_sg, y_at, p_gate, w_rw, w_sg, w_at)


def _out_kernel(mix_ref, wo_ref, x_ref, gate_ref, ng_ref, scale_ref, shift_ref, wr_ref, xo_ref, h_ref, lg_ref):
    xn = x_ref[...] + gate_ref[...] * jnp.dot(mix_ref[...], wo_ref[...], preferred_element_type=f32)
    xo_ref[...] = xn
    y = xn * lax.rsqrt(jnp.mean(xn * xn, axis=-1, keepdims=True) + NORM_EPS) * ng_ref[...]
    h = y * (1.0 + scale_ref[...]) + shift_ref[...]
    h_ref[...] = h
    lg_ref[...] = _dot(h, wr_ref[...])


def out_proj(mix, w_o, x, gate, norm_g, scale, shift, w_router, seq):
    t, d = x.shape
    per_b = pl.BlockSpec((None, 1, d), lambda i: (i * OUT_TM // seq, 0, 0))
    rows = pl.BlockSpec((OUT_TM, d), lambda i: (i, 0))
    return pl.pallas_call(
        _out_kernel,
        grid=(t // OUT_TM,),
        in_specs=[rows, pl.BlockSpec((d, d), lambda i: (0, 0), pipeline_mode=pl.Buffered(1)), rows, per_b,
                  pl.BlockSpec((1, d), lambda i: (0, 0)), per_b, per_b,
                  pl.BlockSpec((d, LANES), lambda i: (0, 0))],
        out_specs=[rows, rows, pl.BlockSpec((OUT_TM, LANES), lambda i: (i, 0))],
        out_shape=[jax.ShapeDtypeStruct((t, d), f32), jax.ShapeDtypeStruct((t, d), f32),
                   jax.ShapeDtypeStruct((t, LANES), f32)],
        compiler_params=_cparams(("parallel",)),
        name="out_proj",
    )(mix, w_o, x, gate, norm_g, scale, shift, w_router)


ROW_GROUP = 8


def _moe_kernel(layer, run_ref, rexp_ref, nrow_ref, nu_ref, nr_ref, tok_ref, h_ref, wg_hbm, wu_hbm, wd_hbm, o_ref,
                wg_f, wu_f, wd_f, wgu_s, wd_s, xbuf, gsem, wsem):
    s = pl.program_id(0)
    nblk = pl.num_programs(0) - 1
    cur = jnp.minimum(s, nblk - 1)
    prv = jnp.maximum(s - 1, 0)
    n_used = nu_ref[0]
    n_runs = nr_ref[0]
    compute = (s >= 1) & (s <= n_used)

    def weight_copies(j):
        e = rexp_ref[j]
        k = j % 2
        return [pltpu.make_async_copy(src.at[layer, e], dst.at[k], wsem.at[k])
                for src, dst in ((wg_hbm, wg_f), (wu_hbm, wu_f), (wd_hbm, wd_f))]

    def fetch(j):
        for cp in weight_copies(j):
            cp.start()

    @pl.when(s == 0)
    def _():
        fetch(0)

        @pl.when(n_runs > 1)
        def _():
            fetch(1)

    @pl.when(compute)
    def _():
        b = prv % 2

        def wait_group(i, carry):
            pltpu.make_async_copy(h_ref.at[pl.ds(0, ROW_GROUP), :], xbuf.at[b, pl.ds(0, ROW_GROUP), :],
                                  gsem.at[b]).wait()
            return carry

        lax.fori_loop(0, nrow_ref[prv] // ROW_GROUP, wait_group, 0)

    @pl.when(s < n_used)
    def _():
        b = s % 2

        def issue_group(i, carry):
            for r in range(ROW_GROUP):
                row = i * ROW_GROUP + r
                pltpu.make_async_copy(h_ref.at[pl.ds(tok_ref[0, row], 1), :], xbuf.at[b, pl.ds(row, 1), :],
                                      gsem.at[b]).start(priority=1)
            return carry

        lax.fori_loop(0, nrow_ref[cur] // ROW_GROUP, issue_group, 0)

    @pl.when((s < n_used) & ((s == 0) | (run_ref[cur] != run_ref[prv])))
    def _():
        j = run_ref[cur]
        k = j % 2
        for cp in weight_copies(j):
            cp.wait()
        wgu_s[k, :, :MOE_D_FF] = wg_f[k].astype(bf16)
        wgu_s[k, :, MOE_D_FF:] = wu_f[k].astype(bf16)
        wd_s[k] = wd_f[k].astype(bf16)

        @pl.when(j + 2 < n_runs)
        def _():
            fetch(j + 2)

    @pl.when(compute)
    def _():
        k = run_ref[prv] % 2
        gathered = lax.broadcasted_iota(jnp.int32, (MOE_BLOCK, 1), 0) < nrow_ref[prv]
        x = jnp.where(gathered, xbuf[prv % 2], 0.0).astype(bf16)
        gu = jnp.dot(x, wgu_s[k], preferred_element_type=f32)
        g = gu[:, :MOE_D_FF]
        hmid = (g * jax.nn.sigmoid(g) * gu[:, MOE_D_FF:]).astype(bf16)
        o_ref[...] = jnp.dot(hmid, wd_s[k], preferred_element_type=f32).astype(o_ref.dtype)

    @pl.when(s > n_used)
    def _():
        o_ref[...] = jnp.zeros_like(o_ref)


def moe_ffn(block_expert, run_expert, block_rows, n_used, h, row_token, w_gate, w_up, w_down, layer):
    rows = row_token.shape[0]
    d = h.shape[1]
    nblk = rows // MOE_BLOCK
    changed = jnp.concatenate([jnp.zeros((1,), jnp.int32),
                               (block_expert[1:] != block_expert[:-1]).astype(jnp.int32)])
    run = jnp.cumsum(changed).astype(jnp.int32)
    n_runs = run[jnp.maximum(n_used[0] - 1, 0)].reshape(1) + 1

    def cur_block(s, *_):
        return (jnp.minimum(s, nblk - 1), 0, 0)

    any_spec = pl.BlockSpec(memory_space=pl.ANY)
    return pl.pallas_call(
        functools.partial(_moe_kernel, layer),
        grid_spec=pltpu.PrefetchScalarGridSpec(
            num_scalar_prefetch=5,
            grid=(nblk + 1,),
            in_specs=[pl.BlockSpec((None, 1, MOE_BLOCK), cur_block, memory_space=pltpu.SMEM),
                      any_spec,
                      any_spec, any_spec, any_spec],
            out_specs=pl.BlockSpec((MOE_BLOCK, d), lambda s, *_: (jnp.maximum(s - 1, 0), 0)),
            scratch_shapes=[pltpu.VMEM((2, d, MOE_D_FF), f32), pltpu.VMEM((2, d, MOE_D_FF), f32),
                            pltpu.VMEM((2, MOE_D_FF, d), f32),
                            pltpu.VMEM((2, d, 2 * MOE_D_FF), bf16), pltpu.VMEM((2, MOE_D_FF, d), bf16),
                            pltpu.VMEM((2, MOE_BLOCK, d), f32),
                            pltpu.SemaphoreType.DMA((2,)), pltpu.SemaphoreType.DMA((2,))],
        ),
        out_shape=jax.ShapeDtypeStruct((rows, d), f32),
        compiler_params=_cparams(("arbitrary",)),
        name="moe_ffn",
    )(run, run_expert, block_rows, n_used, n_runs.astype(jnp.int32), row_token.reshape(nblk, 1, MOE_BLOCK), h,
      w_gate, w_up, w_down)


def _rms(x, g):
    return x * lax.rsqrt(jnp.mean(x * x, axis=-1, keepdims=True) + NORM_EPS) * g


def _pad_cols(w, width):
    return jnp.pad(w, ((0, 0), (0, width - w.shape[1])))


def _pad_rows(w, height):
    return jnp.pad(w, ((0, height - w.shape[0]), (0, 0)))


def _moe(h, logits, layer, g_b, e_b, w_gate, w_up, w_down):
    t, d = h.shape
    group_logits = logits[:, :MOE_GROUPS] + g_b
    group = jnp.argmax(group_logits, axis=-1)
    group_w = jnp.take_along_axis(jax.nn.softmax(group_logits, axis=-1), group[:, None], axis=-1)
    exp_logits = (logits[:, MOE_GROUPS:MOE_GROUPS + MOE_EXPERTS] + e_b).reshape(t, MOE_GROUPS, MOE_EPG)
    in_group = jnp.take_along_axis(exp_logits, group[:, None, None], axis=1)[:, 0]
    lane = jnp.arange(MOE_EPG, dtype=jnp.int32)[None, :]
    first = jnp.argmax(in_group, axis=-1).astype(jnp.int32)
    rest = jnp.where(lane == first[:, None], -jnp.inf, in_group)
    second = jnp.argmax(rest, axis=-1).astype(jnp.int32)
    top_idx = jnp.stack([first, second], axis=1)
    top_logit = jnp.stack([jnp.max(in_group, axis=-1), jnp.max(rest, axis=-1)], axis=1)
    weights = (group_w * jax.nn.softmax(top_logit, axis=-1)).reshape(-1)
    expert_ids = (group[:, None] * MOE_EPG + top_idx).reshape(-1).astype(jnp.int32)
    n_assign = t * MOE_TOP_K
    n_blocks = -(-n_assign // MOE_BLOCK) + MOE_EXPERTS
    rows = n_blocks * MOE_BLOCK
    onehot = (expert_ids[:, None] == jnp.arange(MOE_EXPERTS, dtype=jnp.int32)[None, :]).astype(jnp.int32)
    csum = jnp.cumsum(onehot, axis=0)
    rank = jnp.take_along_axis(csum, expert_ids[:, None], axis=1)[:, 0] - 1
    counts = csum[-1]
    padded = (counts + MOE_BLOCK - 1) // MOE_BLOCK * MOE_BLOCK
    ends = jnp.cumsum(padded)
    starts = ends - padded
    dest = starts[expert_ids] + rank
    token_ids = jnp.arange(n_assign, dtype=jnp.int32) // MOE_TOP_K
    row_token = jnp.zeros((rows,), jnp.int32).at[dest].set(token_ids)
    block_start = jnp.arange(n_blocks, dtype=jnp.int32) * MOE_BLOCK
    block_expert = jnp.minimum(jnp.searchsorted(ends, block_start, side='right'),
                               MOE_EXPERTS - 1).astype(jnp.int32)
    n_used = (ends[-1] // MOE_BLOCK).astype(jnp.int32).reshape(1)
    block_valid = jnp.clip(starts[block_expert] + counts[block_expert] - block_start, 0, MOE_BLOCK)
    block_valid = jnp.where(block_start < ends[-1], block_valid, 0)
    block_rows = ((block_valid + ROW_GROUP - 1) // ROW_GROUP * ROW_GROUP).astype(jnp.int32)
    experts = jnp.arange(MOE_EXPERTS, dtype=jnp.int32)
    run_expert = jnp.sort(jnp.where(counts > 0, experts, MOE_EXPERTS)).astype(jnp.int32)
    yb = moe_ffn(block_expert, run_expert, block_rows, n_used, h, row_token, w_gate, w_up, w_down, layer)
    return yb, dest.reshape(t, MOE_TOP_K), weights.reshape(t, MOE_TOP_K)


def _gather_rows(d0_ref, d1_ref, w_ref, yb_ref, buf, sem):
    def issue(r, carry):
        pltpu.make_async_copy(yb_ref.at[pl.ds(d0_ref[0, r], 1), :], buf.at[0, pl.ds(r, 1), :],
                              sem.at[0]).start(priority=0)
        pltpu.make_async_copy(yb_ref.at[pl.ds(d1_ref[0, r], 1), :], buf.at[1, pl.ds(r, 1), :],
                              sem.at[1]).start(priority=1)
        return carry

    lax.fori_loop(0, OUT_TM, issue, 0, unroll=8)
    for k in range(MOE_TOP_K):
        pltpu.make_async_copy(yb_ref.at[pl.ds(0, OUT_TM), :], buf.at[k], sem.at[k]).wait()
    w = w_ref[...]
    return buf[0] * w[:, 0:1] + buf[1] * w[:, 1:2]


def _combine_kernel(d0_ref, d1_ref, w_ref, yb_ref, x_ref, gate_ref, ng_ref, scale_ref, shift_ref, xo_ref, h_ref,
                    buf, sem):
    xn = x_ref[...] + gate_ref[...] * _gather_rows(d0_ref, d1_ref, w_ref, yb_ref, buf, sem)
    xo_ref[...] = xn
    y = xn * lax.rsqrt(jnp.mean(xn * xn, axis=-1, keepdims=True) + NORM_EPS) * ng_ref[...]
    h_ref[...] = (y * (1.0 + scale_ref[...]) + shift_ref[...]).astype(h_ref.dtype)


def _final_kernel(d0_ref, d1_ref, w_ref, yb_ref, x_ref, gate_ref, ng_ref, o_ref, buf, sem):
    xn = x_ref[...] + gate_ref[...] * _gather_rows(d0_ref, d1_ref, w_ref, yb_ref, buf, sem)
    o_ref[...] = xn * lax.rsqrt(jnp.mean(xn * xn, axis=-1, keepdims=True) + NORM_EPS) * ng_ref[...]


def combine(yb, dest, weights, x, gate, norm_g, scale, shift, seq):
    t, d = x.shape
    nb = t // OUT_TM
    per_b = pl.BlockSpec((None, 1, d), lambda i: (i * OUT_TM // seq, 0, 0))
    rows = pl.BlockSpec((OUT_TM, d), lambda i: (i, 0))
    vec = pl.BlockSpec((1, d), lambda i: (0, 0))
    idx = pl.BlockSpec((None, 1, OUT_TM), lambda i: (i, 0, 0), memory_space=pltpu.SMEM)
    wts = pl.BlockSpec((OUT_TM, MOE_TOP_K), lambda i: (i, 0))
    any_spec = pl.BlockSpec(memory_space=pl.ANY)
    scratch = [pltpu.VMEM((MOE_TOP_K, OUT_TM, d), f32), pltpu.SemaphoreType.DMA((MOE_TOP_K,))]
    d0 = dest[:, 0].reshape(nb, 1, OUT_TM)
    d1 = dest[:, 1].reshape(nb, 1, OUT_TM)
    if scale is None:
        return pl.pallas_call(
            _final_kernel, grid=(nb,), in_specs=[idx, idx, wts, any_spec, rows, per_b, vec], out_specs=rows,
            out_shape=jax.ShapeDtypeStruct((t, d), f32), scratch_shapes=scratch,
            compiler_params=_cparams(("arbitrary",)), name="final_norm")(d0, d1, weights, yb, x, gate, norm_g)
    return pl.pallas_call(
        _combine_kernel, grid=(nb,), in_specs=[idx, idx, wts, any_spec, rows, per_b, vec, per_b, per_b],
        out_specs=[rows, rows],
        out_shape=[jax.ShapeDtypeStruct((t, d), f32), jax.ShapeDtypeStruct((t, d), bf16)],
        scratch_shapes=scratch, compiler_params=_cparams(("arbitrary",)),
        name="combine")(d0, d1, weights, yb, x, gate, norm_g, scale, shift)


def kernel(x, c, positions, ada_w, ada_b, norm1_g, norm2_g, final_g, w_in, rw_mu, rw_w0, rw_w2, rw_a0, rw_a2, rw_g2, rw_k_k, rw_k_a, rw_r_k, rw_ln_g, rw_ln_b, sg_ln_g, sg_ln_b, sg_w, sg_b, mla_q_norm_g, mla_w_uq, mla_kv_norm_g, mla_w_ukv, p_rwkv, p_sgu, p_mla, w_o, router_g_w, router_g_b, router_e_w, router_e_b, exp_w_gate, exp_w_up, exp_w_down):
    bsz, seq, d = x.shape
    t = bsz * seq
    half = MLA_QK_ROPE // 2
    inv_freq = ROPE_BASE ** (-jnp.arange(half, dtype=f32) / half)
    ang = (positions.astype(f32)[..., None] * inv_freq).reshape(t, half)
    cos, sin = jnp.cos(ang), jnp.sin(ang)
    zpad = jnp.zeros((t, LANES - MLA_QK_ROPE), f32)
    rope_cs = jnp.concatenate([cos, cos, zpad], axis=1)
    rope_sn = jnp.concatenate([-sin, sin, zpad], axis=1)
    c_act = jnp.pad(jax.nn.silu(c), ((0, 8 - bsz), (0, 0)))
    x = x.reshape(t, d)
    row = lambda vec: vec[None, :]

    def per_batch(vec):
        return vec[:, None, :]

    mods = []
    for l in range(DEPTH):
        mod = matmul(c_act, ada_w, tm=8, tn=1024, layer=l, name="ada")[:bsz] + ada_b[l]
        mods.append(jnp.split(mod, 6, axis=-1))

    shift1, scale1 = mods[0][0], mods[0][1]
    h = ((_rms(x, norm1_g[0]).reshape(bsz, seq, d) * (1.0 + scale1[:, None, :]) + shift1[:, None, :])
         .reshape(t, d).astype(bf16))
    for l in range(DEPTH):
        _, _, gate1, shift2, scale2, gate2 = mods[l]
        wl = w_in[l]
        w_rw = jnp.concatenate([wl[:, :3072], _pad_cols(wl[:, 3072:3168], LANES),
                                _pad_cols(wl[:, 3168:3264], LANES), wl[:, 3264:3520]], axis=1).astype(bf16)
        mul = rw_mu[l]
        mu = jnp.concatenate([mul[:3072], jnp.pad(mul[3072:3168], (0, 32)),
                              jnp.pad(mul[3168:3264], (0, 32)), mul[3264:3520]])
        w_sg = wl[:, 3520:5568].astype(bf16)
        w_at = _pad_cols(wl[:, 5568:6400], MLA_COLS_PAD).astype(bf16)
        w_gt = wl[:, 6400:].astype(bf16)
        p_rw = matmul(h, w_rw, tm=1024, tn=896, name="in_rw").reshape(bsz, seq, RW_COLS_PAD)
        p_sg = matmul(h, w_sg, tm=2048, tn=1024, out_dtype=bf16, name="in_sg")
        p_at = matmul(h, w_at, tm=1024, tn=MLA_COLS_PAD, name="in_at")
        p_gate = matmul(h, w_gt, tm=2048, tn=1024, out_dtype=bf16, name="in_gate")
        lw, a, g = rw_lora(p_rw, row(mu[3 * RW_WIDTH:]), row(rw_w0[l]), row(rw_a0[l]),
                           _pad_rows(rw_w2[l], LANES).astype(bf16), _pad_rows(rw_a2[l], LANES).astype(bf16),
                           rw_g2[l].astype(bf16))
        y_rw = rwkv_mix(p_rw, lw, a, g, row(mu[:3 * RW_WIDTH]), row(rw_k_k[l]), row(rw_k_a[l]),
                        row(rw_r_k[l].reshape(-1)), row(rw_ln_g[l]), row(rw_ln_b[l])).reshape(t, RW_WIDTH)
        bias_full = jnp.repeat(sg_b[l].T, RW_HEAD_DIM, axis=1)
        y_sg = sgu(p_sg, row(sg_ln_g[l]), row(sg_ln_b[l]), jnp.tril(sg_w[l]).astype(bf16), bias_full)
        wq = mla_w_uq[l].reshape(MLA_Q_LORA, MLA_HEADS, MLA_QK_DIM) * (MLA_QK_DIM ** -0.5 * LOG2_E)
        wq = jnp.pad(wq, ((0, 0), (0, 0), (0, ATT_QK_PAD - MLA_QK_DIM))).reshape(MLA_Q_LORA, -1).astype(bf16)
        q, kv, kr = mla_prep(p_at, row(mla_q_norm_g[l]), row(mla_kv_norm_g[l]), wq, mla_w_ukv[l].astype(bf16),
                             rope_cs, rope_sn)
        y_at = attention(q.reshape(bsz, seq, -1), kv.reshape(bsz, seq, -1),
                         kr.reshape(bsz, seq, LANES)).reshape(t, -1)
        mix = merge(y_rw, y_sg, y_at, p_gate, p_rwkv[l].astype(bf16), p_sgu[l].astype(bf16), p_mla[l].astype(bf16))
        w_router = _pad_cols(jnp.concatenate([router_g_w[l], router_e_w[l]], axis=1), LANES).astype(bf16)
        x, h2, logits = out_proj(mix, w_o[l].astype(bf16), x, per_batch(gate1), row(norm2_g[l]), per_batch(scale2),
                                 per_batch(shift2), w_router, seq)
        yb, dest, wts = _moe(h2, logits, l, router_g_b[l], router_e_b[l], exp_w_gate, exp_w_up, exp_w_down)
        if l + 1 < DEPTH:
            x, h = combine(yb, dest, wts, x, per_batch(gate2), row(norm1_g[l + 1]), per_batch(mods[l + 1][1]),
                           per_batch(mods[l + 1][0]), seq)
    return combine(yb, dest, wts, x, per_batch(gate2), row(final_g), None, None, seq).reshape(bsz, seq, d)
```
